```python
import math
import jax, jax.numpy as jnp
from jax import lax
import numpy as np

D_MODEL = 2048
BATCH = 1
SEQ = 16384
DEPTH = 2

N_BRANCH = 4
BRANCH_W = D_MODEL // N_BRANCH
NSA_HEADS = 8
NSA_KV_GROUPS = 2
NSA_HEAD_DIM = BRANCH_W // NSA_HEADS
CMP_LEN = 32
CMP_STRIDE = 16
SLC_BLOCK = 64
SLC_TOP_N = 16
WIN = 512
Q_BLOCK = 128
S5_GROUP = 16
S5_GROUPS = BRANCH_W // S5_GROUP
S5_STATE = 64
POOL_WINDOWS = (2, 4, 8, 16)
POOL_GROUPS = len(POOL_WINDOWS)
POOL_GROUP = BRANCH_W // POOL_GROUPS
RET_HEADS = 4
RET_HEAD_DIM = BRANCH_W // RET_HEADS
RET_CHUNK = 128
D_FF = 4 * D_MODEL
PLE_DIM = 256
NORM_EPS = 1e-6
NEG_INF = -1e30
FORCE_BONUS = 1e4

NSA_Q_COLS = NSA_HEADS * NSA_HEAD_DIM
NSA_KV_COLS = 6 * NSA_KV_GROUPS * NSA_HEAD_DIM
NSA_GATE_COLS = 3 * NSA_HEADS
SPLIT_SIZES = (NSA_Q_COLS, NSA_KV_COLS, NSA_GATE_COLS, BRANCH_W, BRANCH_W,
               BRANCH_W, BRANCH_W, BRANCH_W, BRANCH_W, N_BRANCH * D_MODEL)
IN_COLS = sum(SPLIT_SIZES)

kernel_name = "hybrid_nsa_s5_pool_retention_block"


def rms_norm(x, g):
    x32 = x.astype(jnp.float32)
    y = x32 * lax.rsqrt(jnp.mean(x32 * x32, axis=-1, keepdims=True) + NORM_EPS)
    return (y * g.astype(jnp.float32)).astype(x.dtype)


def masked_softmax(s, mask, axis):
    s = jnp.where(mask, s, NEG_INF)
    e = jnp.exp(s - jnp.max(s, axis=axis, keepdims=True)) * mask
    return e / jnp.maximum(jnp.sum(e, axis=axis, keepdims=True), 1e-30)


def nsa_mixer(q, kv, gates, w_cmp_k, w_cmp_v):
    B, S, _ = q.shape
    H, G, dh = NSA_HEADS, NSA_KV_GROUPS, NSA_HEAD_DIM
    R = H // G
    f32 = jnp.float32
    q = q.reshape(B, S, G, R, dh)
    kv6 = kv.reshape(B, S, 6, G, dh)
    k_cmp, v_cmp, k_slc, v_slc, k_win, v_win = [kv6[:, :, i] for i in range(6)]
    gates = jax.nn.sigmoid(gates.astype(f32)).reshape(B, S, 3, G, R)
    scale = dh ** -0.5

    n_cmp = (S - CMP_LEN) // CMP_STRIDE + 1
    blk_idx = jnp.arange(n_cmp)[:, None] * CMP_STRIDE + jnp.arange(CMP_LEN)[None, :]
    kc = jnp.einsum('bnlgd,lde->bnge', k_cmp[:, blk_idx], w_cmp_k)
    vc = jnp.einsum('bnlgd,lde->bnge', v_cmp[:, blk_idx], w_cmp_v)
    cmp_end = blk_idx[:, -1]

    n_slc = S // SLC_BLOCK
    top_n = min(SLC_TOP_N, n_slc)
    slc_start = jnp.arange(n_slc) * SLC_BLOCK
    overlap = ((blk_idx[:, 0][:, None] < slc_start[None, :] + SLC_BLOCK)
               & (cmp_end[:, None] >= slc_start[None, :])).astype(f32)
    ks_blocks = k_slc.reshape(B, n_slc, SLC_BLOCK, G, dh).transpose(0, 3, 1, 2, 4)
    vs_blocks = v_slc.reshape(B, n_slc, SLC_BLOCK, G, dh).transpose(0, 3, 1, 2, 4)
    b_ix = jnp.arange(B)[:, None, None, None]
    g_ix = jnp.arange(G)[None, :, None, None]

    k_win_p = jnp.pad(k_win, ((0, 0), (WIN, 0), (0, 0), (0, 0)))
    v_win_p = jnp.pad(v_win, ((0, 0), (WIN, 0), (0, 0), (0, 0)))

    def query_block(c):
        start = c * Q_BLOCK
        t = start + jnp.arange(Q_BLOCK)
        qb = lax.dynamic_slice_in_dim(q, start, Q_BLOCK, axis=1)
        gb = lax.dynamic_slice_in_dim(gates, start, Q_BLOCK, axis=1)

        s = jnp.einsum('bqgrd,bngd->bgrqn', qb, kc).astype(f32) * scale
        p_cmp = masked_softmax(s, cmp_end[None, :] <= t[:, None], axis=-1)
        o_cmp = jnp.einsum('bgrqn,bngd->bqgrd', p_cmp.astype(vc.dtype), vc)

        imp = jnp.einsum('bgrqn,nj->bgqj', p_cmp, overlap)
        j = jnp.arange(n_slc)
        forced = (j[None, :] == 0) | (j[None, :] == (t // SLC_BLOCK)[:, None])
        valid = slc_start[None, :] <= t[:, None]
        imp = jnp.where(valid, imp + jnp.where(forced, FORCE_BONUS, 0.0), NEG_INF)
        _, sel = lax.top_k(imp, top_n)
        ks = ks_blocks[b_ix, g_ix, sel]
        vs = vs_blocks[b_ix, g_ix, sel]
        s = jnp.einsum('bqgrd,bgqnld->bgrqnl', qb, ks).astype(f32) * scale
        key_pos = sel[..., None] * SLC_BLOCK + jnp.arange(SLC_BLOCK)
        mask = (key_pos <= t[:, None, None])[:, :, None]
        p_slc = masked_softmax(s, mask, axis=(-2, -1))
        o_slc = jnp.einsum('bgrqnl,bgqnld->bqgrd', p_slc.astype(vs.dtype), vs)

        kw = lax.dynamic_slice_in_dim(k_win_p, start, Q_BLOCK + WIN, axis=1)
        vw = lax.dynamic_slice_in_dim(v_win_p, start, Q_BLOCK + WIN, axis=1)
        pos = start - WIN + jnp.arange(Q_BLOCK + WIN)
        mask = ((pos[None, :] <= t[:, None]) & (pos[None, :] > t[:, None] - WIN)
                & (pos[None, :] >= 0))
        s = jnp.einsum('bqgrd,bkgd->bgrqk', qb, kw).astype(f32) * scale
        p_win = masked_softmax(s, mask, axis=-1)
        o_win = jnp.einsum('bgrqk,bkgd->bqgrd', p_win.astype(vw.dtype), vw)

        o = (gb[:, :, 0, :, :, None] * o_cmp + gb[:, :, 1, :, :, None] * o_slc
             + gb[:, :, 2, :, :, None] * o_win)
        return o.astype(q.dtype)

    out = lax.map(query_block, jnp.arange(S // Q_BLOCK))
    return out.transpose(1, 0, 2, 3, 4, 5).reshape(B, S, H * dh)


def _linear_recurrence(l, r):
    a_l, b_l = l
    a_r, b_r = r
    return a_l * a_r, a_r * b_l + b_r


def s5_mixer(u, a_re, a_im, log_dt, b_re, b_im, c_re, c_im, d_skip, w_glu):
    B, S, _ = u.shape
    f32 = jnp.float32
    u32 = u.astype(f32).reshape(B, S, S5_GROUPS, S5_GROUP)
    A = lax.complex(a_re.astype(f32), a_im.astype(f32))
    dt = jnp.exp(log_dt.astype(f32))[:, None]
    A_bar = jnp.exp(A * dt)
    Bm = lax.complex(b_re.astype(f32), b_im.astype(f32))
    B_bar = ((A_bar - 1.0) / A)[..., None] * Bm
    Bu = jnp.einsum('gnc,bsgc->bsgn', B_bar, u32.astype(jnp.complex64))
    a = jnp.broadcast_to(A_bar, Bu.shape)
    _, h = lax.associative_scan(_linear_recurrence, (a, Bu), axis=1)
    Cm = lax.complex(c_re.astype(f32), c_im.astype(f32))
    y = jnp.real(jnp.einsum('gcn,bsgn->bsgc', Cm, h))
    y = y + d_skip.astype(f32).reshape(S5_GROUPS, S5_GROUP) * u32
    y = jax.nn.gelu(y).reshape(B, S, BRANCH_W).astype(u.dtype)
    ga, gb = jnp.split(y @ w_glu, 2, axis=-1)
    return ga * jax.nn.sigmoid(gb)


def pool_mixer(u, w_pool, pool_scale):
    B, S, _ = u.shape
    f32 = jnp.float32
    u32 = u.astype(f32).reshape(B, S, POOL_GROUPS, POOL_GROUP)
    csum = jnp.concatenate([jnp.zeros((B, 1, POOL_GROUPS, POOL_GROUP), f32),
                            jnp.cumsum(u32, axis=1)], axis=1)
    t = jnp.arange(S)
    pooled = []
    for gi, w in enumerate(POOL_WINDOWS):
        cg = csum[:, :, gi]
        lower = jnp.concatenate([jnp.zeros((B, w - 1, POOL_GROUP), f32), cg[:, :S - w + 1]], axis=1)
        count = jnp.minimum(t + 1, w).astype(f32)[:, None]
        pooled.append((cg[:, 1:] - lower) / count)
    pooled = jnp.stack(pooled, axis=2)
    y = jnp.einsum('bsgc,gcd->bsgd', (pooled - u32).astype(u.dtype), w_pool)
    return y.reshape(B, S, BRANCH_W) * pool_scale


def _rotate_every_two(x):
    x1 = x[..., ::2]
    x2 = x[..., 1::2]
    return jnp.stack((-x2, x1), axis=-1).reshape(x.shape)


def retention_mixer(q, k, v, gate):
    B, S, _ = q.shape
    H, dh, C = RET_HEADS, RET_HEAD_DIM, RET_CHUNK
    f32 = jnp.float32
    q = q.astype(f32).reshape(B, S, H, dh)
    k = k.astype(f32).reshape(B, S, H, dh)
    v = v.astype(f32).reshape(B, S, H, dh)
    angle = jnp.repeat(1.0 / (10000.0 ** jnp.linspace(0.0, 1.0, dh // 2)), 2)
    ang = jnp.arange(S, dtype=f32)[:, None] * angle[None, :]
    sin, cos = jnp.sin(ang)[:, None, :], jnp.cos(ang)[:, None, :]
    q = q * cos + _rotate_every_two(q) * sin
    k = (k * cos + _rotate_every_two(k) * sin) * dh ** -0.5
    n_ch = S // C

    def chunks(x):
        return x.reshape(B, n_ch, C, H, dh).transpose(1, 0, 3, 2, 4)

    log_g = jnp.log(1.0 - 2.0 ** (-5.0 - jnp.arange(H, dtype=f32)))
    idx = jnp.arange(C, dtype=f32)
    rel = idx[:, None] - idx[None, :]
    decay = jnp.where(rel >= 0, jnp.exp(jnp.maximum(rel, 0.0)[None] * log_g[:, None, None]), 0.0)
    xi = jnp.exp((idx + 1.0)[None, :] * log_g[:, None])[..., None]
    zeta = jnp.exp((C - 1.0 - idx)[None, :] * log_g[:, None])[..., None]
    g_chunk = jnp.exp(C * log_g)[:, None, None]

    def step(state, inp):
        qc, kc, vc = inp
        inner = jnp.einsum('bhnm,bhme->bhne', jnp.einsum('bhnd,bhmd->bhnm', qc, kc) * decay, vc)
        cross = jnp.einsum('bhnd,bhde->bhne', qc, state) * xi
        state = state * g_chunk + jnp.einsum('bhmd,bhme->bhde', kc * zeta, vc)
        return state, inner + cross

    state0 = jnp.zeros((B, H, dh, dh), f32)
    _, y = lax.scan(step, state0, (chunks(q), chunks(k), chunks(v)))
    y = y.transpose(1, 0, 3, 2, 4).reshape(B, S, H, dh)
    y = y * lax.rsqrt(jnp.mean(y * y, axis=-1, keepdims=True) + NORM_EPS)
    return (jax.nn.silu(gate.astype(f32)) * y.reshape(B, S, H * dh)).astype(gate.dtype)


def setup_inputs(seed: int = 0) -> dict:
    key = jax.random.key(seed)
    ks = jax.random.split(key, 32)
    f32 = jnp.float32
    L = DEPTH

    def nrm(k, shape, scale):
        return jax.random.normal(k, shape, f32) * scale

    return {
        "x": nrm(ks[0], (BATCH, SEQ, D_MODEL), 1.0),
        "p": nrm(ks[1], (DEPTH, BATCH, SEQ, PLE_DIM), 1.0),
        "g_mix": 1.0 + nrm(ks[2], (L, D_MODEL), 0.1),
        "w_in": nrm(ks[3], (L, D_MODEL, IN_COLS), D_MODEL ** -0.5),
        "w_cmp_k": nrm(ks[4], (L, CMP_LEN, NSA_HEAD_DIM, NSA_HEAD_DIM), (CMP_LEN * NSA_HEAD_DIM) ** -0.5),
        "w_cmp_v": nrm(ks[5], (L, CMP_LEN, NSA_HEAD_DIM, NSA_HEAD_DIM), (CMP_LEN * NSA_HEAD_DIM) ** -0.5),
        "s5_a_re": -0.5 + nrm(ks[6], (L, S5_GROUPS, S5_STATE), 0.01),
        "s5_a_im": math.pi * jnp.arange(S5_STATE, dtype=f32)[None, None, :] + nrm(ks[7], (L, S5_GROUPS, S5_STATE), 0.01),
        "s5_log_dt": jax.random.uniform(ks[8], (L, S5_GROUPS), f32, math.log(1e-3), math.log(1e-1)),
        "s5_b_re": nrm(ks[9], (L, S5_GROUPS, S5_STATE, S5_GROUP), (2 * S5_GROUP) ** -0.5),
        "s5_b_im": nrm(ks[10], (L, S5_GROUPS, S5_STATE, S5_GROUP), (2 * S5_GROUP) ** -0.5),
        "s5_c_re": nrm(ks[11], (L, S5_GROUPS, S5_GROUP, S5_STATE), (2 * S5_STATE) ** -0.5),
        "s5_c_im": nrm(ks[12], (L, S5_GROUPS, S5_GROUP, S5_STATE), (2 * S5_STATE) ** -0.5),
        "s5_d": nrm(ks[13], (L, BRANCH_W), 1.0),
        "s5_w_glu": nrm(ks[14], (L, BRANCH_W, 2 * BRANCH_W), BRANCH_W ** -0.5),
        "pool_w": nrm(ks[15], (L, POOL_GROUPS, POOL_GROUP, POOL_GROUP), POOL_GROUP ** -0.5),
        "pool_scale": 1.0 + nrm(ks[16], (L, BRANCH_W), 0.1),
        "w_branch": nrm(ks[17], (L, N_BRANCH, BRANCH_W, D_MODEL), BRANCH_W ** -0.5),
        "w_out": nrm(ks[18], (L, D_MODEL, D_MODEL), D_MODEL ** -0.5),
        "g_mlp": 1.0 + nrm(ks[19], (L, D_MODEL), 0.1),
        "w_mlp_up": nrm(ks[20], (L, D_MODEL, D_FF), D_MODEL ** -0.5),
        "w_mlp_down": nrm(ks[21], (L, D_FF, D_MODEL), D_FF ** -0.5),
        "w_ple_gate": nrm(ks[22], (L, D_MODEL, D_MODEL), D_MODEL ** -0.5),
        "w_ple_proj": nrm(ks[23], (L, PLE_DIM, D_MODEL), PLE_DIM ** -0.5),
        "g_final": 1.0 + nrm(ks[24], (D_MODEL,), 0.1),
    }


def reference(x, p, g_mix, w_in, w_cmp_k, w_cmp_v, s5_a_re, s5_a_im, s5_log_dt, s5_b_re, s5_b_im,
              s5_c_re, s5_c_im, s5_d, s5_w_glu, pool_w, pool_scale, w_branch, w_out, g_mlp,
              w_mlp_up, w_mlp_down, w_ple_gate, w_ple_proj, g_final):
    B, S, D = x.shape
    offsets = [0]
    for size in SPLIT_SIZES:
        offsets.append(offsets[-1] + size)
    h = x
    for i in range(DEPTH):
        u = rms_norm(h, g_mix[i])
        z = u @ w_in[i]
        (z_nsa_q, z_nsa_kv, z_nsa_g, z_s5, z_pool, z_rq, z_rk, z_rv, z_rg, z_merge) = [
            z[..., offsets[j]:offsets[j + 1]] for j in range(len(SPLIT_SIZES))]

        o_nsa = nsa_mixer(z_nsa_q, z_nsa_kv, z_nsa_g, w_cmp_k[i], w_cmp_v[i])
        o_s5 = s5_mixer(z_s5, s5_a_re[i], s5_a_im[i], s5_log_dt[i], s5_b_re[i], s5_b_im[i],
                        s5_c_re[i], s5_c_im[i], s5_d[i], s5_w_glu[i])
        o_pool = pool_mixer(z_pool, pool_w[i], pool_scale[i])
        o_ret = retention_mixer(z_rq, z_rk, z_rv, z_rg)

        merge_gates = jax.nn.sigmoid(z_merge.reshape(B, S, N_BRANCH, D))
        merged = jnp.zeros_like(h)
        for j, o in enumerate((o_nsa, o_s5, o_pool, o_ret)):
            merged = merged + merge_gates[:, :, j] * (o @ w_branch[i, j])
        h = h + merged @ w_out[i]

        v = rms_norm(h, g_mlp[i])
        h = h + jnp.square(jax.nn.relu(v @ w_mlp_up[i])) @ w_mlp_down[i]

        h = h + jax.nn.sigmoid(h @ w_ple_gate[i]) * (p[i] @ w_ple_proj[i])
    return rms_norm(h, g_final)
```

```python
import functools
import math

import jax
import jax.numpy as jnp
from jax import lax
from jax.experimental import pallas as pl
from jax.experimental.pallas import tpu as pltpu

F32 = jnp.float32
BF16 = jnp.bfloat16

D_MODEL = 2048
N_BRANCH = 4
BRANCH_W = D_MODEL // N_BRANCH
NSA_HEADS = 8
NSA_KV_GROUPS = 2
NSA_HEAD_DIM = BRANCH_W // NSA_HEADS
CMP_LEN = 32
CMP_STRIDE = 16
SLC_BLOCK = 64
SLC_TOP_N = 16
WIN = 512
Q_BLOCK = 128
S5_GROUP = 16
S5_GROUPS = BRANCH_W // S5_GROUP
S5_STATE = 64
S5_COLS = S5_GROUPS * S5_STATE
POOL_WINDOWS = (2, 4, 8, 16)
POOL_GROUP = BRANCH_W // len(POOL_WINDOWS)
POOL_HALO = 16
RET_HEADS = 4
RET_HEAD_DIM = BRANCH_W // RET_HEADS
RET_CHUNK = 128
D_FF = 4 * D_MODEL
PLE_DIM = 256
NORM_EPS = 1e-6
NEG_INF = -1e30
FORCE_BONUS = 1e4

NSA_KV_COLS = 6 * NSA_KV_GROUPS * NSA_HEAD_DIM
NSA_GATE_COLS = 3 * NSA_HEADS
OFF_Q = 0
OFF_KV = OFF_Q + BRANCH_W
OFF_G = OFF_KV + NSA_KV_COLS
OFF_S5 = OFF_G + NSA_GATE_COLS
OFF_POOL = OFF_S5 + BRANCH_W
OFF_RQ = OFF_POOL + BRANCH_W
OFF_RK = OFF_RQ + BRANCH_W
OFF_RV = OFF_RK + BRANCH_W
OFF_RG = OFF_RV + BRANCH_W
OFF_MERGE = OFF_RG + BRANCH_W

LANE = 128
SUBLANE = 8
GATE_PAD = LANE
ZA_Q, ZA_S5, ZA_POOL, ZA_RQ, ZA_RK, ZA_RV, ZA_RG = range(7)
ZA_KV_OFF = 7 * BRANCH_W
ZA_G_OFF = ZA_KV_OFF + NSA_KV_COLS
ZA_COLS = ZA_G_OFF + GATE_PAD

VMEM_LIMIT = 56 * 1024 * 1024


def _params(*sem):
    return pltpu.CompilerParams(dimension_semantics=sem, vmem_limit_bytes=VMEM_LIMIT)


def _rms(x, g):
    return x * lax.rsqrt(jnp.mean(x * x, axis=-1, keepdims=True) + NORM_EPS) * g


def _sigmoid(x):
    return 1.0 / (1.0 + jnp.exp(-x))


def _inproj_kernel(h_ref, g_ref, w_ref, z_ref, u_ref):
    @pl.when(pl.program_id(1) == 0)
    def _():
        u_ref[...] = _rms(h_ref[...], g_ref[...]).astype(BF16)

    z_ref[...] = jnp.dot(u_ref[...], w_ref[...], preferred_element_type=F32)


def _inproj(h, g, w, *, tm, tn):
    S, D = h.shape
    N = w.shape[1]
    return pl.pallas_call(
        _inproj_kernel,
        grid=(S // tm, N // tn),
        in_specs=[pl.BlockSpec((tm, D), lambda i, j: (i, 0)),
                  pl.BlockSpec((1, D), lambda i, j: (0, 0)),
                  pl.BlockSpec((D, tn), lambda i, j: (0, j))],
        out_specs=[pl.BlockSpec((tm, tn), lambda i, j: (i, j)),
                   pl.BlockSpec((tm, D), lambda i, j: (i, 0))],
        out_shape=[jax.ShapeDtypeStruct((S, N), F32), jax.ShapeDtypeStruct((S, D), BF16)],
        compiler_params=_params("parallel", "arbitrary"),
        name="inproj",
    )(h, g, w)


def _merge_kernel(u_ref, o0_ref, o1_ref, o2_ref, o3_ref, wm0_ref, wm1_ref, wm2_ref, wm3_ref,
                  wb_ref, out_ref):
    u = u_ref[...]
    acc = None
    for j, (o_ref, wm_ref) in enumerate(((o0_ref, wm0_ref), (o1_ref, wm1_ref),
                                         (o2_ref, wm2_ref), (o3_ref, wm3_ref))):
        gate = _sigmoid(jnp.dot(u, wm_ref[...], preferred_element_type=F32))
        br = jnp.dot(o_ref[...], wb_ref[j], preferred_element_type=F32)
        acc = gate * br if acc is None else acc + gate * br
    out_ref[...] = acc.astype(BF16)


def _merge(u, outs, wm, wb, *, tm, tn):
    S, D = u.shape
    nt = D // tn
    o_specs = [pl.BlockSpec((tm, BRANCH_W), lambda i, c: (i, 0)) for _ in range(N_BRANCH)]
    wm_specs = [pl.BlockSpec((D, tn), functools.partial(lambda i, c, j: (0, j * nt + c), j=j))
                for j in range(N_BRANCH)]
    return pl.pallas_call(
        _merge_kernel,
        grid=(S // tm, nt),
        in_specs=[pl.BlockSpec((tm, D), lambda i, c: (i, 0))] + o_specs + wm_specs
                 + [pl.BlockSpec((N_BRANCH, BRANCH_W, tn), lambda i, c: (0, 0, c))],
        out_specs=pl.BlockSpec((tm, tn), lambda i, c: (i, c)),
        out_shape=jax.ShapeDtypeStruct((S, D), BF16),
        compiler_params=_params("parallel", "arbitrary"),
        name="merge",
    )(u, *outs, wm, wm, wm, wm, wb)


def _resid_mm_kernel(h_ref, x_ref, w_ref, out_ref):
    out_ref[...] = h_ref[...] + jnp.dot(x_ref[...], w_ref[...], preferred_element_type=F32)


def _resid_mm(h, x, w, *, tm, tn):
    S, D = h.shape
    K = x.shape[1]
    return pl.pallas_call(
        _resid_mm_kernel,
        grid=(S // tm, D // tn),
        in_specs=[pl.BlockSpec((tm, tn), lambda i, c: (i, c)),
                  pl.BlockSpec((tm, K), lambda i, c: (i, 0)),
                  pl.BlockSpec((K, tn), lambda i, c: (0, c))],
        out_specs=pl.BlockSpec((tm, tn), lambda i, c: (i, c)),
        out_shape=jax.ShapeDtypeStruct((S, D), F32),
        compiler_params=_params("parallel", "arbitrary"),
        name="resid_mm",
    )(h, x, w)


def _mlp_kernel(h_ref, g_ref, wu_ref, wd_ref, out_ref, v_ref):
    @pl.when(pl.program_id(1) == 0)
    def _():
        h = h_ref[...]
        v_ref[...] = _rms(h, g_ref[...]).astype(BF16)
        out_ref[...] = h

    hid = jnp.maximum(jnp.dot(v_ref[...], wu_ref[...], preferred_element_type=F32), 0.0)
    out_ref[...] += jnp.dot((hid * hid).astype(BF16), wd_ref[...], preferred_element_type=F32)


def _mlp(h, g, wu, wd, *, tm, tf):
    S, D = h.shape
    return pl.pallas_call(
        _mlp_kernel,
        grid=(S // tm, wu.shape[1] // tf),
        in_specs=[pl.BlockSpec((tm, D), lambda i, f: (i, 0)),
                  pl.BlockSpec((1, D), lambda i, f: (0, 0)),
                  pl.BlockSpec((D, tf), lambda i, f: (0, f)),
                  pl.BlockSpec((tf, D), lambda i, f: (f, 0))],
        out_specs=pl.BlockSpec((tm, D), lambda i, f: (i, 0)),
        out_shape=jax.ShapeDtypeStruct((S, D), F32),
        scratch_shapes=[pltpu.VMEM((tm, D), BF16)],
        compiler_params=_params("parallel", "arbitrary"),
        name="mlp",
    )(h, g, wu, wd)


def _ple_kernel(h_ref, p_ref, wg_ref, wp_ref, gf_ref, out_ref, *, final_norm):
    h = h_ref[...]
    gate = _sigmoid(jnp.dot(h.astype(BF16), wg_ref[...], preferred_element_type=F32))
    emb = jnp.dot(p_ref[...].astype(BF16), wp_ref[...], preferred_element_type=F32)
    y = h + gate * emb
    out_ref[...] = _rms(y, gf_ref[...]) if final_norm else y


def _ple(h, p, wg, wp, gf, *, tm, final_norm):
    S, D = h.shape
    P = p.shape[1]
    return pl.pallas_call(
        functools.partial(_ple_kernel, final_norm=final_norm),
        grid=(S // tm,),
        in_specs=[pl.BlockSpec((tm, D), lambda i: (i, 0)),
                  pl.BlockSpec((tm, P), lambda i: (i, 0)),
                  pl.BlockSpec((D, D), lambda i: (0, 0)),
                  pl.BlockSpec((P, D), lambda i: (0, 0)),
                  pl.BlockSpec((1, D), lambda i: (0, 0))],
        out_specs=pl.BlockSpec((tm, D), lambda i: (i, 0)),
        out_shape=jax.ShapeDtypeStruct((S, D), F32),
        compiler_params=_params("parallel"),
        name="ple",
    )(h, p, wg, wp, gf)


S5_CW = 256
T_A1, T_A2, T_A4, T_ROW, T_A8 = range(5)


def _cmul_add(xr, xi, ar, ai, br, bi):
    return xr + (ar * br - ai * bi), xi + (ar * bi + ai * br)


def _s5_kernel(u_ref, bre_ref, bim_ref, tab_ref, cre_ref, cim_ref, d_ref, wglu_ref, out_ref,
               hre_ref, him_ref, carry_ref):
    tm = u_ref.shape[0]

    @pl.when(pl.program_id(0) == 0)
    def _():
        carry_ref[...] = jnp.zeros_like(carry_ref)

    u = u_ref[...]
    ub = u.astype(BF16)
    hre_ref[...] = jnp.dot(ub, bre_ref[...], preferred_element_type=F32)
    him_ref[...] = jnp.dot(ub, bim_ref[...], preferred_element_type=F32)

    for cb in range(S5_COLS // S5_CW):
        cols = slice(cb * S5_CW, (cb + 1) * S5_CW)
        tabs = [(tab_ref[k, 0, :, cols], tab_ref[k, 1, :, cols]) for k in range(5)]

        def body(r, carry, cols=cols, tabs=tabs):
            cr, ci = carry
            rows = pl.ds(pl.multiple_of(r * SUBLANE, SUBLANE), SUBLANE)
            xr = hre_ref[rows, cols]
            xi = him_ref[rows, cols]
            for k, sh in ((T_A1, 1), (T_A2, 2), (T_A4, 4)):
                ar, ai = tabs[k]
                xr, xi = _cmul_add(xr, xi, ar, ai, pltpu.roll(xr, sh, 0), pltpu.roll(xi, sh, 0))
            ar, ai = tabs[T_ROW]
            hr, hi = _cmul_add(xr, xi, ar, ai, cr, ci)
            hre_ref[rows, cols] = hr
            him_ref[rows, cols] = hi
            ar, ai = tabs[T_A8]
            lr = jnp.broadcast_to(xr[SUBLANE - 1:SUBLANE, :], xr.shape)
            li = jnp.broadcast_to(xi[SUBLANE - 1:SUBLANE, :], xi.shape)
            return _cmul_add(lr, li, ar, ai, cr, ci)

        cr, ci = lax.fori_loop(0, tm // SUBLANE, body,
                               (carry_ref[0, :, cols], carry_ref[1, :, cols]), unroll=4)
        carry_ref[0, :, cols] = cr
        carry_ref[1, :, cols] = ci

    y = (jnp.dot(hre_ref[...].astype(BF16), cre_ref[...], preferred_element_type=F32)
         + jnp.dot(him_ref[...].astype(BF16), cim_ref[...], preferred_element_type=F32))
    y = y + d_ref[...] * u
    y = 0.5 * y * (1.0 + jnp.tanh(math.sqrt(2.0 / math.pi) * (y + 0.044715 * (y * y * y))))
    gl = jnp.dot(y.astype(BF16), wglu_ref[...], preferred_element_type=F32)
    out_ref[...] = (gl[:, :BRANCH_W] * _sigmoid(gl[:, BRANCH_W:])).astype(BF16)


def _s5_tables(a_re, a_im, log_dt, b_re, b_im, c_re, c_im):
    A = lax.complex(a_re, a_im)
    dt = jnp.exp(log_dt)[:, None]
    Adt = A * dt
    A_bar = jnp.exp(Adt)
    Bm = lax.complex(b_re, b_im)
    B_bar = ((A_bar - 1.0) / A)[..., None] * Bm

    def power(k):
        return jnp.exp(Adt * k).reshape(S5_COLS)

    row = jnp.arange(SUBLANE)
    planes = []
    for sh in (1, 2, 4):
        planes.append(jnp.where((row >= sh)[:, None], power(float(sh))[None, :], 0.0))
    planes.append(jnp.exp(Adt.reshape(1, S5_COLS) * (row + 1.0)[:, None].astype(F32)))
    planes.append(jnp.broadcast_to(power(float(SUBLANE))[None, :], (SUBLANE, S5_COLS)))
    tab = jnp.stack([jnp.stack([jnp.real(p), jnp.imag(p)]) for p in planes]).astype(F32)

    eye = jnp.eye(S5_GROUPS, dtype=F32)
    def bdiag_in(m):
        return jnp.einsum('gnc,gh->gchn', m, eye).reshape(BRANCH_W, S5_COLS)

    def bdiag_out(m):
        return jnp.einsum('gcn,gh->gnhc', m, eye).reshape(S5_COLS, BRANCH_W)

    return (tab, bdiag_in(jnp.real(B_bar)).astype(BF16), bdiag_in(jnp.imag(B_bar)).astype(BF16),
            bdiag_out(c_re).astype(BF16), bdiag_out(-c_im).astype(BF16))


def _s5_mixer(za, tabs, d_skip, w_glu, *, tm):
    S = za.shape[0]
    tab, bre, bim, cre, cim = tabs
    full = lambda shape: pl.BlockSpec(shape, lambda i: (0,) * len(shape))
    return pl.pallas_call(
        _s5_kernel,
        grid=(S // tm,),
        in_specs=[pl.BlockSpec((tm, BRANCH_W), lambda i: (i, ZA_S5)),
                  full(bre.shape), full(bim.shape), full(tab.shape), full(cre.shape),
                  full(cim.shape), full((1, BRANCH_W)), full(w_glu.shape)],
        out_specs=pl.BlockSpec((tm, BRANCH_W), lambda i: (i, 0)),
        out_shape=jax.ShapeDtypeStruct((S, BRANCH_W), BF16),
        scratch_shapes=[pltpu.VMEM((tm, S5_COLS), F32), pltpu.VMEM((tm, S5_COLS), F32),
                        pltpu.VMEM((2, SUBLANE, S5_COLS), F32)],
        compiler_params=_params("arbitrary"),
        name="s5",
    )(za, bre, bim, tab, cre, cim, d_skip.reshape(1, BRANCH_W), w_glu)


def _pool_kernel(x_ref, w_ref, scale_ref, out_ref, ext_ref):
    tm = x_ref.shape[0]
    i = pl.program_id(0)

    @pl.when(i == 0)
    def _():
        ext_ref[0:POOL_HALO, :] = jnp.zeros((POOL_HALO, BRANCH_W), F32)

    x = x_ref[...]
    ext_ref[POOL_HALO:, :] = x
    t = i * tm + lax.broadcasted_iota(jnp.int32, (tm, 1), 0)
    for gi, w in enumerate(POOL_WINDOWS):
        cols = slice(gi * POOL_GROUP, (gi + 1) * POOL_GROUP)
        acc = x[:, cols]
        for d in range(1, w):
            acc = acc + ext_ref[POOL_HALO - d:POOL_HALO - d + tm, cols]
        count = jnp.minimum(t + 1, w).astype(F32)
        diff = (acc / count - x[:, cols]).astype(BF16)
        y = jnp.dot(diff, w_ref[gi], preferred_element_type=F32)
        out_ref[:, cols] = (y * scale_ref[:, cols]).astype(BF16)
    ext_ref[0:POOL_HALO, :] = x[tm - POOL_HALO:, :]


def _pool_mixer(za, w_pool, pool_scale, *, tm):
    S = za.shape[0]
    return pl.pallas_call(
        _pool_kernel,
        grid=(S // tm,),
        in_specs=[pl.BlockSpec((tm, BRANCH_W), lambda i: (i, ZA_POOL)),
                  pl.BlockSpec(w_pool.shape, lambda i: (0, 0, 0)),
                  pl.BlockSpec((1, BRANCH_W), lambda i: (0, 0))],
        out_specs=pl.BlockSpec((tm, BRANCH_W), lambda i: (i, 0)),
        out_shape=jax.ShapeDtypeStruct((S, BRANCH_W), BF16),
        scratch_shapes=[pltpu.VMEM((tm + POOL_HALO, BRANCH_W), F32)],
        compiler_params=_params("arbitrary"),
        name="pool",
    )(za, w_pool, pool_scale.reshape(1, BRANCH_W))


def _ret_consts():
    H, dh, C = RET_HEADS, RET_HEAD_DIM, RET_CHUNK
    angle = jnp.repeat(1.0 / (10000.0 ** jnp.linspace(0.0, 1.0, dh // 2)), 2).reshape(1, dh)
    log_g = jnp.log(1.0 - 2.0 ** (-5.0 - jnp.arange(H, dtype=F32)))
    idx = jnp.arange(C, dtype=F32)
    rel = idx[:, None] - idx[None, :]
    decay = jnp.where(rel >= 0, jnp.exp(jnp.maximum(rel, 0.0)[None] * log_g[:, None, None]), 0.0)
    xi = jnp.exp((idx + 1.0)[None, :] * log_g[:, None])[..., None]
    zeta = jnp.exp((C - 1.0 - idx)[None, :] * log_g[:, None])[..., None]
    g_chunk = jnp.broadcast_to(jnp.exp(C * log_g)[:, None, None], (H, 1, dh))
    return angle.astype(F32), decay, xi, zeta, g_chunk


def _rotate_pairs(x, even):
    n = x.shape[-1]
    return jnp.where(even, -pltpu.roll(x, n - 1, 1), pltpu.roll(x, 1, 1))


def _ret_kernel(q_ref, k_ref, v_ref, g_ref, ang_ref, decay_ref, xi_ref, zeta_ref, gch_ref,
                out_ref, state_ref):
    C, dh = RET_CHUNK, RET_HEAD_DIM
    i = pl.program_id(0)

    @pl.when(i == 0)
    def _():
        state_ref[...] = jnp.zeros_like(state_ref)

    pos = (i * C + lax.broadcasted_iota(jnp.int32, (C, 1), 0)).astype(F32)
    ang = pos * ang_ref[...]
    sin, cos = jnp.sin(ang), jnp.cos(ang)
    even = (lax.broadcasted_iota(jnp.int32, (C, dh), 1) % 2) == 0
    for hd in range(RET_HEADS):
        cols = slice(hd * dh, (hd + 1) * dh)
        q = q_ref[:, cols]
        k = k_ref[:, cols]
        vb = v_ref[:, cols].astype(BF16)
        q = q * cos + _rotate_pairs(q, even) * sin
        k = (k * cos + _rotate_pairs(k, even) * sin) * dh ** -0.5
        qb = q.astype(BF16)
        s = lax.dot_general(qb, k.astype(BF16), (((1,), (1,)), ((), ())),
                            preferred_element_type=F32) * decay_ref[hd]
        inner = jnp.dot(s.astype(BF16), vb, preferred_element_type=F32)
        state = state_ref[hd]
        cross = jnp.dot(qb, state.astype(BF16), preferred_element_type=F32) * xi_ref[hd]
        kz = (k * zeta_ref[hd]).astype(BF16)
        state_ref[hd] = state * gch_ref[hd] + lax.dot_general(
            kz, vb, (((0,), (0,)), ((), ())), preferred_element_type=F32)
        y = inner + cross
        y = y * lax.rsqrt(jnp.mean(y * y, axis=-1, keepdims=True) + NORM_EPS)
        g = g_ref[:, cols]
        out_ref[:, cols] = (g * _sigmoid(g) * y).astype(BF16)


def _ret_mixer(za):
    S = za.shape[0]
    C = RET_CHUNK
    consts = _ret_consts()
    col = lambda c: pl.BlockSpec((C, BRANCH_W), lambda i: (i, c))
    full = lambda a: pl.BlockSpec(a.shape, lambda i: (0,) * a.ndim)
    return pl.pallas_call(
        _ret_kernel,
        grid=(S // C,),
        in_specs=[col(ZA_RQ), col(ZA_RK), col(ZA_RV), col(ZA_RG)] + [full(a) for a in consts],
        out_specs=pl.BlockSpec((C, BRANCH_W), lambda i: (i, 0)),
        out_shape=jax.ShapeDtypeStruct((S, BRANCH_W), BF16),
        scratch_shapes=[pltpu.VMEM((RET_HEADS, RET_HEAD_DIM, RET_HEAD_DIM), F32)],
        compiler_params=_params("arbitrary"),
        name="retention",
    )(za, za, za, za, *consts)


def _masked_softmax(s, mask, axis):
    s = jnp.where(mask, s, NEG_INF)
    e = jnp.exp(s - jnp.max(s, axis=axis, keepdims=True)) * mask
    return e / jnp.maximum(jnp.sum(e, axis=axis, keepdims=True), 1e-30)


def _nsa_mixer_jax(q, kv, gates, w_cmp_k, w_cmp_v):
    B, S, _ = q.shape
    H, G, dh = NSA_HEADS, NSA_KV_GROUPS, NSA_HEAD_DIM
    R = H // G
    q = q.reshape(B, S, G, R, dh)
    kv6 = kv.reshape(B, S, 6, G, dh)
    k_cmp, v_cmp, k_slc, v_slc, k_win, v_win = [kv6[:, :, i] for i in range(6)]
    gates = jax.nn.sigmoid(gates.astype(F32)).reshape(B, S, 3, G, R)
    scale = dh ** -0.5
    n_cmp = (S - CMP_LEN) // CMP_STRIDE + 1
    blk_idx = jnp.arange(n_cmp)[:, None] * CMP_STRIDE + jnp.arange(CMP_LEN)[None, :]
    kc = jnp.einsum('bnlgd,lde->bnge', k_cmp[:, blk_idx], w_cmp_k)
    vc = jnp.einsum('bnlgd,lde->bnge', v_cmp[:, blk_idx], w_cmp_v)
    cmp_end = blk_idx[:, -1]
    n_slc = S // SLC_BLOCK
    top_n = min(SLC_TOP_N, n_slc)
    slc_start = jnp.arange(n_slc) * SLC_BLOCK
    overlap = ((blk_idx[:, 0][:, None] < slc_start[None, :] + SLC_BLOCK)
               & (cmp_end[:, None] >= slc_start[None, :])).astype(F32)
    ks_blocks = k_slc.reshape(B, n_slc, SLC_BLOCK, G, dh).transpose(0, 3, 1, 2, 4)
    vs_blocks = v_slc.reshape(B, n_slc, SLC_BLOCK, G, dh).transpose(0, 3, 1, 2, 4)
    b_ix = jnp.arange(B)[:, None, None, None]
    g_ix = jnp.arange(G)[None, :, None, None]
    k_win_p = jnp.pad(k_win, ((0, 0), (WIN, 0), (0, 0), (0, 0)))
    v_win_p = jnp.pad(v_win, ((0, 0), (WIN, 0), (0, 0), (0, 0)))

    def query_block(c):
        start = c * Q_BLOCK
        t = start + jnp.arange(Q_BLOCK)
        qb = lax.dynamic_slice_in_dim(q, start, Q_BLOCK, axis=1)
        gb = lax.dynamic_slice_in_dim(gates, start, Q_BLOCK, axis=1)
        s = jnp.einsum('bqgrd,bngd->bgrqn', qb, kc).astype(F32) * scale
        p_cmp = _masked_softmax(s, cmp_end[None, :] <= t[:, None], axis=-1)
        o_cmp = jnp.einsum('bgrqn,bngd->bqgrd', p_cmp.astype(vc.dtype), vc)
        imp = jnp.einsum('bgrqn,nj->bgqj', p_cmp, overlap)
        j = jnp.arange(n_slc)
        forced = (j[None, :] == 0) | (j[None, :] == (t // SLC_BLOCK)[:, None])
        valid = slc_start[None, :] <= t[:, None]
        imp = jnp.where(valid, imp + jnp.where(forced, FORCE_BONUS, 0.0), NEG_INF)
        _, sel = lax.top_k(imp, top_n)
        ks = ks_blocks[b_ix, g_ix, sel]
        vs = vs_blocks[b_ix, g_ix, sel]
        s = jnp.einsum('bqgrd,bgqnld->bgrqnl', qb, ks).astype(F32) * scale
        key_pos = sel[..., None] * SLC_BLOCK + jnp.arange(SLC_BLOCK)
        mask = (key_pos <= t[:, None, None])[:, :, None]
        p_slc = _masked_softmax(s, mask, axis=(-2, -1))
        o_slc = jnp.einsum('bgrqnl,bgqnld->bqgrd', p_slc.astype(vs.dtype), vs)
        kw = lax.dynamic_slice_in_dim(k_win_p, start, Q_BLOCK + WIN, axis=1)
        vw = lax.dynamic_slice_in_dim(v_win_p, start, Q_BLOCK + WIN, axis=1)
        pos = start - WIN + jnp.arange(Q_BLOCK + WIN)
        mask = ((pos[None, :] <= t[:, None]) & (pos[None, :] > t[:, None] - WIN)
                & (pos[None, :] >= 0))
        s = jnp.einsum('bqgrd,bkgd->bgrqk', qb, kw).astype(F32) * scale
        p_win = _masked_softmax(s, mask, axis=-1)
        o_win = jnp.einsum('bgrqk,bkgd->bqgrd', p_win.astype(vw.dtype), vw)
        o = (gb[:, :, 0, :, :, None] * o_cmp + gb[:, :, 1, :, :, None] * o_slc
             + gb[:, :, 2, :, :, None] * o_win)
        return o.astype(q.dtype)

    out = lax.map(query_block, jnp.arange(S // Q_BLOCK))
    return out.transpose(1, 0, 2, 3, 4, 5).reshape(B, S, H * dh)


def _nsa_mixer(za, w_cmp_k, w_cmp_v):
    q = za[None, :, :BRANCH_W]
    kv = za[None, :, ZA_KV_OFF:ZA_KV_OFF + NSA_KV_COLS]
    gates = za[None, :, ZA_G_OFF:ZA_G_OFF + NSA_GATE_COLS]
    return _nsa_mixer_jax(q, kv, gates, w_cmp_k, w_cmp_v)[0].astype(BF16)


def _pack_w_in(w):
    seg = lambda off, n: w[:, off:off + n]
    gates = jnp.pad(seg(OFF_G, NSA_GATE_COLS), ((0, 0), (0, GATE_PAD - NSA_GATE_COLS)))
    wa = jnp.concatenate([seg(OFF_Q, BRANCH_W), seg(OFF_S5, BRANCH_W), seg(OFF_POOL, BRANCH_W),
                          seg(OFF_RQ, BRANCH_W), seg(OFF_RK, BRANCH_W), seg(OFF_RV, BRANCH_W),
                          seg(OFF_RG, BRANCH_W), seg(OFF_KV, NSA_KV_COLS), gates], axis=1)
    return wa.astype(BF16), w[:, OFF_MERGE:].astype(BF16)


def kernel(x, p, g_mix, w_in, w_cmp_k, w_cmp_v, s5_a_re, s5_a_im, s5_log_dt, s5_b_re, s5_b_im,
           s5_c_re, s5_c_im, s5_d, s5_w_glu, pool_w, pool_scale, w_branch, w_out, g_mlp,
           w_mlp_up, w_mlp_down, w_ple_gate, w_ple_proj, g_final):
    B, S, D = x.shape
    assert B == 1 and D == D_MODEL
    depth = w_in.shape[0]
    h = x.reshape(S, D)
    for i in range(depth):
        wa, wm = _pack_w_in(w_in[i])
        za, u = _inproj(h, g_mix[i].reshape(1, D), wa, tm=1024, tn=640)

        o_nsa = _nsa_mixer(za, w_cmp_k[i], w_cmp_v[i])
        s5_tabs = _s5_tables(s5_a_re[i], s5_a_im[i], s5_log_dt[i], s5_b_re[i], s5_b_im[i],
                             s5_c_re[i], s5_c_im[i])
        o_s5 = _s5_mixer(za, s5_tabs, s5_d[i], s5_w_glu[i].astype(BF16), tm=512)
        o_pool = _pool_mixer(za, pool_w[i].astype(BF16), pool_scale[i], tm=1024)
        o_ret = _ret_mixer(za)

        merged = _merge(u, (o_nsa, o_s5, o_pool, o_ret), wm, w_branch[i].astype(BF16),
                        tm=1024, tn=256)
        h = _resid_mm(h, merged, w_out[i].astype(BF16), tm=1024, tn=1024)
        h = _mlp(h, g_mlp[i].reshape(1, D), w_mlp_up[i].astype(BF16),
                 w_mlp_down[i].astype(BF16), tm=512, tf=512)
        h = _ple(h, p[i].reshape(S, PLE_DIM), w_ple_gate[i].astype(BF16),
                 w_ple_proj[i].astype(BF16), g_final.reshape(1, D), tm=512,
                 final_norm=(i == depth - 1))
    return h.reshape(B, S, D)
```

```python
import functools
import math

import jax
import jax.numpy as jnp
from jax import lax
from jax.experimental import pallas as pl
from jax.experimental.pallas import tpu as pltpu

F32 = jnp.float32
BF16 = jnp.bfloat16
MX = BF16

D_MODEL = 2048
N_BRANCH = 4
BRANCH_W = D_MODEL // N_BRANCH
NSA_HEADS = 8
NSA_KV_GROUPS = 2
NSA_HEAD_DIM = BRANCH_W // NSA_HEADS
CMP_LEN = 32
CMP_STRIDE = 16
SLC_BLOCK = 64
SLC_TOP_N = 16
WIN = 512
Q_BLOCK = 128
S5_GROUP = 16
S5_GROUPS = BRANCH_W // S5_GROUP
S5_STATE = 64
S5_COLS = S5_GROUPS * S5_STATE
POOL_WINDOWS = (2, 4, 8, 16)
POOL_GROUP = BRANCH_W // len(POOL_WINDOWS)
POOL_HALO = 16
RET_HEADS = 4
RET_HEAD_DIM = BRANCH_W // RET_HEADS
RET_CHUNK = 128
D_FF = 4 * D_MODEL
PLE_DIM = 256
NORM_EPS = 1e-6
NEG_INF = -1e30
FORCE_BONUS = 1e4

NSA_KV_COLS = 6 * NSA_KV_GROUPS * NSA_HEAD_DIM
NSA_GATE_COLS = 3 * NSA_HEADS
OFF_Q = 0
OFF_KV = OFF_Q + BRANCH_W
OFF_G = OFF_KV + NSA_KV_COLS
OFF_S5 = OFF_G + NSA_GATE_COLS
OFF_POOL = OFF_S5 + BRANCH_W
OFF_RQ = OFF_POOL + BRANCH_W
OFF_RK = OFF_RQ + BRANCH_W
OFF_RV = OFF_RK + BRANCH_W
OFF_RG = OFF_RV + BRANCH_W
OFF_MERGE = OFF_RG + BRANCH_W

LANE = 128
SUBLANE = 8
GATE_PAD = LANE
ZA_Q, ZA_S5, ZA_POOL, ZA_RQ, ZA_RK, ZA_RV, ZA_RG = range(7)
ZA_KV_OFF = 7 * BRANCH_W
ZA_G_OFF = ZA_KV_OFF + NSA_KV_COLS
ZA_COLS = ZA_G_OFF + NSA_KV_GROUPS * GATE_PAD

VMEM_LIMIT = 56 * 1024 * 1024


def _params(*sem):
    return pltpu.CompilerParams(dimension_semantics=sem, vmem_limit_bytes=VMEM_LIMIT)


def _rms(x, g):
    return x * lax.rsqrt(jnp.mean(x * x, axis=-1, keepdims=True) + NORM_EPS) * g


def _sigmoid(x):
    return 1.0 / (1.0 + jnp.exp(-x))


def _inproj_kernel(h_ref, g_ref, w_ref, z_ref, u_ref):
    @pl.when(pl.program_id(1) == 0)
    def _():
        u_ref[...] = _rms(h_ref[...], g_ref[...]).astype(MX)

    z_ref[...] = jnp.dot(u_ref[...], w_ref[...], preferred_element_type=F32)


def _inproj(h, g, w, *, tm, tn):
    S, D = h.shape
    N = w.shape[1]
    return pl.pallas_call(
        _inproj_kernel,
        grid=(S // tm, N // tn),
        in_specs=[pl.BlockSpec((tm, D), lambda i, j: (i, 0)),
                  pl.BlockSpec((1, D), lambda i, j: (0, 0)),
                  pl.BlockSpec((D, tn), lambda i, j: (0, j))],
        out_specs=[pl.BlockSpec((tm, tn), lambda i, j: (i, j)),
                   pl.BlockSpec((tm, D), lambda i, j: (i, 0))],
        out_shape=[jax.ShapeDtypeStruct((S, N), F32), jax.ShapeDtypeStruct((S, D), MX)],
        compiler_params=_params("parallel", "arbitrary"),
        name="inproj",
    )(h, g, w)


def _merge_kernel(u_ref, o0_ref, o1_ref, o2_ref, o3_ref, wm0_ref, wm1_ref, wm2_ref, wm3_ref,
                  wb_ref, out_ref):
    u = u_ref[...]
    acc = None
    for j, (o_ref, wm_ref) in enumerate(((o0_ref, wm0_ref), (o1_ref, wm1_ref),
                                         (o2_ref, wm2_ref), (o3_ref, wm3_ref))):
        gate = _sigmoid(jnp.dot(u, wm_ref[...], preferred_element_type=F32))
        br = jnp.dot(o_ref[...], wb_ref[j], preferred_element_type=F32)
        acc = gate * br if acc is None else acc + gate * br
    out_ref[...] = acc.astype(MX)


def _merge(u, outs, wm, wb, *, tm, tn):
    S, D = u.shape
    nt = D // tn
    o_specs = [pl.BlockSpec((tm, BRANCH_W), lambda i, c: (i, 0)) for _ in range(N_BRANCH)]
    wm_specs = [pl.BlockSpec((D, tn), functools.partial(lambda i, c, j: (0, j * nt + c), j=j))
                for j in range(N_BRANCH)]
    return pl.pallas_call(
        _merge_kernel,
        grid=(S // tm, nt),
        in_specs=[pl.BlockSpec((tm, D), lambda i, c: (i, 0))] + o_specs + wm_specs
                 + [pl.BlockSpec((N_BRANCH, BRANCH_W, tn), lambda i, c: (0, 0, c))],
        out_specs=pl.BlockSpec((tm, tn), lambda i, c: (i, c)),
        out_shape=jax.ShapeDtypeStruct((S, D), MX),
        compiler_params=_params("parallel", "arbitrary"),
        name="merge",
    )(u, *outs, wm, wm, wm, wm, wb)


def _resid_mm_kernel(h_ref, x_ref, w_ref, out_ref):
    out_ref[...] = h_ref[...] + jnp.dot(x_ref[...], w_ref[...], preferred_element_type=F32)


def _resid_mm(h, x, w, *, tm, tn):
    S, D = h.shape
    K = x.shape[1]
    return pl.pallas_call(
        _resid_mm_kernel,
        grid=(S // tm, D // tn),
        in_specs=[pl.BlockSpec((tm, tn), lambda i, c: (i, c)),
                  pl.BlockSpec((tm, K), lambda i, c: (i, 0)),
                  pl.BlockSpec((K, tn), lambda i, c: (0, c))],
        out_specs=pl.BlockSpec((tm, tn), lambda i, c: (i, c)),
        out_shape=jax.ShapeDtypeStruct((S, D), F32),
        compiler_params=_params("parallel", "arbitrary"),
        name="resid_mm",
    )(h, x, w)


def _mlp_kernel(h_ref, g_ref, wu_ref, wd_ref, out_ref, v_ref):
    @pl.when(pl.program_id(1) == 0)
    def _():
        h = h_ref[...]
        v_ref[...] = _rms(h, g_ref[...]).astype(MX)
        out_ref[...] = h

    hid = jnp.maximum(jnp.dot(v_ref[...], wu_ref[...], preferred_element_type=F32), 0.0)
    out_ref[...] += jnp.dot((hid * hid).astype(MX), wd_ref[...], preferred_element_type=F32)


def _mlp(h, g, wu, wd, *, tm, tf):
    S, D = h.shape
    return pl.pallas_call(
        _mlp_kernel,
        grid=(S // tm, wu.shape[1] // tf),
        in_specs=[pl.BlockSpec((tm, D), lambda i, f: (i, 0)),
                  pl.BlockSpec((1, D), lambda i, f: (0, 0)),
                  pl.BlockSpec((D, tf), lambda i, f: (0, f)),
                  pl.BlockSpec((tf, D), lambda i, f: (f, 0))],
        out_specs=pl.BlockSpec((tm, D), lambda i, f: (i, 0)),
        out_shape=jax.ShapeDtypeStruct((S, D), F32),
        scratch_shapes=[pltpu.VMEM((tm, D), MX)],
        compiler_params=_params("parallel", "arbitrary"),
        name="mlp",
    )(h, g, wu, wd)


def _ple_kernel(h_ref, p_ref, wg_ref, wp_ref, gf_ref, out_ref, *, final_norm):
    h = h_ref[...]
    gate = _sigmoid(jnp.dot(h.astype(MX), wg_ref[...], preferred_element_type=F32))
    emb = jnp.dot(p_ref[...].astype(MX), wp_ref[...], preferred_element_type=F32)
    y = h + gate * emb
    out_ref[...] = _rms(y, gf_ref[...]) if final_norm else y


def _ple(h, p, wg, wp, gf, *, tm, final_norm):
    S, D = h.shape
    P = p.shape[1]
    return pl.pallas_call(
        functools.partial(_ple_kernel, final_norm=final_norm),
        grid=(S // tm,),
        in_specs=[pl.BlockSpec((tm, D), lambda i: (i, 0)),
                  pl.BlockSpec((tm, P), lambda i: (i, 0)),
                  pl.BlockSpec((D, D), lambda i: (0, 0)),
                  pl.BlockSpec((P, D), lambda i: (0, 0)),
                  pl.BlockSpec((1, D), lambda i: (0, 0))],
        out_specs=pl.BlockSpec((tm, D), lambda i: (i, 0)),
        out_shape=jax.ShapeDtypeStruct((S, D), F32),
        compiler_params=_params("parallel"),
        name="ple",
    )(h, p, wg, wp, gf)


S5_CW = 256
T_A1, T_A2, T_A4, T_ROW, T_A8 = range(5)


def _cmul_add(xr, xi, ar, ai, br, bi):
    return xr + (ar * br - ai * bi), xi + (ar * bi + ai * br)


def _s5_kernel(u_ref, bre_ref, bim_ref, tab_ref, cre_ref, cim_ref, d_ref, wglu_ref, out_ref,
               hre_ref, him_ref, carry_ref):
    tm = u_ref.shape[0]

    @pl.when(pl.program_id(0) == 0)
    def _():
        carry_ref[...] = jnp.zeros_like(carry_ref)

    u = u_ref[...]
    ub = u.astype(MX)
    hre_ref[...] = jnp.dot(ub, bre_ref[...], preferred_element_type=F32)
    him_ref[...] = jnp.dot(ub, bim_ref[...], preferred_element_type=F32)

    for cb in range(S5_COLS // S5_CW):
        cols = slice(cb * S5_CW, (cb + 1) * S5_CW)
        tabs = [(tab_ref[k, 0, :, cols], tab_ref[k, 1, :, cols]) for k in range(5)]

        def body(r, carry, cols=cols, tabs=tabs):
            cr, ci = carry
            rows = pl.ds(pl.multiple_of(r * SUBLANE, SUBLANE), SUBLANE)
            xr = hre_ref[rows, cols]
            xi = him_ref[rows, cols]
            for k, sh in ((T_A1, 1), (T_A2, 2), (T_A4, 4)):
                ar, ai = tabs[k]
                xr, xi = _cmul_add(xr, xi, ar, ai, pltpu.roll(xr, sh, 0), pltpu.roll(xi, sh, 0))
            ar, ai = tabs[T_ROW]
            hr, hi = _cmul_add(xr, xi, ar, ai, cr, ci)
            hre_ref[rows, cols] = hr
            him_ref[rows, cols] = hi
            ar, ai = tabs[T_A8]
            lr = jnp.broadcast_to(xr[SUBLANE - 1:SUBLANE, :], xr.shape)
            li = jnp.broadcast_to(xi[SUBLANE - 1:SUBLANE, :], xi.shape)
            return _cmul_add(lr, li, ar, ai, cr, ci)

        cr, ci = lax.fori_loop(0, tm // SUBLANE, body,
                               (carry_ref[0, :, cols], carry_ref[1, :, cols]), unroll=4)
        carry_ref[0, :, cols] = cr
        carry_ref[1, :, cols] = ci

    y = (jnp.dot(hre_ref[...].astype(MX), cre_ref[...], preferred_element_type=F32)
         + jnp.dot(him_ref[...].astype(MX), cim_ref[...], preferred_element_type=F32))
    y = y + d_ref[...] * u
    y = 0.5 * y * (1.0 + jnp.tanh(math.sqrt(2.0 / math.pi) * (y + 0.044715 * (y * y * y))))
    gl = jnp.dot(y.astype(MX), wglu_ref[...], preferred_element_type=F32)
    out_ref[...] = (gl[:, :BRANCH_W] * _sigmoid(gl[:, BRANCH_W:])).astype(MX)


def _s5_tables(a_re, a_im, log_dt, b_re, b_im, c_re, c_im):
    dt = jnp.exp(log_dt)[:, None]
    xr, xi = a_re * dt, a_im * dt

    pr = jnp.exp(xr) * jnp.cos(xi)
    pi = jnp.exp(xr) * jnp.sin(xi)
    den = a_re * a_re + a_im * a_im
    qr = ((pr - 1.0) * a_re + pi * a_im) / den
    qi = (pi * a_re - (pr - 1.0) * a_im) / den
    bbar_re = qr[..., None] * b_re - qi[..., None] * b_im
    bbar_im = qr[..., None] * b_im + qi[..., None] * b_re

    flat = lambda z: z.reshape(1, S5_COLS)
    powers = [(flat(pr), flat(pi))]
    for _ in range(SUBLANE - 1):
        zr, zi = powers[-1]
        powers.append((zr * flat(pr) - zi * flat(pi), zr * flat(pi) + zi * flat(pr)))
    row = jnp.arange(SUBLANE)[:, None]
    planes = []
    for sh in (1, 2, 4):
        zr, zi = powers[sh - 1]
        planes.append((jnp.where(row >= sh, zr, 0.0), jnp.where(row >= sh, zi, 0.0)))
    planes.append((jnp.concatenate([p[0] for p in powers]), jnp.concatenate([p[1] for p in powers])))
    zr, zi = powers[SUBLANE - 1]
    planes.append((jnp.broadcast_to(zr, (SUBLANE, S5_COLS)), jnp.broadcast_to(zi, (SUBLANE, S5_COLS))))
    tab = jnp.stack([jnp.stack(p) for p in planes]).astype(F32)

    eye = jnp.eye(S5_GROUPS, dtype=F32)

    def bdiag_in(m):
        return jnp.einsum('gnc,gh->gchn', m, eye).reshape(BRANCH_W, S5_COLS)

    def bdiag_out(m):
        return jnp.einsum('gcn,gh->gnhc', m, eye).reshape(S5_COLS, BRANCH_W)

    return (tab, bdiag_in(bbar_re).astype(MX), bdiag_in(bbar_im).astype(MX),
            bdiag_out(c_re).astype(MX), bdiag_out(-c_im).astype(MX))


def _s5_mixer(za, tabs, d_skip, w_glu, *, tm):
    S = za.shape[0]
    tab, bre, bim, cre, cim = tabs
    full = lambda shape: pl.BlockSpec(shape, lambda i: (0,) * len(shape))
    return pl.pallas_call(
        _s5_kernel,
        grid=(S // tm,),
        in_specs=[pl.BlockSpec((tm, BRANCH_W), lambda i: (i, ZA_S5)),
                  full(bre.shape), full(bim.shape), full(tab.shape), full(cre.shape),
                  full(cim.shape), full((1, BRANCH_W)), full(w_glu.shape)],
        out_specs=pl.BlockSpec((tm, BRANCH_W), lambda i: (i, 0)),
        out_shape=jax.ShapeDtypeStruct((S, BRANCH_W), MX),
        scratch_shapes=[pltpu.VMEM((tm, S5_COLS), F32), pltpu.VMEM((tm, S5_COLS), F32),
                        pltpu.VMEM((2, SUBLANE, S5_COLS), F32)],
        compiler_params=_params("arbitrary"),
        name="s5",
    )(za, bre, bim, tab, cre, cim, d_skip.reshape(1, BRANCH_W), w_glu)


def _pool_kernel(x_ref, w_ref, scale_ref, out_ref, ext_ref):
    tm = x_ref.shape[0]
    i = pl.program_id(0)

    @pl.when(i == 0)
    def _():
        ext_ref[0:POOL_HALO, :] = jnp.zeros((POOL_HALO, BRANCH_W), F32)

    x = x_ref[...]
    ext_ref[POOL_HALO:, :] = x
    t = i * tm + lax.broadcasted_iota(jnp.int32, (tm, 1), 0)
    for gi, w in enumerate(POOL_WINDOWS):
        cols = slice(gi * POOL_GROUP, (gi + 1) * POOL_GROUP)
        acc = x[:, cols]
        for d in range(1, w):
            acc = acc + ext_ref[POOL_HALO - d:POOL_HALO - d + tm, cols]
        count = jnp.minimum(t + 1, w).astype(F32)
        diff = (acc / count - x[:, cols]).astype(MX)
        y = jnp.dot(diff, w_ref[gi], preferred_element_type=F32)
        out_ref[:, cols] = (y * scale_ref[:, cols]).astype(MX)
    ext_ref[0:POOL_HALO, :] = x[tm - POOL_HALO:, :]


def _pool_mixer(za, w_pool, pool_scale, *, tm):
    S = za.shape[0]
    return pl.pallas_call(
        _pool_kernel,
        grid=(S // tm,),
        in_specs=[pl.BlockSpec((tm, BRANCH_W), lambda i: (i, ZA_POOL)),
                  pl.BlockSpec(w_pool.shape, lambda i: (0, 0, 0)),
                  pl.BlockSpec((1, BRANCH_W), lambda i: (0, 0))],
        out_specs=pl.BlockSpec((tm, BRANCH_W), lambda i: (i, 0)),
        out_shape=jax.ShapeDtypeStruct((S, BRANCH_W), MX),
        scratch_shapes=[pltpu.VMEM((tm + POOL_HALO, BRANCH_W), F32)],
        compiler_params=_params("arbitrary"),
        name="pool",
    )(za, w_pool, pool_scale.reshape(1, BRANCH_W))


def _ret_consts():
    H, dh, C = RET_HEADS, RET_HEAD_DIM, RET_CHUNK
    angle = jnp.repeat(1.0 / (10000.0 ** jnp.linspace(0.0, 1.0, dh // 2)), 2).reshape(1, dh)
    log_g = jnp.log(1.0 - 2.0 ** (-5.0 - jnp.arange(H, dtype=F32)))
    idx = jnp.arange(C, dtype=F32)
    rel = idx[:, None] - idx[None, :]
    decay = jnp.where(rel >= 0, jnp.exp(jnp.maximum(rel, 0.0)[None] * log_g[:, None, None]), 0.0)
    xi = jnp.exp((idx + 1.0)[None, :] * log_g[:, None])[..., None]
    zeta = jnp.exp((C - 1.0 - idx)[None, :] * log_g[:, None])[..., None]
    g_chunk = jnp.broadcast_to(jnp.exp(C * log_g)[:, None, None], (H, 1, dh))
    return angle.astype(F32), decay, xi, zeta, g_chunk


def _rotate_pairs(x, even):
    n = x.shape[-1]
    return jnp.where(even, -pltpu.roll(x, n - 1, 1), pltpu.roll(x, 1, 1))


def _ret_kernel(q_ref, k_ref, v_ref, g_ref, ang_ref, decay_ref, xi_ref, zeta_ref, gch_ref,
                out_ref, state_ref):
    C, dh = RET_CHUNK, RET_HEAD_DIM
    i = pl.program_id(0)

    @pl.when(i == 0)
    def _():
        state_ref[...] = jnp.zeros_like(state_ref)

    pos = (i * C + lax.broadcasted_iota(jnp.int32, (C, 1), 0)).astype(F32)
    ang = pos * ang_ref[...]
    sin, cos = jnp.sin(ang), jnp.cos(ang)
    even = (lax.broadcasted_iota(jnp.int32, (C, dh), 1) % 2) == 0
    for hd in range(RET_HEADS):
        cols = slice(hd * dh, (hd + 1) * dh)
        q = q_ref[:, cols]
        k = k_ref[:, cols]
        vb = v_ref[:, cols].astype(MX)
        q = q * cos + _rotate_pairs(q, even) * sin
        k = (k * cos + _rotate_pairs(k, even) * sin) * dh ** -0.5
        qb = q.astype(MX)
        s = lax.dot_general(qb, k.astype(MX), (((1,), (1,)), ((), ())),
                            preferred_element_type=F32) * decay_ref[hd]
        inner = jnp.dot(s.astype(MX), vb, preferred_element_type=F32)
        state = state_ref[hd]
        cross = jnp.dot(qb, state.astype(MX), preferred_element_type=F32) * xi_ref[hd]
        kz = (k * zeta_ref[hd]).astype(MX)
        state_ref[hd] = state * gch_ref[hd] + lax.dot_general(
            kz, vb, (((0,), (0,)), ((), ())), preferred_element_type=F32)
        y = inner + cross
        y = y * lax.rsqrt(jnp.mean(y * y, axis=-1, keepdims=True) + NORM_EPS)
        g = g_ref[:, cols]
        out_ref[:, cols] = (g * _sigmoid(g) * y).astype(MX)


def _ret_mixer(za):
    S = za.shape[0]
    C = RET_CHUNK
    consts = _ret_consts()
    col = lambda c: pl.BlockSpec((C, BRANCH_W), lambda i: (i, c))
    full = lambda a: pl.BlockSpec(a.shape, lambda i: (0,) * a.ndim)
    return pl.pallas_call(
        _ret_kernel,
        grid=(S // C,),
        in_specs=[col(ZA_RQ), col(ZA_RK), col(ZA_RV), col(ZA_RG)] + [full(a) for a in consts],
        out_specs=pl.BlockSpec((C, BRANCH_W), lambda i: (i, 0)),
        out_shape=jax.ShapeDtypeStruct((S, BRANCH_W), MX),
        scratch_shapes=[pltpu.VMEM((RET_HEADS, RET_HEAD_DIM, RET_HEAD_DIM), F32)],
        compiler_params=_params("arbitrary"),
        name="retention",
    )(za, za, za, za, *consts)


NSA_R = NSA_HEADS // NSA_KV_GROUPS
NSA_TQ = 128
NSA_TK = 512
CMP_CH = 128
P_OFF = SUBLANE
NT_DIMS = (((1,), (1,)), ((), ()))


def _compress_kernel(x_ref, w_ref, out_ref):
    out_ref[0] = jnp.dot(x_ref[0], w_ref[...], preferred_element_type=F32).astype(out_ref.dtype)


def _compress_t_kernel(x_ref, wt_ref, out_ref):
    out_ref[0] = lax.dot_general(wt_ref[...], x_ref[0], NT_DIMS,
                                 preferred_element_type=F32).astype(out_ref.dtype)


def _compress(x, w, *, transposed):
    G, nc, kk = x.shape
    if transposed:
        kern, out_block = _compress_t_kernel, (1, w.shape[0], nc)
    else:
        kern, out_block = _compress_kernel, (1, nc, w.shape[1])
    return pl.pallas_call(
        kern,
        grid=(G,),
        in_specs=[pl.BlockSpec((1, nc, kk), lambda g: (g, 0, 0)),
                  pl.BlockSpec(w.shape, lambda g: (0, 0))],
        out_specs=pl.BlockSpec(out_block, lambda g: (g, 0, 0)),
        out_shape=jax.ShapeDtypeStruct((G,) + out_block[1:], x.dtype),
        compiler_params=_params("arbitrary"),
        name="nsa_compress",
    )(x, w)


def _group_queries(q_ref, g):
    hd = NSA_HEAD_DIM
    base = g * NSA_R * hd
    qs = [q_ref[:, base + r * hd:base + (r + 1) * hd] for r in range(NSA_R)]
    return (jnp.concatenate(qs, axis=0) * hd ** -0.5).astype(MX)


def _cmp_select_kernel(q_ref, kc_ref, vct_ref, ocmp_ref, sel_ref, s_sc, p_sc, ot_sc):
    i = pl.program_id(0)
    tq, hd = NSA_TQ, NSA_HEAD_DIM
    nc = kc_ref.shape[1]
    ns = sel_ref.shape[2]
    ch = min(CMP_CH, nc)
    nch = jnp.minimum((SUBLANE * (i + 1) + ch - 1) // ch, nc // ch)
    t4 = i * tq + lax.broadcasted_iota(jnp.int32, (1, NSA_R * tq), 1) % tq
    t1 = i * tq + lax.broadcasted_iota(jnp.int32, (1, tq), 1)
    nrow = lax.broadcasted_iota(jnp.int32, (ch, 1), 0)
    jrow = lax.broadcasted_iota(jnp.int32, (ns, 1), 0)
    jf = jrow.astype(F32)

    def rows(c):
        return pl.ds(pl.multiple_of(c * ch, ch), ch)

    def visible(c):
        return ((c * ch + nrow) * CMP_STRIDE + (CMP_LEN - 1)) <= t4

    for g in range(NSA_KV_GROUPS):
        qg = _group_queries(q_ref, g)

        def scores(c, m, g=g, qg=qg):
            s = lax.dot_general(kc_ref[g, rows(c), :], qg, NT_DIMS, preferred_element_type=F32)
            s = jnp.where(visible(c), s, NEG_INF)
            s_sc[rows(c), :] = s
            return jnp.maximum(m, jnp.max(s, axis=0, keepdims=True))

        m = lax.fori_loop(0, nch, scores, jnp.full((1, NSA_R * tq), NEG_INF, F32))

        def expsum(c, l, m=m):
            e = jnp.where(visible(c), jnp.exp(s_sc[rows(c), :] - m), 0.0)
            s_sc[rows(c), :] = e
            return l + jnp.sum(e, axis=0, keepdims=True)

        l = lax.fori_loop(0, nch, expsum, jnp.zeros((1, NSA_R * tq), F32))
        inv = 1.0 / jnp.maximum(l, 1e-30)
        p_sc[...] = jnp.zeros_like(p_sc)
        ot_sc[...] = jnp.zeros_like(ot_sc)

        def weigh(c, carry, g=g, inv=inv):
            p = s_sc[rows(c), :] * inv
            p_sc[pl.ds(pl.multiple_of(P_OFF + c * ch, SUBLANE), ch), :] = (
                p[:, 0:tq] + p[:, tq:2 * tq] + p[:, 2 * tq:3 * tq] + p[:, 3 * tq:4 * tq])
            ot_sc[...] += jnp.dot(vct_ref[g, :, rows(c)], p.astype(MX), preferred_element_type=F32)
            return carry

        lax.fori_loop(0, nch, weigh, 0)
        for r in range(NSA_R):
            o = ot_sc[:, r * tq:(r + 1) * tq].T
            h = g * NSA_R + r
            ocmp_ref[:, h * hd:(h + 1) * hd] = o[:, :hd]

        ratio = SLC_BLOCK // CMP_STRIDE
        imp = p_sc[pl.ds(P_OFF - 1, ns, stride=ratio), :]
        for k in range(ratio):
            imp = imp + p_sc[pl.ds(P_OFF + k, ns, stride=ratio), :]
        valid = jrow * SLC_BLOCK <= t1
        forced = (jrow == 0) | (jrow == t1 // SLC_BLOCK)
        x = jnp.where(valid, imp + jnp.where(forced, FORCE_BONUS, 0.0), NEG_INF)
        sel = jnp.zeros((ns, tq), F32)
        for _ in range(min(SLC_TOP_N, ns)):
            mx = jnp.max(x, axis=0, keepdims=True)
            idx = jnp.min(jnp.where(x == mx, jf, float(ns)), axis=0, keepdims=True)
            hit = jf == idx
            sel = jnp.where(hit, 1.0, sel)
            x = jnp.where(hit, -jnp.inf, x)
        sel_ref[g] = sel.T.astype(BF16)


def _cmp_select(za, kc, vct):
    S = za.shape[0]
    G, nc, hd = kc.shape
    ns = S // SLC_BLOCK
    tq = NSA_TQ
    return pl.pallas_call(
        _cmp_select_kernel,
        grid=(S // tq,),
        in_specs=[pl.BlockSpec((tq, BRANCH_W), lambda i: (i, ZA_Q)),
                  pl.BlockSpec(kc.shape, lambda i: (0, 0, 0)),
                  pl.BlockSpec(vct.shape, lambda i: (0, 0, 0))],
        out_specs=[pl.BlockSpec((tq, BRANCH_W), lambda i: (i, 0)),
                   pl.BlockSpec((G, tq, ns), lambda i: (0, i, 0))],
        out_shape=[jax.ShapeDtypeStruct((S, BRANCH_W), F32),
                   jax.ShapeDtypeStruct((G, S, ns), BF16)],
        scratch_shapes=[pltpu.VMEM((nc, NSA_R * tq), F32), pltpu.VMEM((P_OFF + nc, tq), F32),
                        pltpu.VMEM((LANE, NSA_R * tq), F32)],
        compiler_params=_params("parallel"),
        name="nsa_cmp_select",
    )(za, kc, vct)


def _slc_kernel(q_ref, ks_ref, vs_ref, sel_ref, e_ref, out_ref, m_sc, acc_sc):
    i = pl.program_id(1)
    tq, hd = NSA_TQ, NSA_HEAD_DIM
    tk = e_ref.shape[2]
    ksel = e_ref.shape[1]
    bpt = tk // SLC_BLOCK
    qg = _group_queries(q_ref, 0)
    t = i * tq + lax.broadcasted_iota(jnp.int32, (tq, 1), 0)
    lane = lax.broadcasted_iota(jnp.int32, (1, tk), 1)
    m_sc[...] = jnp.full_like(m_sc, NEG_INF)
    acc_sc[...] = jnp.zeros_like(acc_sc)

    def step(kt, carry):
        k0 = pl.multiple_of(kt * tk, tk)
        s = lax.dot_general(qg, ks_ref[0, pl.ds(k0, tk), :], NT_DIMS, preferred_element_type=F32)
        blk0 = kt * bpt
        lane0 = pl.multiple_of((blk0 // ksel) * ksel, ksel)
        picked = jnp.dot(sel_ref[0, :, pl.ds(lane0, ksel)], e_ref[(blk0 % ksel) // bpt],
                         preferred_element_type=F32)
        bias = jnp.where((picked > 0.5) & (k0 + lane <= t), 0.0, NEG_INF)
        s = s + jnp.concatenate([bias] * NSA_R, axis=0)
        m_prev = m_sc[...]
        m_new = jnp.maximum(m_prev, jnp.max(s, axis=-1, keepdims=True))
        p = jnp.exp(s - jnp.concatenate([m_new] * (tk // LANE), axis=1))
        acc_sc[...] = jnp.exp(m_prev - m_new) * acc_sc[...] + jnp.dot(
            p.astype(MX), vs_ref[0, pl.ds(k0, tk), :], preferred_element_type=F32)
        m_sc[...] = m_new
        return carry

    lax.fori_loop(0, (i * tq + tq - 1) // tk + 1, step, 0)
    acc = acc_sc[...]
    o = acc[:, :hd] / acc[:, hd:]
    for r in range(NSA_R):
        out_ref[:, r * hd:(r + 1) * hd] = o[r * tq:(r + 1) * tq, :]


def _slc_attention(za, ks, vs, sel):
    S = za.shape[0]
    G = ks.shape[0]
    ns = sel.shape[2]
    tq, hd = NSA_TQ, NSA_HEAD_DIM
    tk = min(NSA_TK, S)
    bpt = tk // SLC_BLOCK
    ksel = min(LANE, ns)
    j = jnp.arange(ksel)[None, :, None]
    c = jnp.arange(tk)[None, None, :]
    k = jnp.arange(ksel // bpt)[:, None, None]
    expand = (j == k * bpt + c // SLC_BLOCK).astype(BF16)
    gw = NSA_R * hd
    return pl.pallas_call(
        _slc_kernel,
        grid=(G, S // tq),
        in_specs=[pl.BlockSpec((tq, gw), lambda g, i: (i, g)),
                  pl.BlockSpec((1, S, hd), lambda g, i: (g, 0, 0)),
                  pl.BlockSpec((1, S, LANE), lambda g, i: (g, 0, 0)),
                  pl.BlockSpec((1, tq, ns), lambda g, i: (g, i, 0)),
                  pl.BlockSpec(expand.shape, lambda g, i: (0, 0, 0))],
        out_specs=pl.BlockSpec((tq, gw), lambda g, i: (i, g)),
        out_shape=jax.ShapeDtypeStruct((S, BRANCH_W), F32),
        scratch_shapes=[pltpu.VMEM((NSA_R * tq, LANE), F32), pltpu.VMEM((NSA_R * tq, LANE), F32)],
        compiler_params=_params("arbitrary", "arbitrary"),
        name="nsa_selected",
    )(za, ks, vs, sel, expand)


def _win_kernel(q_ref, kw_ref, vw_ref, gate_ref, ocmp_ref, oslc_ref, out_ref):
    i = pl.program_id(1)
    tq, hd = NSA_TQ, NSA_HEAD_DIM
    span = tq + WIN
    qg = _group_queries(q_ref, 0)
    r0 = pl.multiple_of(i * tq, tq)
    t = i * tq + lax.broadcasted_iota(jnp.int32, (tq, 1), 0)
    pos = i * tq - WIN + lax.broadcasted_iota(jnp.int32, (1, span), 1)
    ok = (pos <= t) & (pos > t - WIN) & (pos >= 0)
    s = lax.dot_general(qg, kw_ref[0, pl.ds(r0, span), :], NT_DIMS, preferred_element_type=F32)
    s = s + jnp.concatenate([jnp.where(ok, 0.0, NEG_INF)] * NSA_R, axis=0)
    e = jnp.exp(s - jnp.max(s, axis=-1, keepdims=True))
    acc = jnp.dot(e.astype(MX), vw_ref[0, pl.ds(r0, span), :], preferred_element_type=F32)
    ow = acc[:, :hd] / acc[:, hd:]
    sig = _sigmoid(gate_ref[...])
    for r in range(NSA_R):
        cols = slice(r * hd, (r + 1) * hd)
        out_ref[:, cols] = (sig[:, r:r + 1] * ocmp_ref[:, cols]
                            + sig[:, NSA_R + r:NSA_R + r + 1] * oslc_ref[:, cols]
                            + sig[:, 2 * NSA_R + r:2 * NSA_R + r + 1] * ow[r * tq:(r + 1) * tq, :]
                            ).astype(MX)


def _win_combine(za, kw, vw, ocmp, oslc):
    S = za.shape[0]
    G = kw.shape[0]
    tq, hd = NSA_TQ, NSA_HEAD_DIM
    gw = NSA_R * hd
    blk = lambda: pl.BlockSpec((tq, gw), lambda g, i: (i, g))
    return pl.pallas_call(
        _win_kernel,
        grid=(G, S // tq),
        in_specs=[blk(),
                  pl.BlockSpec((1, S + WIN, hd), lambda g, i: (g, 0, 0)),
                  pl.BlockSpec((1, S + WIN, LANE), lambda g, i: (g, 0, 0)),
                  pl.BlockSpec((tq, GATE_PAD), lambda g, i: (i, ZA_G_OFF // GATE_PAD + g)),
                  blk(), blk()],
        out_specs=blk(),
        out_shape=jax.ShapeDtypeStruct((S, BRANCH_W), MX),
        compiler_params=_params("arbitrary", "arbitrary"),
        name="nsa_window_combine",
    )(za, kw, vw, za, ocmp, oslc)


def _nsa_mixer(za, w_cmp_k, w_cmp_v):
    S = za.shape[0]
    G, hd = NSA_KV_GROUPS, NSA_HEAD_DIM
    nc = S // CMP_STRIDE

    def kv(idx):
        off = ZA_KV_OFF + idx * G * hd
        return jnp.stack([za[:, off + g * hd:off + (g + 1) * hd] for g in range(G)])

    def blocks(x):
        half = x.reshape(G, nc, CMP_STRIDE * hd)
        nxt = jnp.concatenate([half[:, 1:], jnp.zeros_like(half[:, :1])], axis=1)
        return jnp.concatenate([half, nxt], axis=2).astype(MX)

    def with_ones(x):
        return jnp.concatenate([x, jnp.ones((G, S, LANE - hd), x.dtype)], axis=2).astype(MX)

    front = lambda x: jnp.pad(x, ((0, 0), (WIN, 0), (0, 0)))
    wk = w_cmp_k.reshape(CMP_LEN * hd, hd).astype(MX)
    wvt = jnp.pad(w_cmp_v.reshape(CMP_LEN * hd, hd).T, ((0, LANE - hd), (0, 0))).astype(MX)
    kc = _compress(blocks(kv(0)), wk, transposed=False)
    vct = _compress(blocks(kv(1)), wvt, transposed=True)
    ocmp, sel = _cmp_select(za, kc, vct)
    oslc = _slc_attention(za, kv(2).astype(MX), with_ones(kv(3)), sel)
    return _win_combine(za, front(kv(4).astype(MX)), front(with_ones(kv(5))), ocmp, oslc)


def _pack_w_in(w):
    seg = lambda off, n: w[:, off:off + n]
    R = NSA_HEADS // NSA_KV_GROUPS
    gates = []
    for g in range(NSA_KV_GROUPS):
        cols = [OFF_G + br * NSA_HEADS + g * R + r for br in range(3) for r in range(R)]
        gates.append(jnp.pad(w[:, jnp.array(cols)], ((0, 0), (0, GATE_PAD - len(cols)))))
    wa = jnp.concatenate([seg(OFF_Q, BRANCH_W), seg(OFF_S5, BRANCH_W), seg(OFF_POOL, BRANCH_W),
                          seg(OFF_RQ, BRANCH_W), seg(OFF_RK, BRANCH_W), seg(OFF_RV, BRANCH_W),
                          seg(OFF_RG, BRANCH_W), seg(OFF_KV, NSA_KV_COLS)] + gates, axis=1)
    return wa.astype(MX), w[:, OFF_MERGE:].astype(MX)


def kernel(x, p, g_mix, w_in, w_cmp_k, w_cmp_v, s5_a_re, s5_a_im, s5_log_dt, s5_b_re, s5_b_im,
           s5_c_re, s5_c_im, s5_d, s5_w_glu, pool_w, pool_scale, w_branch, w_out, g_mlp,
           w_mlp_up, w_mlp_down, w_ple_gate, w_ple_proj, g_final):
    B, S, D = x.shape
    assert B == 1 and D == D_MODEL
    depth = w_in.shape[0]
    h = x.reshape(S, D)
    for i in range(depth):
        wa, wm = _pack_w_in(w_in[i])
        za, u = _inproj(h, g_mix[i].reshape(1, D), wa, tm=1024, tn=768)

        o_nsa = _nsa_mixer(za, w_cmp_k[i], w_cmp_v[i])
        s5_tabs = _s5_tables(s5_a_re[i], s5_a_im[i], s5_log_dt[i], s5_b_re[i], s5_b_im[i],
                             s5_c_re[i], s5_c_im[i])
        o_s5 = _s5_mixer(za, s5_tabs, s5_d[i], s5_w_glu[i].astype(MX), tm=512)
        o_pool = _pool_mixer(za, pool_w[i].astype(MX), pool_scale[i], tm=1024)
        o_ret = _ret_mixer(za)

        merged = _merge(u, (o_nsa, o_s5, o_pool, o_ret), wm, w_branch[i].astype(MX),
                        tm=1024, tn=256)
        h = _resid_mm(h, merged, w_out[i].astype(MX), tm=1024, tn=1024)
        h = _mlp(h, g_mlp[i].reshape(1, D), w_mlp_up[i].astype(MX),
                 w_mlp_down[i].astype(MX), tm=512, tf=512)
        h = _ple(h, p[i].reshape(S, PLE_DIM), w_ple_gate[i].astype(MX),
                 w_ple_proj[i].astype(MX), g_final.reshape(1, D), tm=512,
                 final_norm=(i == depth - 1))
    return h.reshape(B, S, D)
```

```python
import functools
import math

import jax
import jax.numpy as jnp
from jax import lax
from jax.experimental import pallas as pl
from jax.experimental.pallas import tpu as pltpu

F32 = jnp.float32
BF16 = jnp.bfloat16
MX = BF16

D_MODEL = 2048
N_BRANCH = 4
BRANCH_W = D_MODEL // N_BRANCH
NSA_HEADS = 8
NSA_KV_GROUPS = 2
NSA_HEAD_DIM = BRANCH_W // NSA_HEADS
CMP_LEN = 32
CMP_STRIDE = 16
SLC_BLOCK = 64
SLC_TOP_N = 16
WIN = 512
Q_BLOCK = 128
S5_GROUP = 16
S5_GROUPS = BRANCH_W // S5_GROUP
S5_STATE = 64
S5_COLS = S5_GROUPS * S5_STATE
POOL_WINDOWS = (2, 4, 8, 16)
POOL_GROUP = BRANCH_W // len(POOL_WINDOWS)
POOL_HALO = 16
RET_HEADS = 4
RET_HEAD_DIM = BRANCH_W // RET_HEADS
RET_CHUNK = 128
D_FF = 4 * D_MODEL
PLE_DIM = 256
NORM_EPS = 1e-6
NEG_INF = -1e30
FORCE_BONUS = 1e4

NSA_KV_COLS = 6 * NSA_KV_GROUPS * NSA_HEAD_DIM
NSA_GATE_COLS = 3 * NSA_HEADS
OFF_Q = 0
OFF_KV = OFF_Q + BRANCH_W
OFF_G = OFF_KV + NSA_KV_COLS
OFF_S5 = OFF_G + NSA_GATE_COLS
OFF_POOL = OFF_S5 + BRANCH_W
OFF_RQ = OFF_POOL + BRANCH_W
OFF_RK = OFF_RQ + BRANCH_W
OFF_RV = OFF_RK + BRANCH_W
OFF_RG = OFF_RV + BRANCH_W
OFF_MERGE = OFF_RG + BRANCH_W

LANE = 128
SUBLANE = 8
GATE_PAD = LANE
ZA_Q, ZA_S5, ZA_POOL, ZA_RQ, ZA_RK, ZA_RV, ZA_RG = range(7)
ZA_KV_OFF = 7 * BRANCH_W
ZA_G_OFF = ZA_KV_OFF + NSA_KV_COLS
ZA_COLS = ZA_G_OFF + NSA_KV_GROUPS * GATE_PAD

VMEM_LIMIT = 56 * 1024 * 1024


def _params(*sem):
    return pltpu.CompilerParams(dimension_semantics=sem, vmem_limit_bytes=VMEM_LIMIT)


def _rms(x, g):
    return x * lax.rsqrt(jnp.mean(x * x, axis=-1, keepdims=True) + NORM_EPS) * g


def _sigmoid(x):
    return 1.0 / (1.0 + jnp.exp(-x))


def _inproj_kernel(h_ref, g_ref, w_ref, z_ref, u_ref):
    @pl.when(pl.program_id(1) == 0)
    def _():
        u_ref[...] = _rms(h_ref[...], g_ref[...]).astype(MX)

    z_ref[...] = jnp.dot(u_ref[...], w_ref[...], preferred_element_type=F32)


def _inproj(h, g, w, *, tm, tn):
    S, D = h.shape
    N = w.shape[1]
    return pl.pallas_call(
        _inproj_kernel,
        grid=(S // tm, N // tn),
        in_specs=[pl.BlockSpec((tm, D), lambda i, j: (i, 0)),
                  pl.BlockSpec((1, D), lambda i, j: (0, 0)),
                  pl.BlockSpec((D, tn), lambda i, j: (0, j))],
        out_specs=[pl.BlockSpec((tm, tn), lambda i, j: (i, j)),
                   pl.BlockSpec((tm, D), lambda i, j: (i, 0))],
        out_shape=[jax.ShapeDtypeStruct((S, N), F32), jax.ShapeDtypeStruct((S, D), MX)],
        compiler_params=_params("parallel", "arbitrary"),
        name="inproj",
    )(h, g, w)


def _merge_kernel(u_ref, o0_ref, o1_ref, o2_ref, o3_ref, wm0_ref, wm1_ref, wm2_ref, wm3_ref,
                  wb_ref, out_ref):
    u = u_ref[...]
    acc = None
    for j, (o_ref, wm_ref) in enumerate(((o0_ref, wm0_ref), (o1_ref, wm1_ref),
                                         (o2_ref, wm2_ref), (o3_ref, wm3_ref))):
        gate = _sigmoid(jnp.dot(u, wm_ref[...], preferred_element_type=F32))
        br = jnp.dot(o_ref[...], wb_ref[j], preferred_element_type=F32)
        acc = gate * br if acc is None else acc + gate * br
    out_ref[...] = acc.astype(MX)


def _merge(u, outs, wm, wb, *, tm, tn):
    S, D = u.shape
    nt = D // tn
    o_specs = [pl.BlockSpec((tm, BRANCH_W), lambda i, c: (i, 0)) for _ in range(N_BRANCH)]
    wm_specs = [pl.BlockSpec((D, tn), functools.partial(lambda i, c, j: (0, j * nt + c), j=j))
                for j in range(N_BRANCH)]
    return pl.pallas_call(
        _merge_kernel,
        grid=(S // tm, nt),
        in_specs=[pl.BlockSpec((tm, D), lambda i, c: (i, 0))] + o_specs + wm_specs
                 + [pl.BlockSpec((N_BRANCH, BRANCH_W, tn), lambda i, c: (0, 0, c))],
        out_specs=pl.BlockSpec((tm, tn), lambda i, c: (i, c)),
        out_shape=jax.ShapeDtypeStruct((S, D), MX),
        compiler_params=_params("parallel", "arbitrary"),
        name="merge",
    )(u, *outs, wm, wm, wm, wm, wb)


def _resid_mm_kernel(h_ref, x_ref, w_ref, out_ref):
    out_ref[...] = h_ref[...] + jnp.dot(x_ref[...], w_ref[...], preferred_element_type=F32)


def _resid_mm(h, x, w, *, tm, tn):
    S, D = h.shape
    K = x.shape[1]
    return pl.pallas_call(
        _resid_mm_kernel,
        grid=(S // tm, D // tn),
        in_specs=[pl.BlockSpec((tm, tn), lambda i, c: (i, c)),
                  pl.BlockSpec((tm, K), lambda i, c: (i, 0)),
                  pl.BlockSpec((K, tn), lambda i, c: (0, c))],
        out_specs=pl.BlockSpec((tm, tn), lambda i, c: (i, c)),
        out_shape=jax.ShapeDtypeStruct((S, D), F32),
        compiler_params=_params("parallel", "arbitrary"),
        name="resid_mm",
    )(h, x, w)


def _mlp_kernel(h_ref, g_ref, wu_ref, wd_ref, out_ref, v_ref):
    @pl.when(pl.program_id(1) == 0)
    def _():
        h = h_ref[...]
        v_ref[...] = _rms(h, g_ref[...]).astype(MX)
        out_ref[...] = h

    hid = jnp.maximum(jnp.dot(v_ref[...], wu_ref[...], preferred_element_type=F32), 0.0)
    out_ref[...] += jnp.dot((hid * hid).astype(MX), wd_ref[...], preferred_element_type=F32)


def _mlp(h, g, wu, wd, *, tm, tf):
    S, D = h.shape
    return pl.pallas_call(
        _mlp_kernel,
        grid=(S // tm, wu.shape[1] // tf),
        in_specs=[pl.BlockSpec((tm, D), lambda i, f: (i, 0)),
                  pl.BlockSpec((1, D), lambda i, f: (0, 0)),
                  pl.BlockSpec((D, tf), lambda i, f: (0, f)),
                  pl.BlockSpec((tf, D), lambda i, f: (f, 0))],
        out_specs=pl.BlockSpec((tm, D), lambda i, f: (i, 0)),
        out_shape=jax.ShapeDtypeStruct((S, D), F32),
        scratch_shapes=[pltpu.VMEM((tm, D), MX)],
        compiler_params=_params("parallel", "arbitrary"),
        name="mlp",
    )(h, g, wu, wd)


def _ple_kernel(h_ref, p_ref, wg_ref, wp_ref, gf_ref, out_ref, *, final_norm):
    h = h_ref[...]
    gate = _sigmoid(jnp.dot(h.astype(MX), wg_ref[...], preferred_element_type=F32))
    emb = jnp.dot(p_ref[...].astype(MX), wp_ref[...], preferred_element_type=F32)
    y = h + gate * emb
    out_ref[...] = _rms(y, gf_ref[...]) if final_norm else y


def _ple(h, p, wg, wp, gf, *, tm, final_norm):
    S, D = h.shape
    P = p.shape[1]
    return pl.pallas_call(
        functools.partial(_ple_kernel, final_norm=final_norm),
        grid=(S // tm,),
        in_specs=[pl.BlockSpec((tm, D), lambda i: (i, 0)),
                  pl.BlockSpec((tm, P), lambda i: (i, 0)),
                  pl.BlockSpec((D, D), lambda i: (0, 0)),
                  pl.BlockSpec((P, D), lambda i: (0, 0)),
                  pl.BlockSpec((1, D), lambda i: (0, 0))],
        out_specs=pl.BlockSpec((tm, D), lambda i: (i, 0)),
        out_shape=jax.ShapeDtypeStruct((S, D), F32),
        compiler_params=_params("parallel"),
        name="ple",
    )(h, p, wg, wp, gf)


S5_CW = 256
T_A1, T_A2, T_A4, T_ROW, T_A8 = range(5)


def _cmul_add(xr, xi, ar, ai, br, bi):
    return xr + (ar * br - ai * bi), xi + (ar * bi + ai * br)


def _s5_kernel(u_ref, bre_ref, bim_ref, tab_ref, cre_ref, cim_ref, d_ref, wglu_ref, out_ref,
               hre_ref, him_ref, carry_ref):
    tm = u_ref.shape[0]

    @pl.when(pl.program_id(0) == 0)
    def _():
        carry_ref[...] = jnp.zeros_like(carry_ref)

    u = u_ref[...]
    ub = u.astype(MX)
    hre_ref[...] = jnp.dot(ub, bre_ref[...], preferred_element_type=F32)
    him_ref[...] = jnp.dot(ub, bim_ref[...], preferred_element_type=F32)

    for cb in range(S5_COLS // S5_CW):
        cols = slice(cb * S5_CW, (cb + 1) * S5_CW)
        tabs = [(tab_ref[k, 0, :, cols], tab_ref[k, 1, :, cols]) for k in range(5)]

        def body(r, carry, cols=cols, tabs=tabs):
            cr, ci = carry
            rows = pl.ds(pl.multiple_of(r * SUBLANE, SUBLANE), SUBLANE)
            xr = hre_ref[rows, cols]
            xi = him_ref[rows, cols]
            for k, sh in ((T_A1, 1), (T_A2, 2), (T_A4, 4)):
                ar, ai = tabs[k]
                xr, xi = _cmul_add(xr, xi, ar, ai, pltpu.roll(xr, sh, 0), pltpu.roll(xi, sh, 0))
            ar, ai = tabs[T_ROW]
            hr, hi = _cmul_add(xr, xi, ar, ai, cr, ci)
            hre_ref[rows, cols] = hr
            him_ref[rows, cols] = hi
            ar, ai = tabs[T_A8]
            lr = jnp.broadcast_to(xr[SUBLANE - 1:SUBLANE, :], xr.shape)
            li = jnp.broadcast_to(xi[SUBLANE - 1:SUBLANE, :], xi.shape)
            return _cmul_add(lr, li, ar, ai, cr, ci)

        cr, ci = lax.fori_loop(0, tm // SUBLANE, body,
                               (carry_ref[0, :, cols], carry_ref[1, :, cols]), unroll=4)
        carry_ref[0, :, cols] = cr
        carry_ref[1, :, cols] = ci

    y = (jnp.dot(hre_ref[...].astype(MX), cre_ref[...], preferred_element_type=F32)
         + jnp.dot(him_ref[...].astype(MX), cim_ref[...], preferred_element_type=F32))
    y = y + d_ref[...] * u
    y = 0.5 * y * (1.0 + jnp.tanh(math.sqrt(2.0 / math.pi) * (y + 0.044715 * (y * y * y))))
    gl = jnp.dot(y.astype(MX), wglu_ref[...], preferred_element_type=F32)
    out_ref[...] = (gl[:, :BRANCH_W] * _sigmoid(gl[:, BRANCH_W:])).astype(MX)


def _s5_tables(a_re, a_im, log_dt, b_re, b_im, c_re, c_im):
    dt = jnp.exp(log_dt)[:, None]
    xr, xi = a_re * dt, a_im * dt

    pr = jnp.exp(xr) * jnp.cos(xi)
    pi = jnp.exp(xr) * jnp.sin(xi)
    den = a_re * a_re + a_im * a_im
    qr = ((pr - 1.0) * a_re + pi * a_im) / den
    qi = (pi * a_re - (pr - 1.0) * a_im) / den
    bbar_re = qr[..., None] * b_re - qi[..., None] * b_im
    bbar_im = qr[..., None] * b_im + qi[..., None] * b_re

    flat = lambda z: z.reshape(1, S5_COLS)
    powers = [(flat(pr), flat(pi))]
    for _ in range(SUBLANE - 1):
        zr, zi = powers[-1]
        powers.append((zr * flat(pr) - zi * flat(pi), zr * flat(pi) + zi * flat(pr)))
    row = jnp.arange(SUBLANE)[:, None]
    planes = []
    for sh in (1, 2, 4):
        zr, zi = powers[sh - 1]
        planes.append((jnp.where(row >= sh, zr, 0.0), jnp.where(row >= sh, zi, 0.0)))
    planes.append((jnp.concatenate([p[0] for p in powers]), jnp.concatenate([p[1] for p in powers])))
    zr, zi = powers[SUBLANE - 1]
    planes.append((jnp.broadcast_to(zr, (SUBLANE, S5_COLS)), jnp.broadcast_to(zi, (SUBLANE, S5_COLS))))
    tab = jnp.stack([jnp.stack(p) for p in planes]).astype(F32)

    eye = jnp.eye(S5_GROUPS, dtype=F32)

    def bdiag_in(m):
        return jnp.einsum('gnc,gh->gchn', m, eye).reshape(BRANCH_W, S5_COLS)

    def bdiag_out(m):
        return jnp.einsum('gcn,gh->gnhc', m, eye).reshape(S5_COLS, BRANCH_W)

    return (tab, bdiag_in(bbar_re).astype(MX), bdiag_in(bbar_im).astype(MX),
            bdiag_out(c_re).astype(MX), bdiag_out(-c_im).astype(MX))


def _s5_mixer(za, tabs, d_skip, w_glu, *, tm):
    S = za.shape[0]
    tab, bre, bim, cre, cim = tabs
    full = lambda shape: pl.BlockSpec(shape, lambda i: (0,) * len(shape))
    return pl.pallas_call(
        _s5_kernel,
        grid=(S // tm,),
        in_specs=[pl.BlockSpec((tm, BRANCH_W), lambda i: (i, ZA_S5)),
                  full(bre.shape), full(bim.shape), full(tab.shape), full(cre.shape),
                  full(cim.shape), full((1, BRANCH_W)), full(w_glu.shape)],
        out_specs=pl.BlockSpec((tm, BRANCH_W), lambda i: (i, 0)),
        out_shape=jax.ShapeDtypeStruct((S, BRANCH_W), MX),
        scratch_shapes=[pltpu.VMEM((tm, S5_COLS), F32), pltpu.VMEM((tm, S5_COLS), F32),
                        pltpu.VMEM((2, SUBLANE, S5_COLS), F32)],
        compiler_params=_params("arbitrary"),
        name="s5",
    )(za, bre, bim, tab, cre, cim, d_skip.reshape(1, BRANCH_W), w_glu)


def _pool_kernel(x_ref, w_ref, scale_ref, out_ref, ext_ref):
    tm = x_ref.shape[0]
    i = pl.program_id(0)

    @pl.when(i == 0)
    def _():
        ext_ref[0:POOL_HALO, :] = jnp.zeros((POOL_HALO, BRANCH_W), F32)

    x = x_ref[...]
    ext_ref[POOL_HALO:, :] = x
    t = i * tm + lax.broadcasted_iota(jnp.int32, (tm, 1), 0)
    for gi, w in enumerate(POOL_WINDOWS):
        cols = slice(gi * POOL_GROUP, (gi + 1) * POOL_GROUP)
        acc = x[:, cols]
        for d in range(1, w):
            acc = acc + ext_ref[POOL_HALO - d:POOL_HALO - d + tm, cols]
        count = jnp.minimum(t + 1, w).astype(F32)
        diff = (acc / count - x[:, cols]).astype(MX)
        y = jnp.dot(diff, w_ref[gi], preferred_element_type=F32)
        out_ref[:, cols] = (y * scale_ref[:, cols]).astype(MX)
    ext_ref[0:POOL_HALO, :] = x[tm - POOL_HALO:, :]


def _pool_mixer(za, w_pool, pool_scale, *, tm):
    S = za.shape[0]
    return pl.pallas_call(
        _pool_kernel,
        grid=(S // tm,),
        in_specs=[pl.BlockSpec((tm, BRANCH_W), lambda i: (i, ZA_POOL)),
                  pl.BlockSpec(w_pool.shape, lambda i: (0, 0, 0)),
                  pl.BlockSpec((1, BRANCH_W), lambda i: (0, 0))],
        out_specs=pl.BlockSpec((tm, BRANCH_W), lambda i: (i, 0)),
        out_shape=jax.ShapeDtypeStruct((S, BRANCH_W), MX),
        scratch_shapes=[pltpu.VMEM((tm + POOL_HALO, BRANCH_W), F32)],
        compiler_params=_params("arbitrary"),
        name="pool",
    )(za, w_pool, pool_scale.reshape(1, BRANCH_W))


def _ret_consts():
    H, dh, C = RET_HEADS, RET_HEAD_DIM, RET_CHUNK
    angle = jnp.repeat(1.0 / (10000.0 ** jnp.linspace(0.0, 1.0, dh // 2)), 2).reshape(1, dh)
    log_g = jnp.log(1.0 - 2.0 ** (-5.0 - jnp.arange(H, dtype=F32)))
    idx = jnp.arange(C, dtype=F32)
    rel = idx[:, None] - idx[None, :]
    decay = jnp.where(rel >= 0, jnp.exp(jnp.maximum(rel, 0.0)[None] * log_g[:, None, None]), 0.0)
    xi = jnp.exp((idx + 1.0)[None, :] * log_g[:, None])[..., None]
    zeta = jnp.exp((C - 1.0 - idx)[None, :] * log_g[:, None])[..., None]
    g_chunk = jnp.broadcast_to(jnp.exp(C * log_g)[:, None, None], (H, 1, dh))
    return angle.astype(F32), decay, xi, zeta, g_chunk


def _rotate_pairs(x, even):
    n = x.shape[-1]
    return jnp.where(even, -pltpu.roll(x, n - 1, 1), pltpu.roll(x, 1, 1))


def _ret_kernel(q_ref, k_ref, v_ref, g_ref, ang_ref, decay_ref, xi_ref, zeta_ref, gch_ref,
                out_ref, state_ref):
    C, dh = RET_CHUNK, RET_HEAD_DIM
    i = pl.program_id(0)

    @pl.when(i == 0)
    def _():
        state_ref[...] = jnp.zeros_like(state_ref)

    pos = (i * C + lax.broadcasted_iota(jnp.int32, (C, 1), 0)).astype(F32)
    ang = pos * ang_ref[...]
    sin, cos = jnp.sin(ang), jnp.cos(ang)
    even = (lax.broadcasted_iota(jnp.int32, (C, dh), 1) % 2) == 0
    for hd in range(RET_HEADS):
        cols = slice(hd * dh, (hd + 1) * dh)
        q = q_ref[:, cols]
        k = k_ref[:, cols]
        vb = v_ref[:, cols].astype(MX)
        q = q * cos + _rotate_pairs(q, even) * sin
        k = (k * cos + _rotate_pairs(k, even) * sin) * dh ** -0.5
        qb = q.astype(MX)
        s = lax.dot_general(qb, k.astype(MX), (((1,), (1,)), ((), ())),
                            preferred_element_type=F32) * decay_ref[hd]
        inner = jnp.dot(s.astype(MX), vb, preferred_element_type=F32)
        state = state_ref[hd]
        cross = jnp.dot(qb, state.astype(MX), preferred_element_type=F32) * xi_ref[hd]
        kz = (k * zeta_ref[hd]).astype(MX)
        state_ref[hd] = state * gch_ref[hd] + lax.dot_general(
            kz, vb, (((0,), (0,)), ((), ())), preferred_element_type=F32)
        y = inner + cross
        y = y * lax.rsqrt(jnp.mean(y * y, axis=-1, keepdims=True) + NORM_EPS)
        g = g_ref[:, cols]
        out_ref[:, cols] = (g * _sigmoid(g) * y).astype(MX)


def _ret_mixer(za):
    S = za.shape[0]
    C = RET_CHUNK
    consts = _ret_consts()
    col = lambda c: pl.BlockSpec((C, BRANCH_W), lambda i: (i, c))
    full = lambda a: pl.BlockSpec(a.shape, lambda i: (0,) * a.ndim)
    return pl.pallas_call(
        _ret_kernel,
        grid=(S // C,),
        in_specs=[col(ZA_RQ), col(ZA_RK), col(ZA_RV), col(ZA_RG)] + [full(a) for a in consts],
        out_specs=pl.BlockSpec((C, BRANCH_W), lambda i: (i, 0)),
        out_shape=jax.ShapeDtypeStruct((S, BRANCH_W), MX),
        scratch_shapes=[pltpu.VMEM((RET_HEADS, RET_HEAD_DIM, RET_HEAD_DIM), F32)],
        compiler_params=_params("arbitrary"),
        name="retention",
    )(za, za, za, za, *consts)


NSA_R = NSA_HEADS // NSA_KV_GROUPS
NSA_TQ = 128
NSA_TK = 512
CMP_CH = 256
P_OFF = SUBLANE
NT_DIMS = (((1,), (1,)), ((), ()))


def _compress_kernel(x_ref, w_ref, out_ref):
    out_ref[0] = jnp.dot(x_ref[0], w_ref[...], preferred_element_type=F32).astype(out_ref.dtype)


def _compress_t_kernel(x_ref, wt_ref, out_ref):
    out_ref[0] = lax.dot_general(wt_ref[...], x_ref[0], NT_DIMS,
                                 preferred_element_type=F32).astype(out_ref.dtype)


def _compress(x, w, *, transposed):
    G, nc, kk = x.shape
    if transposed:
        kern, out_block = _compress_t_kernel, (1, w.shape[0], nc)
    else:
        kern, out_block = _compress_kernel, (1, nc, w.shape[1])
    return pl.pallas_call(
        kern,
        grid=(G,),
        in_specs=[pl.BlockSpec((1, nc, kk), lambda g: (g, 0, 0)),
                  pl.BlockSpec(w.shape, lambda g: (0, 0))],
        out_specs=pl.BlockSpec(out_block, lambda g: (g, 0, 0)),
        out_shape=jax.ShapeDtypeStruct((G,) + out_block[1:], x.dtype),
        compiler_params=_params("arbitrary"),
        name="nsa_compress",
    )(x, w)


def _group_queries(q_ref, g, extra_scale=1.0):
    hd = NSA_HEAD_DIM
    base = g * NSA_R * hd
    qs = [q_ref[:, base + r * hd:base + (r + 1) * hd] for r in range(NSA_R)]
    return (jnp.concatenate(qs, axis=0) * (hd ** -0.5 * extra_scale)).astype(MX)


def _cmp_select_kernel(q_ref, kc_ref, vct_ref, ocmp_ref, sel_ref, s_sc, p_sc, ot_sc):
    i = pl.program_id(0)
    tq, hd = NSA_TQ, NSA_HEAD_DIM
    nc = kc_ref.shape[1]
    ns = sel_ref.shape[2]
    ch = min(CMP_CH, nc)
    nch = jnp.minimum((tq // CMP_STRIDE * (i + 1) + ch - 1) // ch, nc // ch)
    t4 = i * tq + lax.broadcasted_iota(jnp.int32, (1, NSA_R * tq), 1) % tq
    t1 = i * tq + lax.broadcasted_iota(jnp.int32, (1, tq), 1)
    nrow = lax.broadcasted_iota(jnp.int32, (ch, 1), 0)
    jrow = lax.broadcasted_iota(jnp.int32, (ns, 1), 0)
    jfull = jnp.broadcast_to(jrow.astype(F32), (ns, tq))

    def rows(c):
        return pl.ds(pl.multiple_of(c * ch, ch), ch)

    def visible(c):
        return ((c * ch + nrow) * CMP_STRIDE + (CMP_LEN - 1)) <= t4

    groups = range(NSA_KV_GROUPS)
    qgs = [_group_queries(q_ref, g) for g in groups]
    lanes = (1, NSA_R * tq)

    def scores(c, ms):
        vis = visible(c)
        out = []
        for g in groups:
            s = lax.dot_general(kc_ref[g, rows(c), :], qgs[g], NT_DIMS, preferred_element_type=F32)
            s = jnp.where(vis, s, NEG_INF)
            s_sc[g, rows(c), :] = s
            out.append(jnp.maximum(ms[g], jnp.max(s, axis=0, keepdims=True)))
        return tuple(out)

    ms = lax.fori_loop(0, nch, scores, tuple(jnp.full(lanes, NEG_INF, F32) for _ in groups))

    def expsum(c, ls):
        vis = visible(c)
        out = []
        for g in groups:
            e = jnp.where(vis, jnp.exp(s_sc[g, rows(c), :] - ms[g]), 0.0)
            s_sc[g, rows(c), :] = e
            out.append(ls[g] + jnp.sum(e, axis=0, keepdims=True))
        return tuple(out)

    ls = lax.fori_loop(0, nch, expsum, tuple(jnp.zeros(lanes, F32) for _ in groups))
    invs = [1.0 / jnp.maximum(l, 1e-30) for l in ls]
    p_sc[...] = jnp.zeros_like(p_sc)
    ot_sc[...] = jnp.zeros_like(ot_sc)

    def weigh(c, carry):
        for g in groups:
            p = s_sc[g, rows(c), :] * invs[g]
            p_sc[g, pl.ds(pl.multiple_of(P_OFF + c * ch, SUBLANE), ch), :] = (
                p[:, 0:tq] + p[:, tq:2 * tq] + p[:, 2 * tq:3 * tq] + p[:, 3 * tq:4 * tq])
            ot_sc[g] += jnp.dot(vct_ref[g, :, rows(c)], p.astype(MX), preferred_element_type=F32)
        return carry

    lax.fori_loop(0, nch, weigh, 0)
    for h in range(NSA_HEADS):
        g, r = divmod(h, NSA_R)
        o = ot_sc[g, :, r * tq:(r + 1) * tq].T
        ocmp_ref[:, h * hd:(h + 1) * hd] = o[:, :hd]

    ratio = SLC_BLOCK // CMP_STRIDE
    valid = jrow * SLC_BLOCK <= t1
    bonus = jnp.where((jrow == 0) | (jrow == t1 // SLC_BLOCK), FORCE_BONUS, 0.0)
    xs = []
    for g in groups:
        imp = p_sc[g, pl.ds(P_OFF - 1, ns, stride=ratio), :]
        for k in range(ratio):
            imp = imp + p_sc[g, pl.ds(P_OFF + k, ns, stride=ratio), :]
        xs.append(jnp.where(valid, imp + bonus, NEG_INF))
    for _ in range(min(SLC_TOP_N, ns)):
        for g in groups:
            mx = jnp.max(xs[g], axis=0, keepdims=True)
            cand = jnp.where(xs[g] == mx, jfull, float(ns))
            xs[g] = jnp.where(cand == jnp.min(cand, axis=0, keepdims=True), -jnp.inf, xs[g])
    for g in groups:
        sel_ref[g] = jnp.where(xs[g] == -jnp.inf, 1.0, 0.0).T.astype(BF16)


def _cmp_select(za, kc, vct):
    S = za.shape[0]
    G, nc, hd = kc.shape
    ns = S // SLC_BLOCK
    tq = NSA_TQ
    return pl.pallas_call(
        _cmp_select_kernel,
        grid=(S // tq,),
        in_specs=[pl.BlockSpec((tq, BRANCH_W), lambda i: (i, ZA_Q)),
                  pl.BlockSpec(kc.shape, lambda i: (0, 0, 0)),
                  pl.BlockSpec(vct.shape, lambda i: (0, 0, 0))],
        out_specs=[pl.BlockSpec((tq, BRANCH_W), lambda i: (i, 0)),
                   pl.BlockSpec((G, tq, ns), lambda i: (0, i, 0))],
        out_shape=[jax.ShapeDtypeStruct((S, BRANCH_W), F32),
                   jax.ShapeDtypeStruct((G, S, ns), BF16)],
        scratch_shapes=[pltpu.VMEM((G, nc, NSA_R * tq), F32), pltpu.VMEM((G, P_OFF + nc, tq), F32),
                        pltpu.VMEM((G, LANE, NSA_R * tq), F32)],
        compiler_params=_params("parallel"),
        name="nsa_cmp_select",
    )(za, kc, vct)


SLC_MASK = 2.0 ** 100
LOG2E = math.log2(math.e)


def _slc_kernel(q_ref, kx_ref, vs_ref, sel_ref, out_ref, lhs_sc, m_sc, acc_sc, *, tk):
    i = pl.program_id(1)
    tq, hd = NSA_TQ, NSA_HEAD_DIM
    ksel = kx_ref.shape[2] - LANE
    bpt = tk // SLC_BLOCK
    qg = _group_queries(q_ref, 0, LOG2E)
    pad = jnp.zeros((NSA_R * tq, LANE - hd), MX)
    for h in range(sel_ref.shape[2] // ksel):
        unsel = (1.0 - sel_ref[0, :, h * ksel:(h + 1) * ksel].astype(F32)).astype(MX)
        lhs_sc[h] = jnp.concatenate([qg, pad, jnp.concatenate([unsel] * NSA_R, axis=0)], axis=1)
    t = i * tq + lax.broadcasted_iota(jnp.int32, (tq, 1), 0)
    lane = lax.broadcasted_iota(jnp.int32, (1, tk), 1)
    m_sc[...] = jnp.full_like(m_sc, NEG_INF)
    acc_sc[...] = jnp.zeros_like(acc_sc)

    def tile(kt, diagonal):
        k0 = pl.multiple_of(kt * tk, tk)
        s = lax.dot_general(lhs_sc[(kt * bpt) // ksel], kx_ref[0, pl.ds(k0, tk), :], NT_DIMS,
                            preferred_element_type=F32)
        if diagonal:
            hidden = jnp.where(k0 + lane <= t, 0.0, NEG_INF)
            s = s + jnp.concatenate([hidden] * NSA_R, axis=0)
        m_prev = m_sc[...]
        m_new = jnp.maximum(m_prev, jnp.max(s, axis=-1, keepdims=True))
        p = jnp.exp2(s - jnp.concatenate([m_new] * (tk // LANE), axis=1))
        acc_sc[...] = jnp.exp2(m_prev - m_new) * acc_sc[...] + jnp.dot(
            p.astype(MX), vs_ref[0, pl.ds(k0, tk), :], preferred_element_type=F32)
        m_sc[...] = m_new

    last = (i * tq) // tk

    def quad(j, carry):
        for u in range(4):
            tile(4 * j + u, False)
        return carry

    lax.fori_loop(0, last // 4, quad, 0)
    done = (last // 4) * 4

    @pl.when(last - done >= 2)
    def _():
        tile(done, False)
        tile(done + 1, False)

    @pl.when((last - done) % 2 == 1)
    def _():
        tile(last - 1, False)

    tile(last, True)
    acc = acc_sc[...]
    o = acc[:, :hd] / acc[:, hd:]
    for r in range(NSA_R):
        out_ref[:, r * hd:(r + 1) * hd] = o[r * tq:(r + 1) * tq, :]


def _slc_attention(za, ks, vs, sel):
    S = za.shape[0]
    G = ks.shape[0]
    ns = sel.shape[2]
    tq, hd = NSA_TQ, NSA_HEAD_DIM
    tk = min(NSA_TK, S)
    ksel = min(LANE, ns)
    own_block = (jnp.arange(S)[:, None] // SLC_BLOCK) % ksel == jnp.arange(ksel)[None, :]
    kx = jnp.concatenate([ks, jnp.zeros((G, S, LANE - hd), ks.dtype),
                          jnp.broadcast_to(jnp.where(own_block, -SLC_MASK, 0.0).astype(ks.dtype),
                                           (G, S, ksel))], axis=2)
    gw = NSA_R * hd
    return pl.pallas_call(
        functools.partial(_slc_kernel, tk=tk),
        grid=(G, S // tq),
        in_specs=[pl.BlockSpec((tq, gw), lambda g, i: (i, g)),
                  pl.BlockSpec((1, S, LANE + ksel), lambda g, i: (g, 0, 0)),
                  pl.BlockSpec((1, S, LANE), lambda g, i: (g, 0, 0)),
                  pl.BlockSpec((1, tq, ns), lambda g, i: (g, i, 0))],
        out_specs=pl.BlockSpec((tq, gw), lambda g, i: (i, g)),
        out_shape=jax.ShapeDtypeStruct((S, BRANCH_W), F32),
        scratch_shapes=[pltpu.VMEM((ns // ksel, NSA_R * tq, LANE + ksel), MX),
                        pltpu.VMEM((NSA_R * tq, LANE), F32), pltpu.VMEM((NSA_R * tq, LANE), F32)],
        compiler_params=_params("arbitrary", "arbitrary"),
        name="nsa_selected",
    )(za, kx, vs, sel)


def _win_kernel(q_ref, kw_ref, vw_ref, gate_ref, ocmp_ref, oslc_ref, out_ref):
    i = pl.program_id(1)
    tq, hd = NSA_TQ, NSA_HEAD_DIM
    span = tq + WIN
    qg = _group_queries(q_ref, 0)
    r0 = pl.multiple_of(i * tq, tq)
    t = i * tq + lax.broadcasted_iota(jnp.int32, (tq, 1), 0)
    pos = i * tq - WIN + lax.broadcasted_iota(jnp.int32, (1, span), 1)
    ok = (pos <= t) & (pos > t - WIN) & (pos >= 0)
    s = lax.dot_general(qg, kw_ref[0, pl.ds(r0, span), :], NT_DIMS, preferred_element_type=F32)
    s = s + jnp.concatenate([jnp.where(ok, 0.0, NEG_INF)] * NSA_R, axis=0)
    e = jnp.exp(s - jnp.max(s, axis=-1, keepdims=True))
    acc = jnp.dot(e.astype(MX), vw_ref[0, pl.ds(r0, span), :], preferred_element_type=F32)
    ow = acc[:, :hd] / acc[:, hd:]
    sig = _sigmoid(gate_ref[...])
    for r in range(NSA_R):
        cols = slice(r * hd, (r + 1) * hd)
        out_ref[:, cols] = (sig[:, r:r + 1] * ocmp_ref[:, cols]
                            + sig[:, NSA_R + r:NSA_R + r + 1] * oslc_ref[:, cols]
                            + sig[:, 2 * NSA_R + r:2 * NSA_R + r + 1] * ow[r * tq:(r + 1) * tq, :]
                            ).astype(MX)


def _win_combine(za, kw, vw, ocmp, oslc):
    S = za.shape[0]
    G = kw.shape[0]
    tq, hd = NSA_TQ, NSA_HEAD_DIM
    gw = NSA_R * hd
    blk = lambda: pl.BlockSpec((tq, gw), lambda g, i: (i, g))
    return pl.pallas_call(
        _win_kernel,
        grid=(G, S // tq),
        in_specs=[blk(),
                  pl.BlockSpec((1, S + WIN, hd), lambda g, i: (g, 0, 0)),
                  pl.BlockSpec((1, S + WIN, LANE), lambda g, i: (g, 0, 0)),
                  pl.BlockSpec((tq, GATE_PAD), lambda g, i: (i, ZA_G_OFF // GATE_PAD + g)),
                  blk(), blk()],
        out_specs=blk(),
        out_shape=jax.ShapeDtypeStruct((S, BRANCH_W), MX),
        compiler_params=_params("arbitrary", "arbitrary"),
        name="nsa_window_combine",
    )(za, kw, vw, za, ocmp, oslc)


def _nsa_mixer(za, w_cmp_k, w_cmp_v):
    S = za.shape[0]
    G, hd = NSA_KV_GROUPS, NSA_HEAD_DIM
    nc = S // CMP_STRIDE

    def kv(idx):
        off = ZA_KV_OFF + idx * G * hd
        return jnp.stack([za[:, off + g * hd:off + (g + 1) * hd] for g in range(G)])

    def blocks(x):
        half = x.reshape(G, nc, CMP_STRIDE * hd)
        nxt = jnp.concatenate([half[:, 1:], jnp.zeros_like(half[:, :1])], axis=1)
        return jnp.concatenate([half, nxt], axis=2).astype(MX)

    def with_ones(x):
        return jnp.concatenate([x, jnp.ones((G, S, LANE - hd), x.dtype)], axis=2).astype(MX)

    front = lambda x: jnp.pad(x, ((0, 0), (WIN, 0), (0, 0)))
    wk = w_cmp_k.reshape(CMP_LEN * hd, hd).astype(MX)
    wvt = jnp.pad(w_cmp_v.reshape(CMP_LEN * hd, hd).T, ((0, LANE - hd), (0, 0))).astype(MX)
    kc = _compress(blocks(kv(0)), wk, transposed=False)
    vct = _compress(blocks(kv(1)), wvt, transposed=True)
    ocmp, sel = _cmp_select(za, kc, vct)
    oslc = _slc_attention(za, kv(2).astype(MX), with_ones(kv(3)), sel)
    return _win_combine(za, front(kv(4).astype(MX)), front(with_ones(kv(5))), ocmp, oslc)


def _pack_w_in(w):
    seg = lambda off, n: w[:, off:off + n]
    R = NSA_HEADS // NSA_KV_GROUPS
    gates = []
    for g in range(NSA_KV_GROUPS):
        cols = [OFF_G + br * NSA_HEADS + g * R + r for br in range(3) for r in range(R)]
        gates.append(jnp.pad(w[:, jnp.array(cols)], ((0, 0), (0, GATE_PAD - len(cols)))))
    wa = jnp.concatenate([seg(OFF_Q, BRANCH_W), seg(OFF_S5, BRANCH_W), seg(OFF_POOL, BRANCH_W),
                          seg(OFF_RQ, BRANCH_W), seg(OFF_RK, BRANCH_W), seg(OFF_RV, BRANCH_W),
                          seg(OFF_RG, BRANCH_W), seg(OFF_KV, NSA_KV_COLS)] + gates, axis=1)
    return wa.astype(MX), w[:, OFF_MERGE:].astype(MX)


def kernel(x, p, g_mix, w_in, w_cmp_k, w_cmp_v, s5_a_re, s5_a_im, s5_log_dt, s5_b_re, s5_b_im,
           s5_c_re, s5_c_im, s5_d, s5_w_glu, pool_w, pool_scale, w_branch, w_out, g_mlp,
           w_mlp_up, w_mlp_down, w_ple_gate, w_ple_proj, g_final):
    B, S, D = x.shape
    assert B == 1 and D == D_MODEL
    depth = w_in.shape[0]
    h = x.reshape(S, D)
    for i in range(depth):
        wa, wm = _pack_w_in(w_in[i])
        za, u = _inproj(h, g_mix[i].reshape(1, D), wa, tm=1024, tn=768)

        o_nsa = _nsa_mixer(za, w_cmp_k[i], w_cmp_v[i])
        s5_tabs = _s5_tables(s5_a_re[i], s5_a_im[i], s5_log_dt[i], s5_b_re[i], s5_b_im[i],
                             s5_c_re[i], s5_c_im[i])
        o_s5 = _s5_mixer(za, s5_tabs, s5_d[i], s5_w_glu[i].astype(MX), tm=512)
        o_pool = _pool_mixer(za, pool_w[i].astype(MX), pool_scale[i], tm=1024)
        o_ret = _ret_mixer(za)

        merged = _merge(u, (o_nsa, o_s5, o_pool, o_ret), wm, w_branch[i].astype(MX),
                        tm=1024, tn=256)
        h = _resid_mm(h, merged, w_out[i].astype(MX), tm=1024, tn=1024)
        h = _mlp(h, g_mlp[i].reshape(1, D), w_mlp_up[i].astype(MX),
                 w_mlp_down[i].astype(MX), tm=512, tf=512)
        h = _ple(h, p[i].reshape(S, PLE_DIM), w_ple_gate[i].astype(MX),
                 w_ple_proj[i].astype(MX), g_final.reshape(1, D), tm=512,
                 final_norm=(i == depth - 1))
    return h.reshape(B, S, D)
```

```python
import functools
import math

import jax
import jax.numpy as jnp
from jax import lax
from jax.experimental import pallas as pl
from jax.experimental.pallas import tpu as pltpu

F32 = jnp.float32
BF16 = jnp.bfloat16
MX = BF16

D_MODEL = 2048
N_BRANCH = 4
BRANCH_W = D_MODEL // N_BRANCH
NSA_HEADS = 8
NSA_KV_GROUPS = 2
NSA_HEAD_DIM = BRANCH_W // NSA_HEADS
CMP_LEN = 32
CMP_STRIDE = 16
SLC_BLOCK = 64
SLC_TOP_N = 16
WIN = 512
Q_BLOCK = 128
S5_GROUP = 16
S5_GROUPS = BRANCH_W // S5_GROUP
S5_STATE = 64
S5_COLS = S5_GROUPS * S5_STATE
POOL_WINDOWS = (2, 4, 8, 16)
POOL_GROUP = BRANCH_W // len(POOL_WINDOWS)
POOL_HALO = 16
RET_HEADS = 4
RET_HEAD_DIM = BRANCH_W // RET_HEADS
RET_CHUNK = 128
D_FF = 4 * D_MODEL
PLE_DIM = 256
NORM_EPS = 1e-6
NEG_INF = -1e30
FORCE_BONUS = 1e4

NSA_KV_COLS = 6 * NSA_KV_GROUPS * NSA_HEAD_DIM
NSA_GATE_COLS = 3 * NSA_HEADS
OFF_Q = 0
OFF_KV = OFF_Q + BRANCH_W
OFF_G = OFF_KV + NSA_KV_COLS
OFF_S5 = OFF_G + NSA_GATE_COLS
OFF_POOL = OFF_S5 + BRANCH_W
OFF_RQ = OFF_POOL + BRANCH_W
OFF_RK = OFF_RQ + BRANCH_W
OFF_RV = OFF_RK + BRANCH_W
OFF_RG = OFF_RV + BRANCH_W
OFF_MERGE = OFF_RG + BRANCH_W

LANE = 128
SUBLANE = 8
GATE_PAD = LANE
ZA_Q, ZA_S5, ZA_POOL, ZA_RQ, ZA_RK, ZA_RV, ZA_RG = range(7)
ZA_KV_OFF = 7 * BRANCH_W
ZA_G_OFF = ZA_KV_OFF + NSA_KV_COLS
ZA_COLS = ZA_G_OFF + NSA_KV_GROUPS * GATE_PAD

VMEM_LIMIT = 56 * 1024 * 1024


def _params(*sem):
    return pltpu.CompilerParams(dimension_semantics=sem, vmem_limit_bytes=VMEM_LIMIT)


def _rms(x, g):
    return x * lax.rsqrt(jnp.mean(x * x, axis=-1, keepdims=True) + NORM_EPS) * g


def _sigmoid(x):
    return 1.0 / (1.0 + jnp.exp(-x))


def _inproj_kernel(h_ref, g_ref, w_ref, z_ref, u_ref):
    @pl.when(pl.program_id(1) == 0)
    def _():
        u_ref[...] = _rms(h_ref[...], g_ref[...]).astype(MX)

    z_ref[...] = jnp.dot(u_ref[...], w_ref[...], preferred_element_type=F32)


def _inproj(h, g, w, *, tm, tn):
    S, D = h.shape
    N = w.shape[1]
    return pl.pallas_call(
        _inproj_kernel,
        grid=(S // tm, N // tn),
        in_specs=[pl.BlockSpec((tm, D), lambda i, j: (i, 0)),
                  pl.BlockSpec((1, D), lambda i, j: (0, 0)),
                  pl.BlockSpec((D, tn), lambda i, j: (0, j))],
        out_specs=[pl.BlockSpec((tm, tn), lambda i, j: (i, j)),
                   pl.BlockSpec((tm, D), lambda i, j: (i, 0))],
        out_shape=[jax.ShapeDtypeStruct((S, N), F32), jax.ShapeDtypeStruct((S, D), MX)],
        compiler_params=_params("parallel", "arbitrary"),
        name="inproj",
    )(h, g, w)


def _merge_kernel(u_ref, o0_ref, o1_ref, o2_ref, o3_ref, wm0_ref, wm1_ref, wm2_ref, wm3_ref,
                  wb_ref, out_ref):
    u = u_ref[...]
    acc = None
    for j, (o_ref, wm_ref) in enumerate(((o0_ref, wm0_ref), (o1_ref, wm1_ref),
                                         (o2_ref, wm2_ref), (o3_ref, wm3_ref))):
        gate = _sigmoid(jnp.dot(u, wm_ref[...], preferred_element_type=F32))
        br = jnp.dot(o_ref[...], wb_ref[j], preferred_element_type=F32)
        acc = gate * br if acc is None else acc + gate * br
    out_ref[...] = acc.astype(MX)


def _merge(u, outs, wm, wb, *, tm, tn):
    S, D = u.shape
    nt = D // tn
    o_specs = [pl.BlockSpec((tm, BRANCH_W), lambda i, c: (i, 0)) for _ in range(N_BRANCH)]
    wm_specs = [pl.BlockSpec((D, tn), functools.partial(lambda i, c, j: (0, j * nt + c), j=j))
                for j in range(N_BRANCH)]
    return pl.pallas_call(
        _merge_kernel,
        grid=(S // tm, nt),
        in_specs=[pl.BlockSpec((tm, D), lambda i, c: (i, 0))] + o_specs + wm_specs
                 + [pl.BlockSpec((N_BRANCH, BRANCH_W, tn), lambda i, c: (0, 0, c))],
        out_specs=pl.BlockSpec((tm, tn), lambda i, c: (i, c)),
        out_shape=jax.ShapeDtypeStruct((S, D), MX),
        compiler_params=_params("parallel", "arbitrary"),
        name="merge",
    )(u, *outs, wm, wm, wm, wm, wb)


def _resid_mm_kernel(h_ref, x_ref, w_ref, out_ref):
    out_ref[...] = h_ref[...] + jnp.dot(x_ref[...], w_ref[...], preferred_element_type=F32)


def _resid_mm(h, x, w, *, tm, tn):
    S, D = h.shape
    K = x.shape[1]
    return pl.pallas_call(
        _resid_mm_kernel,
        grid=(S // tm, D // tn),
        in_specs=[pl.BlockSpec((tm, tn), lambda i, c: (i, c)),
                  pl.BlockSpec((tm, K), lambda i, c: (i, 0)),
                  pl.BlockSpec((K, tn), lambda i, c: (0, c))],
        out_specs=pl.BlockSpec((tm, tn), lambda i, c: (i, c)),
        out_shape=jax.ShapeDtypeStruct((S, D), F32),
        compiler_params=_params("parallel", "arbitrary"),
        name="resid_mm",
    )(h, x, w)


def _mlp_kernel(h_ref, g_ref, wu_ref, wd_ref, out_ref, v_ref):
    @pl.when(pl.program_id(1) == 0)
    def _():
        h = h_ref[...]
        v_ref[...] = _rms(h, g_ref[...]).astype(MX)
        out_ref[...] = h

    hid = jnp.maximum(jnp.dot(v_ref[...], wu_ref[...], preferred_element_type=F32), 0.0)
    out_ref[...] += jnp.dot((hid * hid).astype(MX), wd_ref[...], preferred_element_type=F32)


def _mlp(h, g, wu, wd, *, tm, tf):
    S, D = h.shape
    return pl.pallas_call(
        _mlp_kernel,
        grid=(S // tm, wu.shape[1] // tf),
        in_specs=[pl.BlockSpec((tm, D), lambda i, f: (i, 0)),
                  pl.BlockSpec((1, D), lambda i, f: (0, 0)),
                  pl.BlockSpec((D, tf), lambda i, f: (0, f)),
                  pl.BlockSpec((tf, D), lambda i, f: (f, 0))],
        out_specs=pl.BlockSpec((tm, D), lambda i, f: (i, 0)),
        out_shape=jax.ShapeDtypeStruct((S, D), F32),
        scratch_shapes=[pltpu.VMEM((tm, D), MX)],
        compiler_params=_params("parallel", "arbitrary"),
        name="mlp",
    )(h, g, wu, wd)


def _ple_kernel(h_ref, p_ref, wg_ref, wp_ref, gf_ref, out_ref, *, final_norm):
    h = h_ref[...]
    gate = _sigmoid(jnp.dot(h.astype(MX), wg_ref[...], preferred_element_type=F32))
    emb = jnp.dot(p_ref[...].astype(MX), wp_ref[...], preferred_element_type=F32)
    y = h + gate * emb
    out_ref[...] = _rms(y, gf_ref[...]) if final_norm else y


def _ple(h, p, wg, wp, gf, *, tm, final_norm):
    S, D = h.shape
    P = p.shape[1]
    return pl.pallas_call(
        functools.partial(_ple_kernel, final_norm=final_norm),
        grid=(S // tm,),
        in_specs=[pl.BlockSpec((tm, D), lambda i: (i, 0)),
                  pl.BlockSpec((tm, P), lambda i: (i, 0)),
                  pl.BlockSpec((D, D), lambda i: (0, 0)),
                  pl.BlockSpec((P, D), lambda i: (0, 0)),
                  pl.BlockSpec((1, D), lambda i: (0, 0))],
        out_specs=pl.BlockSpec((tm, D), lambda i: (i, 0)),
        out_shape=jax.ShapeDtypeStruct((S, D), F32),
        compiler_params=_params("parallel"),
        name="ple",
    )(h, p, wg, wp, gf)


S5_PANELS = BRANCH_W // LANE
S5_PANEL_COLS = S5_COLS // S5_PANELS
S5_CW = 256
T_A1, T_A2, T_A4, T_ROW, T_A8 = range(5)


def _cmul_add(xr, xi, ar, ai, br, bi):
    return xr + (ar * br - ai * bi), xi + (ar * bi + ai * br)


def _s5_kernel(u_ref, bre_ref, bim_ref, tab_ref, cre_ref, cim_ref, d_ref, wglu_ref, out_ref,
               hre_ref, him_ref, carry_ref):
    tm = u_ref.shape[0]

    @pl.when(pl.program_id(0) == 0)
    def _():
        carry_ref[...] = jnp.zeros_like(carry_ref)

    u = u_ref[...]
    ub = u.astype(MX)
    for k in range(S5_PANELS):
        uk = ub[:, k * LANE:(k + 1) * LANE]
        cols = slice(k * S5_PANEL_COLS, (k + 1) * S5_PANEL_COLS)
        hre_ref[:, cols] = jnp.dot(uk, bre_ref[k], preferred_element_type=F32)
        him_ref[:, cols] = jnp.dot(uk, bim_ref[k], preferred_element_type=F32)

    for cb in range(S5_COLS // S5_CW):
        cols = slice(cb * S5_CW, (cb + 1) * S5_CW)
        tabs = [(tab_ref[k, 0, :, cols], tab_ref[k, 1, :, cols]) for k in range(5)]

        def body(r, carry, cols=cols, tabs=tabs):
            cr, ci = carry
            rows = pl.ds(pl.multiple_of(r * SUBLANE, SUBLANE), SUBLANE)
            xr = hre_ref[rows, cols]
            xi = him_ref[rows, cols]
            for k, sh in ((T_A1, 1), (T_A2, 2), (T_A4, 4)):
                ar, ai = tabs[k]
                xr, xi = _cmul_add(xr, xi, ar, ai, pltpu.roll(xr, sh, 0), pltpu.roll(xi, sh, 0))
            ar, ai = tabs[T_ROW]
            hr, hi = _cmul_add(xr, xi, ar, ai, cr, ci)
            hre_ref[rows, cols] = hr
            him_ref[rows, cols] = hi
            ar, ai = tabs[T_A8]
            lr = jnp.broadcast_to(xr[SUBLANE - 1:SUBLANE, :], xr.shape)
            li = jnp.broadcast_to(xi[SUBLANE - 1:SUBLANE, :], xi.shape)
            return _cmul_add(lr, li, ar, ai, cr, ci)

        cr, ci = lax.fori_loop(0, tm // SUBLANE, body,
                               (carry_ref[0, :, cols], carry_ref[1, :, cols]), unroll=4)
        carry_ref[0, :, cols] = cr
        carry_ref[1, :, cols] = ci

    ys = []
    for k in range(S5_PANELS):
        cols = slice(k * S5_PANEL_COLS, (k + 1) * S5_PANEL_COLS)
        ys.append(jnp.dot(hre_ref[:, cols].astype(MX), cre_ref[k], preferred_element_type=F32)
                  + jnp.dot(him_ref[:, cols].astype(MX), cim_ref[k], preferred_element_type=F32))
    y = jnp.concatenate(ys, axis=1) + d_ref[...] * u
    y = 0.5 * y * (1.0 + jnp.tanh(math.sqrt(2.0 / math.pi) * (y + 0.044715 * (y * y * y))))
    gl = jnp.dot(y.astype(MX), wglu_ref[...], preferred_element_type=F32)
    out_ref[...] = (gl[:, :BRANCH_W] * _sigmoid(gl[:, BRANCH_W:])).astype(MX)


def _s5_tables(a_re, a_im, log_dt, b_re, b_im, c_re, c_im):
    dt = jnp.exp(log_dt)[:, None]
    xr, xi = a_re * dt, a_im * dt

    pr = jnp.exp(xr) * jnp.cos(xi)
    pi = jnp.exp(xr) * jnp.sin(xi)
    den = a_re * a_re + a_im * a_im
    qr = ((pr - 1.0) * a_re + pi * a_im) / den
    qi = (pi * a_re - (pr - 1.0) * a_im) / den
    bbar_re = qr[..., None] * b_re - qi[..., None] * b_im
    bbar_im = qr[..., None] * b_im + qi[..., None] * b_re

    flat = lambda z: z.reshape(1, S5_COLS)
    powers = [(flat(pr), flat(pi))]
    for _ in range(SUBLANE - 1):
        zr, zi = powers[-1]
        powers.append((zr * flat(pr) - zi * flat(pi), zr * flat(pi) + zi * flat(pr)))
    row = jnp.arange(SUBLANE)[:, None]
    planes = []
    for sh in (1, 2, 4):
        zr, zi = powers[sh - 1]
        planes.append((jnp.where(row >= sh, zr, 0.0), jnp.where(row >= sh, zi, 0.0)))
    planes.append((jnp.concatenate([p[0] for p in powers]), jnp.concatenate([p[1] for p in powers])))
    zr, zi = powers[SUBLANE - 1]
    planes.append((jnp.broadcast_to(zr, (SUBLANE, S5_COLS)), jnp.broadcast_to(zi, (SUBLANE, S5_COLS))))
    tab = jnp.stack([jnp.stack(p) for p in planes]).astype(F32)

    pg = S5_GROUPS // S5_PANELS
    eye = jnp.eye(pg, dtype=F32)

    def bdiag_in(m):
        m = m.reshape(S5_PANELS, pg, S5_STATE, S5_GROUP)
        return jnp.einsum('pgnc,gh->pgchn', m, eye).reshape(S5_PANELS, LANE, S5_PANEL_COLS)

    def bdiag_out(m):
        m = m.reshape(S5_PANELS, pg, S5_GROUP, S5_STATE)
        return jnp.einsum('pgcn,gh->pgnhc', m, eye).reshape(S5_PANELS, S5_PANEL_COLS, LANE)

    return (tab, bdiag_in(bbar_re).astype(MX), bdiag_in(bbar_im).astype(MX),
            bdiag_out(c_re).astype(MX), bdiag_out(-c_im).astype(MX))


def _s5_mixer(za, tabs, d_skip, w_glu, *, tm):
    S = za.shape[0]
    tab, bre, bim, cre, cim = tabs
    full = lambda shape: pl.BlockSpec(shape, lambda i: (0,) * len(shape))
    return pl.pallas_call(
        _s5_kernel,
        grid=(S // tm,),
        in_specs=[pl.BlockSpec((tm, BRANCH_W), lambda i: (i, ZA_S5)),
                  full(bre.shape), full(bim.shape), full(tab.shape), full(cre.shape),
                  full(cim.shape), full((1, BRANCH_W)), full(w_glu.shape)],
        out_specs=pl.BlockSpec((tm, BRANCH_W), lambda i: (i, 0)),
        out_shape=jax.ShapeDtypeStruct((S, BRANCH_W), MX),
        scratch_shapes=[pltpu.VMEM((tm, S5_COLS), F32), pltpu.VMEM((tm, S5_COLS), F32),
                        pltpu.VMEM((2, SUBLANE, S5_COLS), F32)],
        compiler_params=_params("arbitrary"),
        name="s5",
    )(za, bre, bim, tab, cre, cim, d_skip.reshape(1, BRANCH_W), w_glu)


def _pool_kernel(x_ref, w_ref, scale_ref, out_ref, ext_ref):
    tm = x_ref.shape[0]
    i = pl.program_id(0)

    @pl.when(i == 0)
    def _():
        ext_ref[0:POOL_HALO, :] = jnp.zeros((POOL_HALO, BRANCH_W), F32)

    x = x_ref[...]
    ext_ref[POOL_HALO:, :] = x
    t = i * tm + lax.broadcasted_iota(jnp.int32, (tm, 1), 0)
    for gi, w in enumerate(POOL_WINDOWS):
        cols = slice(gi * POOL_GROUP, (gi + 1) * POOL_GROUP)
        acc = x[:, cols]
        for d in range(1, w):
            acc = acc + ext_ref[POOL_HALO - d:POOL_HALO - d + tm, cols]
        count = jnp.minimum(t + 1, w).astype(F32)
        diff = (acc / count - x[:, cols]).astype(MX)
        y = jnp.dot(diff, w_ref[gi], preferred_element_type=F32)
        out_ref[:, cols] = (y * scale_ref[:, cols]).astype(MX)
    ext_ref[0:POOL_HALO, :] = x[tm - POOL_HALO:, :]


def _pool_mixer(za, w_pool, pool_scale, *, tm):
    S = za.shape[0]
    return pl.pallas_call(
        _pool_kernel,
        grid=(S // tm,),
        in_specs=[pl.BlockSpec((tm, BRANCH_W), lambda i: (i, ZA_POOL)),
                  pl.BlockSpec(w_pool.shape, lambda i: (0, 0, 0)),
                  pl.BlockSpec((1, BRANCH_W), lambda i: (0, 0))],
        out_specs=pl.BlockSpec((tm, BRANCH_W), lambda i: (i, 0)),
        out_shape=jax.ShapeDtypeStruct((S, BRANCH_W), MX),
        scratch_shapes=[pltpu.VMEM((tm + POOL_HALO, BRANCH_W), F32)],
        compiler_params=_params("arbitrary"),
        name="pool",
    )(za, w_pool, pool_scale.reshape(1, BRANCH_W))


def _ret_consts():
    H, dh, C = RET_HEADS, RET_HEAD_DIM, RET_CHUNK
    angle = jnp.repeat(1.0 / (10000.0 ** jnp.linspace(0.0, 1.0, dh // 2)), 2).reshape(1, dh)
    log_g = jnp.log(1.0 - 2.0 ** (-5.0 - jnp.arange(H, dtype=F32)))
    idx = jnp.arange(C, dtype=F32)
    rel = idx[:, None] - idx[None, :]
    decay = jnp.where(rel >= 0, jnp.exp(jnp.maximum(rel, 0.0)[None] * log_g[:, None, None]), 0.0)
    xi = jnp.exp((idx + 1.0)[None, :] * log_g[:, None])[..., None]
    zeta = jnp.exp((C - 1.0 - idx)[None, :] * log_g[:, None])[..., None]
    g_chunk = jnp.broadcast_to(jnp.exp(C * log_g)[:, None, None], (H, 1, dh))
    return angle.astype(F32), decay, xi, zeta, g_chunk


def _rotate_pairs(x, even):
    n = x.shape[-1]
    return jnp.where(even, -pltpu.roll(x, n - 1, 1), pltpu.roll(x, 1, 1))


def _ret_kernel(q_ref, k_ref, v_ref, g_ref, ang_ref, decay_ref, xi_ref, zeta_ref, gch_ref,
                out_ref, state_ref):
    C, dh = RET_CHUNK, RET_HEAD_DIM
    i = pl.program_id(0)

    @pl.when(i == 0)
    def _():
        state_ref[...] = jnp.zeros_like(state_ref)

    pos = (i * C + lax.broadcasted_iota(jnp.int32, (C, 1), 0)).astype(F32)
    ang = pos * ang_ref[...]
    sin, cos = jnp.sin(ang), jnp.cos(ang)
    even = (lax.broadcasted_iota(jnp.int32, (C, dh), 1) % 2) == 0
    for hd in range(RET_HEADS):
        cols = slice(hd * dh, (hd + 1) * dh)
        q = q_ref[:, cols]
        k = k_ref[:, cols]
        vb = v_ref[:, cols].astype(MX)
        q = q * cos + _rotate_pairs(q, even) * sin
        k = (k * cos + _rotate_pairs(k, even) * sin) * dh ** -0.5
        qb = q.astype(MX)
        s = lax.dot_general(qb, k.astype(MX), (((1,), (1,)), ((), ())),
                            preferred_element_type=F32) * decay_ref[hd]
        inner = jnp.dot(s.astype(MX), vb, preferred_element_type=F32)
        state = state_ref[hd]
        cross = jnp.dot(qb, state.astype(MX), preferred_element_type=F32) * xi_ref[hd]
        kz = (k * zeta_ref[hd]).astype(MX)
        state_ref[hd] = state * gch_ref[hd] + lax.dot_general(
            kz, vb, (((0,), (0,)), ((), ())), preferred_element_type=F32)
        y = inner + cross
        y = y * lax.rsqrt(jnp.mean(y * y, axis=-1, keepdims=True) + NORM_EPS)
        g = g_ref[:, cols]
        out_ref[:, cols] = (g * _sigmoid(g) * y).astype(MX)


def _ret_mixer(za):
    S = za.shape[0]
    C = RET_CHUNK
    consts = _ret_consts()
    col = lambda c: pl.BlockSpec((C, BRANCH_W), lambda i: (i, c))
    full = lambda a: pl.BlockSpec(a.shape, lambda i: (0,) * a.ndim)
    return pl.pallas_call(
        _ret_kernel,
        grid=(S // C,),
        in_specs=[col(ZA_RQ), col(ZA_RK), col(ZA_RV), col(ZA_RG)] + [full(a) for a in consts],
        out_specs=pl.BlockSpec((C, BRANCH_W), lambda i: (i, 0)),
        out_shape=jax.ShapeDtypeStruct((S, BRANCH_W), MX),
        scratch_shapes=[pltpu.VMEM((RET_HEADS, RET_HEAD_DIM, RET_HEAD_DIM), F32)],
        compiler_params=_params("arbitrary"),
        name="retention",
    )(za, za, za, za, *consts)


NSA_R = NSA_HEADS // NSA_KV_GROUPS
NSA_TQ = 128
SLC_TQ = 256
NSA_TK = 512
CMP_CH = 256
P_OFF = SUBLANE
NT_DIMS = (((1,), (1,)), ((), ()))


def _compress_kernel(x_ref, w_ref, out_ref):
    out_ref[0] = jnp.dot(x_ref[0], w_ref[...], preferred_element_type=F32).astype(out_ref.dtype)


def _compress_t_kernel(x_ref, wt_ref, out_ref):
    out_ref[0] = lax.dot_general(wt_ref[...], x_ref[0], NT_DIMS,
                                 preferred_element_type=F32).astype(out_ref.dtype)


def _compress(x, w, *, transposed):
    G, nc, kk = x.shape
    if transposed:
        kern, out_block = _compress_t_kernel, (1, w.shape[0], nc)
    else:
        kern, out_block = _compress_kernel, (1, nc, w.shape[1])
    return pl.pallas_call(
        kern,
        grid=(G,),
        in_specs=[pl.BlockSpec((1, nc, kk), lambda g: (g, 0, 0)),
                  pl.BlockSpec(w.shape, lambda g: (0, 0))],
        out_specs=pl.BlockSpec(out_block, lambda g: (g, 0, 0)),
        out_shape=jax.ShapeDtypeStruct((G,) + out_block[1:], x.dtype),
        compiler_params=_params("arbitrary"),
        name="nsa_compress",
    )(x, w)


def _group_queries(q_ref, g, extra_scale=1.0):
    hd = NSA_HEAD_DIM
    base = g * NSA_R * hd
    qs = [q_ref[:, base + r * hd:base + (r + 1) * hd] for r in range(NSA_R)]
    return (jnp.concatenate(qs, axis=0) * (hd ** -0.5 * extra_scale)).astype(MX)


def _cmp_select_kernel(q_ref, kc_ref, vct_ref, ocmp_ref, sel_ref, s_sc, p_sc, ot_sc):
    i = pl.program_id(0)
    tq, hd = NSA_TQ, NSA_HEAD_DIM
    nc = kc_ref.shape[1]
    ns = sel_ref.shape[2]
    ch = min(CMP_CH, nc)
    nch = jnp.minimum((tq // CMP_STRIDE * (i + 1) + ch - 1) // ch, nc // ch)
    t4 = i * tq + lax.broadcasted_iota(jnp.int32, (1, NSA_R * tq), 1) % tq
    t1 = i * tq + lax.broadcasted_iota(jnp.int32, (1, tq), 1)
    nrow = lax.broadcasted_iota(jnp.int32, (ch, 1), 0)
    jrow = lax.broadcasted_iota(jnp.int32, (ns, 1), 0)
    jfull = jnp.broadcast_to(jrow.astype(F32), (ns, tq))

    def rows(c):
        return pl.ds(pl.multiple_of(c * ch, ch), ch)

    def visible(c):
        return ((c * ch + nrow) * CMP_STRIDE + (CMP_LEN - 1)) <= t4

    groups = range(NSA_KV_GROUPS)
    qgs = [_group_queries(q_ref, g) for g in groups]
    lanes = (1, NSA_R * tq)

    def scores(c, ms):
        vis = visible(c)
        out = []
        for g in groups:
            s = lax.dot_general(kc_ref[g, rows(c), :], qgs[g], NT_DIMS, preferred_element_type=F32)
            s = jnp.where(vis, s, NEG_INF)
            s_sc[g, rows(c), :] = s
            out.append(jnp.maximum(ms[g], jnp.max(s, axis=0, keepdims=True)))
        return tuple(out)

    ms = lax.fori_loop(0, nch, scores, tuple(jnp.full(lanes, NEG_INF, F32) for _ in groups))

    def expsum(c, ls):
        vis = visible(c)
        out = []
        for g in groups:
            e = jnp.where(vis, jnp.exp(s_sc[g, rows(c), :] - ms[g]), 0.0)
            s_sc[g, rows(c), :] = e
            out.append(ls[g] + jnp.sum(e, axis=0, keepdims=True))
        return tuple(out)

    ls = lax.fori_loop(0, nch, expsum, tuple(jnp.zeros(lanes, F32) for _ in groups))
    invs = [1.0 / jnp.maximum(l, 1e-30) for l in ls]
    p_sc[...] = jnp.zeros_like(p_sc)
    ot_sc[...] = jnp.zeros_like(ot_sc)

    def weigh(c, carry):
        for g in groups:
            p = s_sc[g, rows(c), :] * invs[g]
            p_sc[g, pl.ds(pl.multiple_of(P_OFF + c * ch, SUBLANE), ch), :] = (
                p[:, 0:tq] + p[:, tq:2 * tq] + p[:, 2 * tq:3 * tq] + p[:, 3 * tq:4 * tq])
            ot_sc[g] += jnp.dot(vct_ref[g, :, rows(c)], p.astype(MX), preferred_element_type=F32)
        return carry

    lax.fori_loop(0, nch, weigh, 0)
    for h in range(NSA_HEADS):
        g, r = divmod(h, NSA_R)
        o = ot_sc[g, :, r * tq:(r + 1) * tq].T
        ocmp_ref[:, h * hd:(h + 1) * hd] = o[:, :hd]

    ratio = SLC_BLOCK // CMP_STRIDE
    valid = jrow * SLC_BLOCK <= t1
    bonus = jnp.where((jrow == 0) | (jrow == t1 // SLC_BLOCK), FORCE_BONUS, 0.0)
    xs = []
    for g in groups:
        imp = p_sc[g, pl.ds(P_OFF - 1, ns, stride=ratio), :]
        for k in range(ratio):
            imp = imp + p_sc[g, pl.ds(P_OFF + k, ns, stride=ratio), :]
        xs.append(jnp.where(valid, imp + bonus, NEG_INF))
    for _ in range(min(SLC_TOP_N, ns)):
        for g in groups:
            mx = jnp.max(xs[g], axis=0, keepdims=True)
            cand = jnp.where(xs[g] == mx, jfull, float(ns))
            xs[g] = jnp.where(cand == jnp.min(cand, axis=0, keepdims=True), -jnp.inf, xs[g])
    for g in groups:
        sel_ref[g] = jnp.where(xs[g] == -jnp.inf, 1.0, 0.0).T.astype(BF16)


def _cmp_select(za, kc, vct):
    S = za.shape[0]
    G, nc, hd = kc.shape
    ns = S // SLC_BLOCK
    tq = NSA_TQ
    return pl.pallas_call(
        _cmp_select_kernel,
        grid=(S // tq,),
        in_specs=[pl.BlockSpec((tq, BRANCH_W), lambda i: (i, ZA_Q)),
                  pl.BlockSpec(kc.shape, lambda i: (0, 0, 0)),
                  pl.BlockSpec(vct.shape, lambda i: (0, 0, 0))],
        out_specs=[pl.BlockSpec((tq, BRANCH_W), lambda i: (i, 0)),
                   pl.BlockSpec((G, tq, ns), lambda i: (0, i, 0))],
        out_shape=[jax.ShapeDtypeStruct((S, BRANCH_W), F32),
                   jax.ShapeDtypeStruct((G, S, ns), BF16)],
        scratch_shapes=[pltpu.VMEM((G, nc, NSA_R * tq), F32), pltpu.VMEM((G, P_OFF + nc, tq), F32),
                        pltpu.VMEM((G, LANE, NSA_R * tq), F32)],
        compiler_params=_params("parallel"),
        name="nsa_cmp_select",
    )(za, kc, vct)


SLC_MASK = 2.0 ** 100
LOG2E = math.log2(math.e)


def _slc_kernel(q_ref, kx_ref, vs_ref, sel_ref, out_ref, lhs_sc, m_sc, acc_sc, *, tk):
    i = pl.program_id(1)
    tq, hd = SLC_TQ, NSA_HEAD_DIM
    ksel = kx_ref.shape[2] - LANE
    bpt = tk // SLC_BLOCK
    qg = _group_queries(q_ref, 0, LOG2E)
    pad = jnp.zeros((NSA_R * tq, LANE - hd), MX)
    for h in range(sel_ref.shape[2] // ksel):
        unsel = (1.0 - sel_ref[0, :, h * ksel:(h + 1) * ksel].astype(F32)).astype(MX)
        lhs_sc[h] = jnp.concatenate([qg, pad, jnp.concatenate([unsel] * NSA_R, axis=0)], axis=1)
    t = i * tq + lax.broadcasted_iota(jnp.int32, (tq, 1), 0)
    lane = lax.broadcasted_iota(jnp.int32, (1, tk), 1)
    m_sc[...] = jnp.full_like(m_sc, NEG_INF)
    acc_sc[...] = jnp.zeros_like(acc_sc)

    def tile(kt, diagonal):
        k0 = pl.multiple_of(kt * tk, tk)
        s = lax.dot_general(lhs_sc[(kt * bpt) // ksel], kx_ref[0, pl.ds(k0, tk), :], NT_DIMS,
                            preferred_element_type=F32)
        if diagonal:
            hidden = jnp.where(k0 + lane <= t, 0.0, NEG_INF)
            s = s + jnp.concatenate([hidden] * NSA_R, axis=0)
        m_prev = m_sc[...]
        m_new = jnp.maximum(m_prev, jnp.max(s, axis=-1, keepdims=True))
        p = jnp.exp2(s - jnp.concatenate([m_new] * (tk // LANE), axis=1))
        acc_sc[...] = jnp.exp2(m_prev - m_new) * acc_sc[...] + jnp.dot(
            p.astype(MX), vs_ref[0, pl.ds(k0, tk), :], preferred_element_type=F32)
        m_sc[...] = m_new

    last = (i * tq) // tk

    def quad(j, carry):
        for u in range(4):
            tile(4 * j + u, False)
        return carry

    lax.fori_loop(0, last // 4, quad, 0)
    done = (last // 4) * 4

    @pl.when(last - done >= 2)
    def _():
        tile(done, False)
        tile(done + 1, False)

    @pl.when((last - done) % 2 == 1)
    def _():
        tile(last - 1, False)

    tile(last, True)
    acc = acc_sc[...]
    o = acc[:, :hd] / acc[:, hd:]
    for r in range(NSA_R):
        out_ref[:, r * hd:(r + 1) * hd] = o[r * tq:(r + 1) * tq, :]


def _slc_attention(za, ks, vs, sel):
    S = za.shape[0]
    G = ks.shape[0]
    ns = sel.shape[2]
    tq, hd = SLC_TQ, NSA_HEAD_DIM
    tk = min(NSA_TK, S)
    ksel = min(LANE, ns)
    own_block = (jnp.arange(S)[:, None] // SLC_BLOCK) % ksel == jnp.arange(ksel)[None, :]
    kx = jnp.concatenate([ks, jnp.zeros((G, S, LANE - hd), ks.dtype),
                          jnp.broadcast_to(jnp.where(own_block, -SLC_MASK, 0.0).astype(ks.dtype),
                                           (G, S, ksel))], axis=2)
    gw = NSA_R * hd
    return pl.pallas_call(
        functools.partial(_slc_kernel, tk=tk),
        grid=(G, S // tq),
        in_specs=[pl.BlockSpec((tq, gw), lambda g, i: (i, g)),
                  pl.BlockSpec((1, S, LANE + ksel), lambda g, i: (g, 0, 0)),
                  pl.BlockSpec((1, S, LANE), lambda g, i: (g, 0, 0)),
                  pl.BlockSpec((1, tq, ns), lambda g, i: (g, i, 0))],
        out_specs=pl.BlockSpec((tq, gw), lambda g, i: (i, g)),
        out_shape=jax.ShapeDtypeStruct((S, BRANCH_W), F32),
        scratch_shapes=[pltpu.VMEM((ns // ksel, NSA_R * tq, LANE + ksel), MX),
                        pltpu.VMEM((NSA_R * tq, LANE), F32), pltpu.VMEM((NSA_R * tq, LANE), F32)],
        compiler_params=_params("arbitrary", "arbitrary"),
        name="nsa_selected",
    )(za, kx, vs, sel)


def _win_kernel(q_ref, kw_ref, vw_ref, gate_ref, ocmp_ref, oslc_ref, out_ref):
    i = pl.program_id(1)
    tq, hd = NSA_TQ, NSA_HEAD_DIM
    span = tq + WIN
    qg = _group_queries(q_ref, 0)
    r0 = pl.multiple_of(i * tq, tq)
    t = i * tq + lax.broadcasted_iota(jnp.int32, (tq, 1), 0)
    pos = i * tq - WIN + lax.broadcasted_iota(jnp.int32, (1, span), 1)
    ok = (pos <= t) & (pos > t - WIN) & (pos >= 0)
    s = lax.dot_general(qg, kw_ref[0, pl.ds(r0, span), :], NT_DIMS, preferred_element_type=F32)
    s = s + jnp.concatenate([jnp.where(ok, 0.0, NEG_INF)] * NSA_R, axis=0)
    e = jnp.exp(s - jnp.max(s, axis=-1, keepdims=True))
    acc = jnp.dot(e.astype(MX), vw_ref[0, pl.ds(r0, span), :], preferred_element_type=F32)
    ow = acc[:, :hd] / acc[:, hd:]
    sig = _sigmoid(gate_ref[...])
    for r in range(NSA_R):
        cols = slice(r * hd, (r + 1) * hd)
        out_ref[:, cols] = (sig[:, r:r + 1] * ocmp_ref[:, cols]
                            + sig[:, NSA_R + r:NSA_R + r + 1] * oslc_ref[:, cols]
                            + sig[:, 2 * NSA_R + r:2 * NSA_R + r + 1] * ow[r * tq:(r + 1) * tq, :]
                            ).astype(MX)


def _win_combine(za, kw, vw, ocmp, oslc):
    S = za.shape[0]
    G = kw.shape[0]
    tq, hd = NSA_TQ, NSA_HEAD_DIM
    gw = NSA_R * hd
    blk = lambda: pl.BlockSpec((tq, gw), lambda g, i: (i, g))
    return pl.pallas_call(
        _win_kernel,
        grid=(G, S // tq),
        in_specs=[blk(),
                  pl.BlockSpec((1, S + WIN, hd), lambda g, i: (g, 0, 0)),
                  pl.BlockSpec((1, S + WIN, LANE), lambda g, i: (g, 0, 0)),
                  pl.BlockSpec((tq, GATE_PAD), lambda g, i: (i, ZA_G_OFF // GATE_PAD + g)),
                  blk(), blk()],
        out_specs=blk(),
        out_shape=jax.ShapeDtypeStruct((S, BRANCH_W), MX),
        compiler_params=_params("arbitrary", "arbitrary"),
        name="nsa_window_combine",
    )(za, kw, vw, za, ocmp, oslc)


def _nsa_mixer(za, w_cmp_k, w_cmp_v):
    S = za.shape[0]
    G, hd = NSA_KV_GROUPS, NSA_HEAD_DIM
    nc = S // CMP_STRIDE

    def kv(idx):
        off = ZA_KV_OFF + idx * G * hd
        return jnp.stack([za[:, off + g * hd:off + (g + 1) * hd] for g in range(G)])

    def blocks(x):
        half = x.reshape(G, nc, CMP_STRIDE * hd)
        nxt = jnp.concatenate([half[:, 1:], jnp.zeros_like(half[:, :1])], axis=1)
        return jnp.concatenate([half, nxt], axis=2).astype(MX)

    def with_ones(x):
        return jnp.concatenate([x, jnp.ones((G, S, LANE - hd), x.dtype)], axis=2).astype(MX)

    front = lambda x: jnp.pad(x, ((0, 0), (WIN, 0), (0, 0)))
    wk = w_cmp_k.reshape(CMP_LEN * hd, hd).astype(MX)
    wvt = jnp.pad(w_cmp_v.reshape(CMP_LEN * hd, hd).T, ((0, LANE - hd), (0, 0))).astype(MX)
    kc = _compress(blocks(kv(0)), wk, transposed=False)
    vct = _compress(blocks(kv(1)), wvt, transposed=True)
    ocmp, sel = _cmp_select(za, kc, vct)
    oslc = _slc_attention(za, kv(2).astype(MX), with_ones(kv(3)), sel)
    return _win_combine(za, front(kv(4).astype(MX)), front(with_ones(kv(5))), ocmp, oslc)


def _pack_w_in(w):
    seg = lambda off, n: w[:, off:off + n]
    R = NSA_HEADS // NSA_KV_GROUPS
    gates = []
    for g in range(NSA_KV_GROUPS):
        cols = [OFF_G + br * NSA_HEADS + g * R + r for br in range(3) for r in range(R)]
        gates.append(jnp.pad(w[:, jnp.array(cols)], ((0, 0), (0, GATE_PAD - len(cols)))))
    wa = jnp.concatenate([seg(OFF_Q, BRANCH_W), seg(OFF_S5, BRANCH_W), seg(OFF_POOL, BRANCH_W),
                          seg(OFF_RQ, BRANCH_W), seg(OFF_RK, BRANCH_W), seg(OFF_RV, BRANCH_W),
                          seg(OFF_RG, BRANCH_W), seg(OFF_KV, NSA_KV_COLS)] + gates, axis=1)
    return wa.astype(MX), w[:, OFF_MERGE:].astype(MX)


def kernel(x, p, g_mix, w_in, w_cmp_k, w_cmp_v, s5_a_re, s5_a_im, s5_log_dt, s5_b_re, s5_b_im,
           s5_c_re, s5_c_im, s5_d, s5_w_glu, pool_w, pool_scale, w_branch, w_out, g_mlp,
           w_mlp_up, w_mlp_down, w_ple_gate, w_ple_proj, g_final):
    B, S, D = x.shape
    assert B == 1 and D == D_MODEL
    depth = w_in.shape[0]
    h = x.reshape(S, D)
    for i in range(depth):
        wa, wm = _pack_w_in(w_in[i])
        za, u = _inproj(h, g_mix[i].reshape(1, D), wa, tm=1024, tn=768)

        o_nsa = _nsa_mixer(za, w_cmp_k[i], w_cmp_v[i])
        s5_tabs = _s5_tables(s5_a_re[i], s5_a_im[i], s5_log_dt[i], s5_b_re[i], s5_b_im[i],
                             s5_c_re[i], s5_c_im[i])
        o_s5 = _s5_mixer(za, s5_tabs, s5_d[i], s5_w_glu[i].astype(MX), tm=512)
        o_pool = _pool_mixer(za, pool_w[i].astype(MX), pool_scale[i], tm=1024)
        o_ret = _ret_mixer(za)

        merged = _merge(u, (o_nsa, o_s5, o_pool, o_ret), wm, w_branch[i].astype(MX),
                        tm=1024, tn=256)
        h = _resid_mm(h, merged, w_out[i].astype(MX), tm=1024, tn=1024)
        h = _mlp(h, g_mlp[i].reshape(1, D), w_mlp_up[i].astype(MX),
                 w_mlp_down[i].astype(MX), tm=1024, tf=512)
        h = _ple(h, p[i].reshape(S, PLE_DIM), w_ple_gate[i].astype(MX),
                 w_ple_proj[i].astype(MX), g_final.reshape(1, D), tm=512,
                 final_norm=(i == depth - 1))
    return h.reshape(B, S, D)
```

```python
import functools
import math

import jax
import jax.numpy as jnp
from jax import lax
from jax.experimental import pallas as pl
from jax.experimental.pallas import tpu as pltpu

F32 = jnp.float32
BF16 = jnp.bfloat16
MX = BF16

D_MODEL = 2048
N_BRANCH = 4
BRANCH_W = D_MODEL // N_BRANCH
NSA_HEADS = 8
NSA_KV_GROUPS = 2
NSA_HEAD_DIM = BRANCH_W // NSA_HEADS
CMP_LEN = 32
CMP_STRIDE = 16
SLC_BLOCK = 64
SLC_TOP_N = 16
WIN = 512
Q_BLOCK = 128
S5_GROUP = 16
S5_GROUPS = BRANCH_W // S5_GROUP
S5_STATE = 64
S5_COLS = S5_GROUPS * S5_STATE
POOL_WINDOWS = (2, 4, 8, 16)
POOL_GROUP = BRANCH_W // len(POOL_WINDOWS)
POOL_HALO = 16
RET_HEADS = 4
RET_HEAD_DIM = BRANCH_W // RET_HEADS
RET_CHUNK = 128
D_FF = 4 * D_MODEL
PLE_DIM = 256
NORM_EPS = 1e-6
NEG_INF = -1e30
FORCE_BONUS = 1e4

NSA_KV_COLS = 6 * NSA_KV_GROUPS * NSA_HEAD_DIM
NSA_GATE_COLS = 3 * NSA_HEADS
OFF_Q = 0
OFF_KV = OFF_Q + BRANCH_W
OFF_G = OFF_KV + NSA_KV_COLS
OFF_S5 = OFF_G + NSA_GATE_COLS
OFF_POOL = OFF_S5 + BRANCH_W
OFF_RQ = OFF_POOL + BRANCH_W
OFF_RK = OFF_RQ + BRANCH_W
OFF_RV = OFF_RK + BRANCH_W
OFF_RG = OFF_RV + BRANCH_W
OFF_MERGE = OFF_RG + BRANCH_W

LANE = 128
SUBLANE = 8
GATE_PAD = LANE
ZA_Q, ZA_S5, ZA_POOL, ZA_RQ, ZA_RK, ZA_RV, ZA_RG = range(7)
ZA_KV_OFF = 7 * BRANCH_W
ZA_G_OFF = ZA_KV_OFF + NSA_KV_COLS
ZA_COLS = ZA_G_OFF + NSA_KV_GROUPS * GATE_PAD

VMEM_LIMIT = 56 * 1024 * 1024


def _params(*sem):
    return pltpu.CompilerParams(dimension_semantics=sem, vmem_limit_bytes=VMEM_LIMIT)


def _rms(x, g):
    return x * lax.rsqrt(jnp.mean(x * x, axis=-1, keepdims=True) + NORM_EPS) * g


def _sigmoid(x):
    return 1.0 / (1.0 + jnp.exp(-x))


def _inproj_kernel(h_ref, g_ref, w_ref, z_ref, u_ref):
    @pl.when(pl.program_id(1) == 0)
    def _():
        u_ref[...] = _rms(h_ref[...], g_ref[...]).astype(MX)

    z_ref[...] = jnp.dot(u_ref[...], w_ref[...], preferred_element_type=F32)


def _inproj(h, g, w, *, tm, tn):
    S, D = h.shape
    N = w.shape[1]
    return pl.pallas_call(
        _inproj_kernel,
        grid=(S // tm, N // tn),
        in_specs=[pl.BlockSpec((tm, D), lambda i, j: (i, 0)),
                  pl.BlockSpec((1, D), lambda i, j: (0, 0)),
                  pl.BlockSpec((D, tn), lambda i, j: (0, j))],
        out_specs=[pl.BlockSpec((tm, tn), lambda i, j: (i, j)),
                   pl.BlockSpec((tm, D), lambda i, j: (i, 0))],
        out_shape=[jax.ShapeDtypeStruct((S, N), F32), jax.ShapeDtypeStruct((S, D), MX)],
        compiler_params=_params("parallel", "arbitrary"),
        name="inproj",
    )(h, g, w)


def _merge_kernel(u_ref, o0_ref, o1_ref, o2_ref, o3_ref, wm0_ref, wm1_ref, wm2_ref, wm3_ref,
                  wb_ref, out_ref):
    u = u_ref[...]
    acc = None
    for j, (o_ref, wm_ref) in enumerate(((o0_ref, wm0_ref), (o1_ref, wm1_ref),
                                         (o2_ref, wm2_ref), (o3_ref, wm3_ref))):
        gate = _sigmoid(jnp.dot(u, wm_ref[...], preferred_element_type=F32))
        br = jnp.dot(o_ref[...], wb_ref[j], preferred_element_type=F32)
        acc = gate * br if acc is None else acc + gate * br
    out_ref[...] = acc.astype(MX)


def _merge(u, outs, wm, wb, *, tm, tn):
    S, D = u.shape
    nt = D // tn
    o_specs = [pl.BlockSpec((tm, BRANCH_W), lambda i, c: (i, 0)) for _ in range(N_BRANCH)]
    wm_specs = [pl.BlockSpec((D, tn), functools.partial(lambda i, c, j: (0, j * nt + c), j=j))
                for j in range(N_BRANCH)]
    return pl.pallas_call(
        _merge_kernel,
        grid=(S // tm, nt),
        in_specs=[pl.BlockSpec((tm, D), lambda i, c: (i, 0))] + o_specs + wm_specs
                 + [pl.BlockSpec((N_BRANCH, BRANCH_W, tn), lambda i, c: (0, 0, c))],
        out_specs=pl.BlockSpec((tm, tn), lambda i, c: (i, c)),
        out_shape=jax.ShapeDtypeStruct((S, D), MX),
        compiler_params=_params("parallel", "arbitrary"),
        name="merge",
    )(u, *outs, wm, wm, wm, wm, wb)


def _resid_mm_kernel(h_ref, x_ref, w_ref, out_ref):
    out_ref[...] = h_ref[...] + jnp.dot(x_ref[...], w_ref[...], preferred_element_type=F32)


def _resid_mm(h, x, w, *, tm, tn):
    S, D = h.shape
    K = x.shape[1]
    return pl.pallas_call(
        _resid_mm_kernel,
        grid=(S // tm, D // tn),
        in_specs=[pl.BlockSpec((tm, tn), lambda i, c: (i, c)),
                  pl.BlockSpec((tm, K), lambda i, c: (i, 0)),
                  pl.BlockSpec((K, tn), lambda i, c: (0, c))],
        out_specs=pl.BlockSpec((tm, tn), lambda i, c: (i, c)),
        out_shape=jax.ShapeDtypeStruct((S, D), F32),
        compiler_params=_params("parallel", "arbitrary"),
        name="resid_mm",
    )(h, x, w)


def _mlp_kernel(h_ref, g_ref, wu_ref, wd_ref, out_ref, v_ref):
    @pl.when(pl.program_id(1) == 0)
    def _():
        h = h_ref[...]
        v_ref[...] = _rms(h, g_ref[...]).astype(MX)
        out_ref[...] = h

    hid = jnp.maximum(jnp.dot(v_ref[...], wu_ref[...], preferred_element_type=F32), 0.0)
    out_ref[...] += jnp.dot((hid * hid).astype(MX), wd_ref[...], preferred_element_type=F32)


def _mlp(h, g, wu, wd, *, tm, tf):
    S, D = h.shape
    return pl.pallas_call(
        _mlp_kernel,
        grid=(S // tm, wu.shape[1] // tf),
        in_specs=[pl.BlockSpec((tm, D), lambda i, f: (i, 0)),
                  pl.BlockSpec((1, D), lambda i, f: (0, 0)),
                  pl.BlockSpec((D, tf), lambda i, f: (0, f)),
                  pl.BlockSpec((tf, D), lambda i, f: (f, 0))],
        out_specs=pl.BlockSpec((tm, D), lambda i, f: (i, 0)),
        out_shape=jax.ShapeDtypeStruct((S, D), F32),
        scratch_shapes=[pltpu.VMEM((tm, D), MX)],
        compiler_params=_params("parallel", "arbitrary"),
        name="mlp",
    )(h, g, wu, wd)


def _ple_kernel(h_ref, p_ref, wg_ref, wp_ref, gf_ref, out_ref, *, final_norm):
    h = h_ref[...]
    gate = _sigmoid(jnp.dot(h.astype(MX), wg_ref[...], preferred_element_type=F32))
    emb = jnp.dot(p_ref[...].astype(MX), wp_ref[...], preferred_element_type=F32)
    y = h + gate * emb
    out_ref[...] = _rms(y, gf_ref[...]) if final_norm else y


def _ple(h, p, wg, wp, gf, *, tm, final_norm):
    S, D = h.shape
    P = p.shape[1]
    return pl.pallas_call(
        functools.partial(_ple_kernel, final_norm=final_norm),
        grid=(S // tm,),
        in_specs=[pl.BlockSpec((tm, D), lambda i: (i, 0)),
                  pl.BlockSpec((tm, P), lambda i: (i, 0)),
                  pl.BlockSpec((D, D), lambda i: (0, 0)),
                  pl.BlockSpec((P, D), lambda i: (0, 0)),
                  pl.BlockSpec((1, D), lambda i: (0, 0))],
        out_specs=pl.BlockSpec((tm, D), lambda i: (i, 0)),
        out_shape=jax.ShapeDtypeStruct((S, D), F32),
        compiler_params=_params("parallel"),
        name="ple",
    )(h, p, wg, wp, gf)


S5_PANELS = BRANCH_W // LANE
S5_PANEL_COLS = S5_COLS // S5_PANELS
T_A0, T_A1, T_A2, T_A4 = range(4)


def _cmul_add(xr, xi, ar, ai, br, bi):
    return xr + (ar * br - ai * bi), xi + (ar * bi + ai * br)


def _s5_kernel(u_ref, bre_ref, bim_ref, tab_ref, cre_ref, cim_ref, d_ref, wglu_ref, out_ref,
               hre_ref, him_ref, carry_ref):
    tm = u_ref.shape[0]

    @pl.when(pl.program_id(0) == 0)
    def _():
        carry_ref[...] = jnp.zeros_like(carry_ref)

    u = u_ref[...]
    ub = u.astype(MX)
    for k in range(S5_PANELS):
        uk = ub[:, k * LANE:(k + 1) * LANE]
        cols = slice(k * S5_PANEL_COLS, (k + 1) * S5_PANEL_COLS)
        hre_ref[:, cols] = jnp.dot(uk, bre_ref[k], preferred_element_type=F32)
        him_ref[:, cols] = jnp.dot(uk, bim_ref[k], preferred_element_type=F32)

    def body(r, carry):
        rows = pl.ds(pl.multiple_of(r * SUBLANE, SUBLANE), SUBLANE)
        for cb in range(S5_COLS // LANE):
            cols = slice(cb * LANE, (cb + 1) * LANE)
            xr, xi = _cmul_add(hre_ref[rows, cols], him_ref[rows, cols],
                               tab_ref[T_A0, 0, :, cols], tab_ref[T_A0, 1, :, cols],
                               carry_ref[0, :, cols], carry_ref[1, :, cols])
            for k, sh in ((T_A1, 1), (T_A2, 2), (T_A4, 4)):
                xr, xi = _cmul_add(xr, xi, tab_ref[k, 0, :, cols], tab_ref[k, 1, :, cols],
                                   pltpu.roll(xr, sh, 0), pltpu.roll(xi, sh, 0))
            hre_ref[rows, cols] = xr
            him_ref[rows, cols] = xi
            carry_ref[0, :, cols] = jnp.broadcast_to(xr[SUBLANE - 1:SUBLANE, :], xr.shape)
            carry_ref[1, :, cols] = jnp.broadcast_to(xi[SUBLANE - 1:SUBLANE, :], xi.shape)
        return carry

    lax.fori_loop(0, tm // SUBLANE, body, 0)

    ys = []
    for k in range(S5_PANELS):
        cols = slice(k * S5_PANEL_COLS, (k + 1) * S5_PANEL_COLS)
        ys.append(jnp.dot(hre_ref[:, cols].astype(MX), cre_ref[k], preferred_element_type=F32)
                  + jnp.dot(him_ref[:, cols].astype(MX), cim_ref[k], preferred_element_type=F32))
    y = jnp.concatenate(ys, axis=1) + d_ref[...] * u
    y = 0.5 * y * (1.0 + jnp.tanh(math.sqrt(2.0 / math.pi) * (y + 0.044715 * (y * y * y))))
    gl = jnp.dot(y.astype(MX), wglu_ref[...], preferred_element_type=F32)
    out_ref[...] = (gl[:, :BRANCH_W] * _sigmoid(gl[:, BRANCH_W:])).astype(MX)


def _s5_tables(a_re, a_im, log_dt, b_re, b_im, c_re, c_im):
    dt = jnp.exp(log_dt)[:, None]
    xr, xi = a_re * dt, a_im * dt

    pr = jnp.exp(xr) * jnp.cos(xi)
    pi = jnp.exp(xr) * jnp.sin(xi)
    den = a_re * a_re + a_im * a_im
    qr = ((pr - 1.0) * a_re + pi * a_im) / den
    qi = (pi * a_re - (pr - 1.0) * a_im) / den
    bbar_re = qr[..., None] * b_re - qi[..., None] * b_im
    bbar_im = qr[..., None] * b_im + qi[..., None] * b_re

    flat = lambda z: z.reshape(1, S5_COLS)
    powers = [(flat(pr), flat(pi))]
    for _ in range(3):
        zr, zi = powers[-1]
        powers.append((zr * flat(pr) - zi * flat(pi), zr * flat(pi) + zi * flat(pr)))
    row = jnp.arange(SUBLANE)[:, None]
    planes = [(jnp.where(row == 0, flat(pr), 0.0), jnp.where(row == 0, flat(pi), 0.0))]
    for sh in (1, 2, 4):
        zr, zi = powers[sh - 1]
        planes.append((jnp.where(row >= sh, zr, 0.0), jnp.where(row >= sh, zi, 0.0)))
    tab = jnp.stack([jnp.stack(p) for p in planes]).astype(F32)

    pg = S5_GROUPS // S5_PANELS
    eye = jnp.eye(pg, dtype=F32)

    def bdiag_in(m):
        m = m.reshape(S5_PANELS, pg, S5_STATE, S5_GROUP)
        return jnp.einsum('pgnc,gh->pgchn', m, eye).reshape(S5_PANELS, LANE, S5_PANEL_COLS)

    def bdiag_out(m):
        m = m.reshape(S5_PANELS, pg, S5_GROUP, S5_STATE)
        return jnp.einsum('pgcn,gh->pgnhc', m, eye).reshape(S5_PANELS, S5_PANEL_COLS, LANE)

    return (tab, bdiag_in(bbar_re).astype(MX), bdiag_in(bbar_im).astype(MX),
            bdiag_out(c_re).astype(MX), bdiag_out(-c_im).astype(MX))


def _s5_mixer(za, tabs, d_skip, w_glu, *, tm):
    S = za.shape[0]
    tab, bre, bim, cre, cim = tabs
    full = lambda shape: pl.BlockSpec(shape, lambda i: (0,) * len(shape))
    return pl.pallas_call(
        _s5_kernel,
        grid=(S // tm,),
        in_specs=[pl.BlockSpec((tm, BRANCH_W), lambda i: (i, ZA_S5)),
                  full(bre.shape), full(bim.shape), full(tab.shape), full(cre.shape),
                  full(cim.shape), full((1, BRANCH_W)), full(w_glu.shape)],
        out_specs=pl.BlockSpec((tm, BRANCH_W), lambda i: (i, 0)),
        out_shape=jax.ShapeDtypeStruct((S, BRANCH_W), MX),
        scratch_shapes=[pltpu.VMEM((tm, S5_COLS), F32), pltpu.VMEM((tm, S5_COLS), F32),
                        pltpu.VMEM((2, SUBLANE, S5_COLS), F32)],
        compiler_params=_params("arbitrary"),
        name="s5",
    )(za, bre, bim, tab, cre, cim, d_skip.reshape(1, BRANCH_W), w_glu)


def _pool_kernel(x_ref, w_ref, scale_ref, out_ref, ext_ref):
    tm = x_ref.shape[0]
    i = pl.program_id(0)

    @pl.when(i == 0)
    def _():
        ext_ref[0:POOL_HALO, :] = jnp.zeros((POOL_HALO, BRANCH_W), F32)

    x = x_ref[...]
    ext_ref[POOL_HALO:, :] = x
    t = i * tm + lax.broadcasted_iota(jnp.int32, (tm, 1), 0)
    for gi, w in enumerate(POOL_WINDOWS):
        cols = slice(gi * POOL_GROUP, (gi + 1) * POOL_GROUP)
        acc = x[:, cols]
        for d in range(1, w):
            acc = acc + ext_ref[POOL_HALO - d:POOL_HALO - d + tm, cols]
        count = jnp.minimum(t + 1, w).astype(F32)
        diff = (acc / count - x[:, cols]).astype(MX)
        y = jnp.dot(diff, w_ref[gi], preferred_element_type=F32)
        out_ref[:, cols] = (y * scale_ref[:, cols]).astype(MX)
    ext_ref[0:POOL_HALO, :] = x[tm - POOL_HALO:, :]


def _pool_mixer(za, w_pool, pool_scale, *, tm):
    S = za.shape[0]
    return pl.pallas_call(
        _pool_kernel,
        grid=(S // tm,),
        in_specs=[pl.BlockSpec((tm, BRANCH_W), lambda i: (i, ZA_POOL)),
                  pl.BlockSpec(w_pool.shape, lambda i: (0, 0, 0)),
                  pl.BlockSpec((1, BRANCH_W), lambda i: (0, 0))],
        out_specs=pl.BlockSpec((tm, BRANCH_W), lambda i: (i, 0)),
        out_shape=jax.ShapeDtypeStruct((S, BRANCH_W), MX),
        scratch_shapes=[pltpu.VMEM((tm + POOL_HALO, BRANCH_W), F32)],
        compiler_params=_params("arbitrary"),
        name="pool",
    )(za, w_pool, pool_scale.reshape(1, BRANCH_W))


def _ret_consts():
    H, dh, C = RET_HEADS, RET_HEAD_DIM, RET_CHUNK
    angle = jnp.repeat(1.0 / (10000.0 ** jnp.linspace(0.0, 1.0, dh // 2)), 2).reshape(1, dh)
    log_g = jnp.log(1.0 - 2.0 ** (-5.0 - jnp.arange(H, dtype=F32)))
    idx = jnp.arange(C, dtype=F32)
    rel = idx[:, None] - idx[None, :]
    decay = jnp.where(rel >= 0, jnp.exp(jnp.maximum(rel, 0.0)[None] * log_g[:, None, None]), 0.0)
    xi = jnp.exp((idx + 1.0)[None, :] * log_g[:, None])[..., None]
    zeta = jnp.exp((C - 1.0 - idx)[None, :] * log_g[:, None])[..., None]
    g_chunk = jnp.broadcast_to(jnp.exp(C * log_g)[:, None, None], (H, 1, dh))
    return angle.astype(F32), decay, xi, zeta, g_chunk


def _rotate_pairs(x, even):
    n = x.shape[-1]
    return jnp.where(even, -pltpu.roll(x, n - 1, 1), pltpu.roll(x, 1, 1))


def _ret_kernel(q_ref, k_ref, v_ref, g_ref, ang_ref, decay_ref, xi_ref, zeta_ref, gch_ref,
                out_ref, state_ref):
    C, dh = RET_CHUNK, RET_HEAD_DIM
    i = pl.program_id(0)

    @pl.when(i == 0)
    def _():
        state_ref[...] = jnp.zeros_like(state_ref)

    pos = (i * C + lax.broadcasted_iota(jnp.int32, (C, 1), 0)).astype(F32)
    ang = pos * ang_ref[...]
    sin, cos = jnp.sin(ang), jnp.cos(ang)
    even = (lax.broadcasted_iota(jnp.int32, (C, dh), 1) % 2) == 0
    for hd in range(RET_HEADS):
        cols = slice(hd * dh, (hd + 1) * dh)
        q = q_ref[:, cols]
        k = k_ref[:, cols]
        vb = v_ref[:, cols].astype(MX)
        q = q * cos + _rotate_pairs(q, even) * sin
        k = (k * cos + _rotate_pairs(k, even) * sin) * dh ** -0.5
        qb = q.astype(MX)
        s = lax.dot_general(qb, k.astype(MX), (((1,), (1,)), ((), ())),
                            preferred_element_type=F32) * decay_ref[hd]
        inner = jnp.dot(s.astype(MX), vb, preferred_element_type=F32)
        state = state_ref[hd]
        cross = jnp.dot(qb, state.astype(MX), preferred_element_type=F32) * xi_ref[hd]
        kz = (k * zeta_ref[hd]).astype(MX)
        state_ref[hd] = state * gch_ref[hd] + lax.dot_general(
            kz, vb, (((0,), (0,)), ((), ())), preferred_element_type=F32)
        y = inner + cross
        y = y * lax.rsqrt(jnp.mean(y * y, axis=-1, keepdims=True) + NORM_EPS)
        g = g_ref[:, cols]
        out_ref[:, cols] = (g * _sigmoid(g) * y).astype(MX)


def _ret_mixer(za):
    S = za.shape[0]
    C = RET_CHUNK
    consts = _ret_consts()
    col = lambda c: pl.BlockSpec((C, BRANCH_W), lambda i: (i, c))
    full = lambda a: pl.BlockSpec(a.shape, lambda i: (0,) * a.ndim)
    return pl.pallas_call(
        _ret_kernel,
        grid=(S // C,),
        in_specs=[col(ZA_RQ), col(ZA_RK), col(ZA_RV), col(ZA_RG)] + [full(a) for a in consts],
        out_specs=pl.BlockSpec((C, BRANCH_W), lambda i: (i, 0)),
        out_shape=jax.ShapeDtypeStruct((S, BRANCH_W), MX),
        scratch_shapes=[pltpu.VMEM((RET_HEADS, RET_HEAD_DIM, RET_HEAD_DIM), F32)],
        compiler_params=_params("arbitrary"),
        name="retention",
    )(za, za, za, za, *consts)


NSA_R = NSA_HEADS // NSA_KV_GROUPS
NSA_TQ = 128
SLC_TQ = 512
WIN_TQ = 256
NSA_TK = 512
CMP_CH = 256
P_OFF = SUBLANE
NT_DIMS = (((1,), (1,)), ((), ()))


def _compress_kernel(x_ref, w_ref, out_ref):
    out_ref[0] = jnp.dot(x_ref[0], w_ref[...], preferred_element_type=F32).astype(out_ref.dtype)


def _compress_t_kernel(x_ref, wt_ref, out_ref):
    out_ref[0] = lax.dot_general(wt_ref[...], x_ref[0], NT_DIMS,
                                 preferred_element_type=F32).astype(out_ref.dtype)


def _compress(x, w, *, transposed):
    G, nc, kk = x.shape
    if transposed:
        kern, out_block = _compress_t_kernel, (1, w.shape[0], nc)
    else:
        kern, out_block = _compress_kernel, (1, nc, w.shape[1])
    return pl.pallas_call(
        kern,
        grid=(G,),
        in_specs=[pl.BlockSpec((1, nc, kk), lambda g: (g, 0, 0)),
                  pl.BlockSpec(w.shape, lambda g: (0, 0))],
        out_specs=pl.BlockSpec(out_block, lambda g: (g, 0, 0)),
        out_shape=jax.ShapeDtypeStruct((G,) + out_block[1:], x.dtype),
        compiler_params=_params("arbitrary"),
        name="nsa_compress",
    )(x, w)


def _group_queries(q_ref, g, extra_scale=1.0):
    hd = NSA_HEAD_DIM
    base = g * NSA_R * hd
    qs = [q_ref[:, base + r * hd:base + (r + 1) * hd] for r in range(NSA_R)]
    return (jnp.concatenate(qs, axis=0) * (hd ** -0.5 * extra_scale)).astype(MX)


def _cmp_select_kernel(q_ref, kc_ref, vct_ref, ocmp_ref, sel_ref, s_sc, p_sc, ot_sc):
    i = pl.program_id(0)
    tq, hd = NSA_TQ, NSA_HEAD_DIM
    nc = kc_ref.shape[1]
    ns = sel_ref.shape[2]
    ch = min(CMP_CH, nc)
    nch = jnp.minimum((tq // CMP_STRIDE * (i + 1) + ch - 1) // ch, nc // ch)
    t4 = i * tq + lax.broadcasted_iota(jnp.int32, (1, NSA_R * tq), 1) % tq
    t1 = i * tq + lax.broadcasted_iota(jnp.int32, (1, tq), 1)
    nrow = lax.broadcasted_iota(jnp.int32, (ch, 1), 0)
    jrow = lax.broadcasted_iota(jnp.int32, (ns, 1), 0)
    jfull = jnp.broadcast_to(jrow.astype(F32), (ns, tq))

    def rows(c):
        return pl.ds(pl.multiple_of(c * ch, ch), ch)

    def visible(c):
        return ((c * ch + nrow) * CMP_STRIDE + (CMP_LEN - 1)) <= t4

    groups = range(NSA_KV_GROUPS)
    qgs = [_group_queries(q_ref, g) for g in groups]
    lanes = (1, NSA_R * tq)

    def scores(c, ms):
        vis = visible(c)
        out = []
        for g in groups:
            s = lax.dot_general(kc_ref[g, rows(c), :], qgs[g], NT_DIMS, preferred_element_type=F32)
            s = jnp.where(vis, s, NEG_INF)
            s_sc[g, rows(c), :] = s
            out.append(jnp.maximum(ms[g], jnp.max(s, axis=0, keepdims=True)))
        return tuple(out)

    ms = lax.fori_loop(0, nch, scores, tuple(jnp.full(lanes, NEG_INF, F32) for _ in groups))

    def expsum(c, ls):
        vis = visible(c)
        out = []
        for g in groups:
            e = jnp.where(vis, jnp.exp(s_sc[g, rows(c), :] - ms[g]), 0.0)
            s_sc[g, rows(c), :] = e
            out.append(ls[g] + jnp.sum(e, axis=0, keepdims=True))
        return tuple(out)

    ls = lax.fori_loop(0, nch, expsum, tuple(jnp.zeros(lanes, F32) for _ in groups))
    invs = [1.0 / jnp.maximum(l, 1e-30) for l in ls]
    p_sc[...] = jnp.zeros_like(p_sc)
    ot_sc[...] = jnp.zeros_like(ot_sc)

    def weigh(c, carry):
        for g in groups:
            p = s_sc[g, rows(c), :] * invs[g]
            p_sc[g, pl.ds(pl.multiple_of(P_OFF + c * ch, SUBLANE), ch), :] = (
                p[:, 0:tq] + p[:, tq:2 * tq] + p[:, 2 * tq:3 * tq] + p[:, 3 * tq:4 * tq])
            ot_sc[g] += jnp.dot(vct_ref[g, :, rows(c)], p.astype(MX), preferred_element_type=F32)
        return carry

    lax.fori_loop(0, nch, weigh, 0)
    for h in range(NSA_HEADS):
        g, r = divmod(h, NSA_R)
        o = ot_sc[g, :, r * tq:(r + 1) * tq].T
        ocmp_ref[:, h * hd:(h + 1) * hd] = o[:, :hd]

    ratio = SLC_BLOCK // CMP_STRIDE
    valid = jrow * SLC_BLOCK <= t1
    bonus = jnp.where((jrow == 0) | (jrow == t1 // SLC_BLOCK), FORCE_BONUS, 0.0)
    xs = []
    for g in groups:
        imp = p_sc[g, pl.ds(P_OFF - 1, ns, stride=ratio), :]
        for k in range(ratio):
            imp = imp + p_sc[g, pl.ds(P_OFF + k, ns, stride=ratio), :]
        xs.append(jnp.where(valid, imp + bonus, NEG_INF))
    for _ in range(min(SLC_TOP_N, ns)):
        for g in groups:
            mx = jnp.max(xs[g], axis=0, keepdims=True)
            cand = jnp.where(xs[g] == mx, jfull, float(ns))
            xs[g] = jnp.where(cand == jnp.min(cand, axis=0, keepdims=True), -jnp.inf, xs[g])
    for g in groups:
        sel_ref[g] = jnp.where(xs[g] == -jnp.inf, 1.0, 0.0).T.astype(BF16)


def _cmp_select(za, kc, vct):
    S = za.shape[0]
    G, nc, hd = kc.shape
    ns = S // SLC_BLOCK
    tq = NSA_TQ
    return pl.pallas_call(
        _cmp_select_kernel,
        grid=(S // tq,),
        in_specs=[pl.BlockSpec((tq, BRANCH_W), lambda i: (i, ZA_Q)),
                  pl.BlockSpec(kc.shape, lambda i: (0, 0, 0)),
                  pl.BlockSpec(vct.shape, lambda i: (0, 0, 0))],
        out_specs=[pl.BlockSpec((tq, BRANCH_W), lambda i: (i, 0)),
                   pl.BlockSpec((G, tq, ns), lambda i: (0, i, 0))],
        out_shape=[jax.ShapeDtypeStruct((S, BRANCH_W), F32),
                   jax.ShapeDtypeStruct((G, S, ns), BF16)],
        scratch_shapes=[pltpu.VMEM((G, nc, NSA_R * tq), F32), pltpu.VMEM((G, P_OFF + nc, tq), F32),
                        pltpu.VMEM((G, LANE, NSA_R * tq), F32)],
        compiler_params=_params("parallel"),
        name="nsa_cmp_select",
    )(za, kc, vct)


SLC_MASK = 2.0 ** 100
LOG2E = math.log2(math.e)


def _slc_kernel(q_ref, kx_ref, vs_ref, sel_ref, out_ref, lhs_sc, m_sc, acc_sc, *, tk):
    i = pl.program_id(1)
    tq, hd = SLC_TQ, NSA_HEAD_DIM
    ksel = kx_ref.shape[2] - LANE
    bpt = tk // SLC_BLOCK
    qg = _group_queries(q_ref, 0, LOG2E)
    pad = jnp.zeros((NSA_R * tq, LANE - hd), MX)
    for h in range(sel_ref.shape[2] // ksel):
        unsel = (1.0 - sel_ref[0, :, h * ksel:(h + 1) * ksel].astype(F32)).astype(MX)
        lhs_sc[h] = jnp.concatenate([qg, pad, jnp.concatenate([unsel] * NSA_R, axis=0)], axis=1)
    t = i * tq + lax.broadcasted_iota(jnp.int32, (tq, 1), 0)
    lane = lax.broadcasted_iota(jnp.int32, (1, tk), 1)
    m_sc[...] = jnp.full_like(m_sc, NEG_INF)
    acc_sc[...] = jnp.zeros_like(acc_sc)

    def tile(kt, diagonal):
        k0 = pl.multiple_of(kt * tk, tk)
        s = lax.dot_general(lhs_sc[(kt * bpt) // ksel], kx_ref[0, pl.ds(k0, tk), :], NT_DIMS,
                            preferred_element_type=F32)
        if diagonal:
            hidden = jnp.where(k0 + lane <= t, 0.0, NEG_INF)
            s = s + jnp.concatenate([hidden] * NSA_R, axis=0)
        m_prev = m_sc[...]
        m_new = jnp.maximum(m_prev, jnp.max(s, axis=-1, keepdims=True))
        p = jnp.exp2(s - jnp.concatenate([m_new] * (tk // LANE), axis=1))
        acc_sc[...] = jnp.exp2(m_prev - m_new) * acc_sc[...] + jnp.dot(
            p.astype(MX), vs_ref[0, pl.ds(k0, tk), :], preferred_element_type=F32)
        m_sc[...] = m_new

    last = (i * tq) // tk

    def quad(j, carry):
        for u in range(4):
            tile(4 * j + u, False)
        return carry

    lax.fori_loop(0, last // 4, quad, 0)
    done = (last // 4) * 4

    @pl.when(last - done >= 2)
    def _():
        tile(done, False)
        tile(done + 1, False)

    @pl.when((last - done) % 2 == 1)
    def _():
        tile(last - 1, False)

    tile(last, True)
    acc = acc_sc[...]
    o = acc[:, :hd] / acc[:, hd:]
    for r in range(NSA_R):
        out_ref[:, r * hd:(r + 1) * hd] = o[r * tq:(r + 1) * tq, :]


def _slc_attention(za, ks, vs, sel):
    S = za.shape[0]
    G = ks.shape[0]
    ns = sel.shape[2]
    tq, hd = SLC_TQ, NSA_HEAD_DIM
    tk = min(NSA_TK, S)
    ksel = min(LANE, ns)
    own_block = (jnp.arange(S)[:, None] // SLC_BLOCK) % ksel == jnp.arange(ksel)[None, :]
    kx = jnp.concatenate([ks, jnp.zeros((G, S, LANE - hd), ks.dtype),
                          jnp.broadcast_to(jnp.where(own_block, -SLC_MASK, 0.0).astype(ks.dtype),
                                           (G, S, ksel))], axis=2)
    gw = NSA_R * hd
    return pl.pallas_call(
        functools.partial(_slc_kernel, tk=tk),
        grid=(G, S // tq),
        in_specs=[pl.BlockSpec((tq, gw), lambda g, i: (i, g)),
                  pl.BlockSpec((1, S, LANE + ksel), lambda g, i: (g, 0, 0)),
                  pl.BlockSpec((1, S, LANE), lambda g, i: (g, 0, 0)),
                  pl.BlockSpec((1, tq, ns), lambda g, i: (g, i, 0))],
        out_specs=pl.BlockSpec((tq, gw), lambda g, i: (i, g)),
        out_shape=jax.ShapeDtypeStruct((S, BRANCH_W), F32),
        scratch_shapes=[pltpu.VMEM((ns // ksel, NSA_R * tq, LANE + ksel), MX),
                        pltpu.VMEM((NSA_R * tq, LANE), F32), pltpu.VMEM((NSA_R * tq, LANE), F32)],
        compiler_params=_params("arbitrary", "arbitrary"),
        name="nsa_selected",
    )(za, kx, vs, sel)


def _win_kernel(q_ref, kw_ref, vw_ref, gate_ref, ocmp_ref, oslc_ref, out_ref):
    i = pl.program_id(1)
    tq, hd = WIN_TQ, NSA_HEAD_DIM
    span = tq + WIN
    qg = _group_queries(q_ref, 0, LOG2E)
    r0 = pl.multiple_of(i * tq, tq)
    t = i * tq + lax.broadcasted_iota(jnp.int32, (tq, 1), 0)
    pos = i * tq - WIN + lax.broadcasted_iota(jnp.int32, (1, span), 1)
    ok = (pos <= t) & (pos > t - WIN) & (pos >= 0)
    s = lax.dot_general(qg, kw_ref[0, pl.ds(r0, span), :], NT_DIMS, preferred_element_type=F32)
    s = s + jnp.concatenate([jnp.where(ok, 0.0, NEG_INF)] * NSA_R, axis=0)
    e = jnp.exp2(s - jnp.max(s, axis=-1, keepdims=True))
    acc = jnp.dot(e.astype(MX), vw_ref[0, pl.ds(r0, span), :], preferred_element_type=F32)
    ow = acc[:, :hd] / acc[:, hd:]
    sig = _sigmoid(gate_ref[...])
    for r in range(NSA_R):
        cols = slice(r * hd, (r + 1) * hd)
        out_ref[:, cols] = (sig[:, r:r + 1] * ocmp_ref[:, cols]
                            + sig[:, NSA_R + r:NSA_R + r + 1] * oslc_ref[:, cols]
                            + sig[:, 2 * NSA_R + r:2 * NSA_R + r + 1] * ow[r * tq:(r + 1) * tq, :]
                            ).astype(MX)


def _win_combine(za, kw, vw, ocmp, oslc):
    S = za.shape[0]
    G = kw.shape[0]
    tq, hd = WIN_TQ, NSA_HEAD_DIM
    gw = NSA_R * hd
    blk = lambda: pl.BlockSpec((tq, gw), lambda g, i: (i, g))
    return pl.pallas_call(
        _win_kernel,
        grid=(G, S // tq),
        in_specs=[blk(),
                  pl.BlockSpec((1, S + WIN, hd), lambda g, i: (g, 0, 0)),
                  pl.BlockSpec((1, S + WIN, LANE), lambda g, i: (g, 0, 0)),
                  pl.BlockSpec((tq, GATE_PAD), lambda g, i: (i, ZA_G_OFF // GATE_PAD + g)),
                  blk(), blk()],
        out_specs=blk(),
        out_shape=jax.ShapeDtypeStruct((S, BRANCH_W), MX),
        compiler_params=_params("arbitrary", "arbitrary"),
        name="nsa_window_combine",
    )(za, kw, vw, za, ocmp, oslc)


def _nsa_mixer(za, w_cmp_k, w_cmp_v):
    S = za.shape[0]
    G, hd = NSA_KV_GROUPS, NSA_HEAD_DIM
    nc = S // CMP_STRIDE

    def kv(idx):
        off = ZA_KV_OFF + idx * G * hd
        return jnp.stack([za[:, off + g * hd:off + (g + 1) * hd] for g in range(G)])

    def blocks(x):
        half = x.reshape(G, nc, CMP_STRIDE * hd)
        nxt = jnp.concatenate([half[:, 1:], jnp.zeros_like(half[:, :1])], axis=1)
        return jnp.concatenate([half, nxt], axis=2).astype(MX)

    def with_ones(x):
        return jnp.concatenate([x, jnp.ones((G, S, LANE - hd), x.dtype)], axis=2).astype(MX)

    front = lambda x: jnp.pad(x, ((0, 0), (WIN, 0), (0, 0)))
    wk = w_cmp_k.reshape(CMP_LEN * hd, hd).astype(MX)
    wvt = jnp.pad(w_cmp_v.reshape(CMP_LEN * hd, hd).T, ((0, LANE - hd), (0, 0))).astype(MX)
    kc = _compress(blocks(kv(0)), wk, transposed=False)
    vct = _compress(blocks(kv(1)), wvt, transposed=True)
    ocmp, sel = _cmp_select(za, kc, vct)
    oslc = _slc_attention(za, kv(2).astype(MX), with_ones(kv(3)), sel)
    return _win_combine(za, front(kv(4).astype(MX)), front(with_ones(kv(5))), ocmp, oslc)


def _pack_w_in(w):
    seg = lambda off, n: w[:, off:off + n]
    R = NSA_HEADS // NSA_KV_GROUPS
    gates = []
    for g in range(NSA_KV_GROUPS):
        cols = [OFF_G + br * NSA_HEADS + g * R + r for br in range(3) for r in range(R)]
        gates.append(jnp.pad(w[:, jnp.array(cols)], ((0, 0), (0, GATE_PAD - len(cols)))))
    wa = jnp.concatenate([seg(OFF_Q, BRANCH_W), seg(OFF_S5, BRANCH_W), seg(OFF_POOL, BRANCH_W),
                          seg(OFF_RQ, BRANCH_W), seg(OFF_RK, BRANCH_W), seg(OFF_RV, BRANCH_W),
                          seg(OFF_RG, BRANCH_W), seg(OFF_KV, NSA_KV_COLS)] + gates, axis=1)
    return wa.astype(MX), w[:, OFF_MERGE:].astype(MX)


def kernel(x, p, g_mix, w_in, w_cmp_k, w_cmp_v, s5_a_re, s5_a_im, s5_log_dt, s5_b_re, s5_b_im,
           s5_c_re, s5_c_im, s5_d, s5_w_glu, pool_w, pool_scale, w_branch, w_out, g_mlp,
           w_mlp_up, w_mlp_down, w_ple_gate, w_ple_proj, g_final):
    B, S, D = x.shape
    assert B == 1 and D == D_MODEL
    depth = w_in.shape[0]
    h = x.reshape(S, D)
    for i in range(depth):
        wa, wm = _pack_w_in(w_in[i])
        za, u = _inproj(h, g_mix[i].reshape(1, D), wa, tm=1024, tn=768)

        o_nsa = _nsa_mixer(za, w_cmp_k[i], w_cmp_v[i])
        s5_tabs = _s5_tables(s5_a_re[i], s5_a_im[i], s5_log_dt[i], s5_b_re[i], s5_b_im[i],
                             s5_c_re[i], s5_c_im[i])
        o_s5 = _s5_mixer(za, s5_tabs, s5_d[i], s5_w_glu[i].astype(MX), tm=512)
        o_pool = _pool_mixer(za, pool_w[i].astype(MX), pool_scale[i], tm=1024)
        o_ret = _ret_mixer(za)

        merged = _merge(u, (o_nsa, o_s5, o_pool, o_ret), wm, w_branch[i].astype(MX),
                        tm=1024, tn=256)
        h = _resid_mm(h, merged, w_out[i].astype(MX), tm=1024, tn=1024)
        h = _mlp(h, g_mlp[i].reshape(1, D), w_mlp_up[i].astype(MX),
                 w_mlp_down[i].astype(MX), tm=1024, tf=512)
        h = _ple(h, p[i].reshape(S, PLE_DIM), w_ple_gate[i].astype(MX),
                 w_ple_proj[i].astype(MX), g_final.reshape(1, D), tm=512,
                 final_norm=(i == depth - 1))
    return h.reshape(B, S, D)
```

```python
import functools
import math

import jax
import jax.numpy as jnp
import numpy as np
from jax import lax
from jax.experimental import pallas as pl
from jax.experimental.pallas import tpu as pltpu

F32 = jnp.float32
BF16 = jnp.bfloat16
MX = BF16

D_MODEL = 2048
N_BRANCH = 4
BRANCH_W = D_MODEL // N_BRANCH
NSA_HEADS = 8
NSA_KV_GROUPS = 2
NSA_HEAD_DIM = BRANCH_W // NSA_HEADS
CMP_LEN = 32
CMP_STRIDE = 16
SLC_BLOCK = 64
SLC_TOP_N = 16
WIN = 512
Q_BLOCK = 128
S5_GROUP = 16
S5_GROUPS = BRANCH_W // S5_GROUP
S5_STATE = 64
S5_COLS = S5_GROUPS * S5_STATE
POOL_WINDOWS = (2, 4, 8, 16)
POOL_GROUP = BRANCH_W // len(POOL_WINDOWS)
POOL_HALO = 16
RET_HEADS = 4
RET_HEAD_DIM = BRANCH_W // RET_HEADS
RET_CHUNK = 128
D_FF = 4 * D_MODEL
PLE_DIM = 256
NORM_EPS = 1e-6
NEG_INF = -1e30
FORCE_BONUS = 1e4

NSA_KV_COLS = 6 * NSA_KV_GROUPS * NSA_HEAD_DIM
NSA_GATE_COLS = 3 * NSA_HEADS
OFF_Q = 0
OFF_KV = OFF_Q + BRANCH_W
OFF_G = OFF_KV + NSA_KV_COLS
OFF_S5 = OFF_G + NSA_GATE_COLS
OFF_POOL = OFF_S5 + BRANCH_W
OFF_RQ = OFF_POOL + BRANCH_W
OFF_RK = OFF_RQ + BRANCH_W
OFF_RV = OFF_RK + BRANCH_W
OFF_RG = OFF_RV + BRANCH_W
OFF_MERGE = OFF_RG + BRANCH_W

LANE = 128
SUBLANE = 8
GATE_PAD = LANE
ZA_Q, ZA_S5, ZA_POOL, ZA_RQ, ZA_RK, ZA_RV, ZA_RG = range(7)
ZA_KV_OFF = 7 * BRANCH_W
ZA_G_OFF = ZA_KV_OFF + NSA_KV_COLS
ZA_COLS = ZA_G_OFF + NSA_KV_GROUPS * GATE_PAD

VMEM_LIMIT = 56 * 1024 * 1024


def _params(*sem):
    return pltpu.CompilerParams(dimension_semantics=sem, vmem_limit_bytes=VMEM_LIMIT)


def _rms(x, g):
    return x * lax.rsqrt(jnp.mean(x * x, axis=-1, keepdims=True) + NORM_EPS) * g


def _sigmoid(x):
    return 1.0 / (1.0 + jnp.exp(-x))


def _inproj_kernel(h_ref, g_ref, w_ref, z_ref, u_ref):
    @pl.when(pl.program_id(1) == 0)
    def _():
        u_ref[...] = _rms(h_ref[...], g_ref[...]).astype(MX)

    z_ref[...] = jnp.dot(u_ref[...], w_ref[...], preferred_element_type=F32)


def _inproj(h, g, w, *, tm, tn):
    S, D = h.shape
    N = w.shape[1]
    return pl.pallas_call(
        _inproj_kernel,
        grid=(S // tm, N // tn),
        in_specs=[pl.BlockSpec((tm, D), lambda i, j: (i, 0)),
                  pl.BlockSpec((1, D), lambda i, j: (0, 0)),
                  pl.BlockSpec((D, tn), lambda i, j: (0, j))],
        out_specs=[pl.BlockSpec((tm, tn), lambda i, j: (i, j)),
                   pl.BlockSpec((tm, D), lambda i, j: (i, 0))],
        out_shape=[jax.ShapeDtypeStruct((S, N), F32), jax.ShapeDtypeStruct((S, D), MX)],
        compiler_params=_params("parallel", "arbitrary"),
        name="inproj",
    )(h, g, w)


def _merge_kernel(u_ref, o0_ref, o1_ref, o2_ref, o3_ref, wm0_ref, wm1_ref, wm2_ref, wm3_ref,
                  wb_ref, out_ref):
    u = u_ref[...]
    acc = None
    for j, (o_ref, wm_ref) in enumerate(((o0_ref, wm0_ref), (o1_ref, wm1_ref),
                                         (o2_ref, wm2_ref), (o3_ref, wm3_ref))):
        gate = _sigmoid(jnp.dot(u, wm_ref[...], preferred_element_type=F32))
        br = jnp.dot(o_ref[...], wb_ref[j], preferred_element_type=F32)
        acc = gate * br if acc is None else acc + gate * br
    out_ref[...] = acc.astype(MX)


def _merge(u, outs, wm, wb, *, tm, tn):
    S, D = u.shape
    nt = D // tn
    o_specs = [pl.BlockSpec((tm, BRANCH_W), lambda i, c: (i, 0)) for _ in range(N_BRANCH)]
    wm_specs = [pl.BlockSpec((D, tn), functools.partial(lambda i, c, j: (0, j * nt + c), j=j))
                for j in range(N_BRANCH)]
    return pl.pallas_call(
        _merge_kernel,
        grid=(S // tm, nt),
        in_specs=[pl.BlockSpec((tm, D), lambda i, c: (i, 0))] + o_specs + wm_specs
                 + [pl.BlockSpec((N_BRANCH, BRANCH_W, tn), lambda i, c: (0, 0, c))],
        out_specs=pl.BlockSpec((tm, tn), lambda i, c: (i, c)),
        out_shape=jax.ShapeDtypeStruct((S, D), MX),
        compiler_params=_params("parallel", "arbitrary"),
        name="merge",
    )(u, *outs, wm, wm, wm, wm, wb)


def _resid_mm_kernel(h_ref, x_ref, w_ref, out_ref):
    out_ref[...] = h_ref[...] + jnp.dot(x_ref[...], w_ref[...], preferred_element_type=F32)


def _resid_mm(h, x, w, *, tm, tn):
    S, D = h.shape
    K = x.shape[1]
    return pl.pallas_call(
        _resid_mm_kernel,
        grid=(S // tm, D // tn),
        in_specs=[pl.BlockSpec((tm, tn), lambda i, c: (i, c)),
                  pl.BlockSpec((tm, K), lambda i, c: (i, 0)),
                  pl.BlockSpec((K, tn), lambda i, c: (0, c))],
        out_specs=pl.BlockSpec((tm, tn), lambda i, c: (i, c)),
        out_shape=jax.ShapeDtypeStruct((S, D), F32),
        compiler_params=_params("parallel", "arbitrary"),
        name="resid_mm",
    )(h, x, w)


def _mlp_kernel(h_ref, g_ref, wu_ref, wd_ref, out_ref, v_ref):
    @pl.when(pl.program_id(1) == 0)
    def _():
        h = h_ref[...]
        v_ref[...] = _rms(h, g_ref[...]).astype(MX)
        out_ref[...] = h

    hid = jnp.maximum(jnp.dot(v_ref[...], wu_ref[...].astype(MX), preferred_element_type=F32), 0.0)
    out_ref[...] += jnp.dot((hid * hid).astype(MX), wd_ref[...].astype(MX),
                            preferred_element_type=F32)


def _mlp(h, g, wu, wd, *, tm, tf):
    S, D = h.shape
    return pl.pallas_call(
        _mlp_kernel,
        grid=(S // tm, wu.shape[1] // tf),
        in_specs=[pl.BlockSpec((tm, D), lambda i, f: (i, 0)),
                  pl.BlockSpec((1, D), lambda i, f: (0, 0)),
                  pl.BlockSpec((D, tf), lambda i, f: (0, f)),
                  pl.BlockSpec((tf, D), lambda i, f: (f, 0))],
        out_specs=pl.BlockSpec((tm, D), lambda i, f: (i, 0)),
        out_shape=jax.ShapeDtypeStruct((S, D), F32),
        scratch_shapes=[pltpu.VMEM((tm, D), MX)],
        compiler_params=_params("parallel", "arbitrary"),
        name="mlp",
    )(h, g, wu, wd)


def _ple_kernel(h_ref, p_ref, wg_ref, wp_ref, gf_ref, out_ref, *, final_norm):
    h = h_ref[...]
    gate = _sigmoid(jnp.dot(h.astype(MX), wg_ref[...], preferred_element_type=F32))
    emb = jnp.dot(p_ref[...].astype(MX), wp_ref[...], preferred_element_type=F32)
    y = h + gate * emb
    out_ref[...] = _rms(y, gf_ref[...]) if final_norm else y


def _ple(h, p, wg, wp, gf, *, tm, final_norm):
    S, D = h.shape
    P = p.shape[1]
    return pl.pallas_call(
        functools.partial(_ple_kernel, final_norm=final_norm),
        grid=(S // tm,),
        in_specs=[pl.BlockSpec((tm, D), lambda i: (i, 0)),
                  pl.BlockSpec((tm, P), lambda i: (i, 0)),
                  pl.BlockSpec((D, D), lambda i: (0, 0)),
                  pl.BlockSpec((P, D), lambda i: (0, 0)),
                  pl.BlockSpec((1, D), lambda i: (0, 0))],
        out_specs=pl.BlockSpec((tm, D), lambda i: (i, 0)),
        out_shape=jax.ShapeDtypeStruct((S, D), F32),
        compiler_params=_params("parallel"),
        name="ple",
    )(h, p, wg, wp, gf)


S5_PANELS = BRANCH_W // LANE
S5_PANEL_COLS = S5_COLS // S5_PANELS
T_A0, T_A1, T_A2, T_A4 = range(4)


def _cmul_add(xr, xi, ar, ai, br, bi):
    return xr + (ar * br - ai * bi), xi + (ar * bi + ai * br)


def _s5_kernel(u_ref, bre_ref, bim_ref, tab_ref, cre_ref, cim_ref, d_ref, wglu_ref, out_ref,
               hre_ref, him_ref, carry_ref):
    tm = u_ref.shape[0]

    @pl.when(pl.program_id(0) == 0)
    def _():
        carry_ref[...] = jnp.zeros_like(carry_ref)

    u = u_ref[...]
    ub = u.astype(MX)
    for k in range(S5_PANELS):
        uk = ub[:, k * LANE:(k + 1) * LANE]
        cols = slice(k * S5_PANEL_COLS, (k + 1) * S5_PANEL_COLS)
        hre_ref[:, cols] = jnp.dot(uk, bre_ref[k], preferred_element_type=F32)
        him_ref[:, cols] = jnp.dot(uk, bim_ref[k], preferred_element_type=F32)

    def body(r, carry):
        rows = pl.ds(pl.multiple_of(r * SUBLANE, SUBLANE), SUBLANE)
        for cb in range(S5_COLS // LANE):
            cols = slice(cb * LANE, (cb + 1) * LANE)
            xr, xi = _cmul_add(hre_ref[rows, cols], him_ref[rows, cols],
                               tab_ref[T_A0, 0, :, cols], tab_ref[T_A0, 1, :, cols],
                               carry_ref[0, :, cols], carry_ref[1, :, cols])
            for k, sh in ((T_A1, 1), (T_A2, 2), (T_A4, 4)):
                xr, xi = _cmul_add(xr, xi, tab_ref[k, 0, :, cols], tab_ref[k, 1, :, cols],
                                   pltpu.roll(xr, sh, 0), pltpu.roll(xi, sh, 0))
            hre_ref[rows, cols] = xr
            him_ref[rows, cols] = xi
            carry_ref[0, :, cols] = jnp.broadcast_to(xr[SUBLANE - 1:SUBLANE, :], xr.shape)
            carry_ref[1, :, cols] = jnp.broadcast_to(xi[SUBLANE - 1:SUBLANE, :], xi.shape)
        return carry

    lax.fori_loop(0, tm // SUBLANE, body, 0)

    ys = []
    for k in range(S5_PANELS):
        cols = slice(k * S5_PANEL_COLS, (k + 1) * S5_PANEL_COLS)
        ys.append(jnp.dot(hre_ref[:, cols].astype(MX), cre_ref[k], preferred_element_type=F32)
                  + jnp.dot(him_ref[:, cols].astype(MX), cim_ref[k], preferred_element_type=F32))
    y = jnp.concatenate(ys, axis=1) + d_ref[...] * u
    y = 0.5 * y * (1.0 + jnp.tanh(math.sqrt(2.0 / math.pi) * (y + 0.044715 * (y * y * y))))
    gl = jnp.dot(y.astype(MX), wglu_ref[...], preferred_element_type=F32)
    out_ref[...] = (gl[:, :BRANCH_W] * _sigmoid(gl[:, BRANCH_W:])).astype(MX)


def _s5_tables(a_re, a_im, log_dt, b_re, b_im, c_re, c_im):
    dt = jnp.exp(log_dt)[:, None]
    xr, xi = a_re * dt, a_im * dt

    pr = jnp.exp(xr) * jnp.cos(xi)
    pi = jnp.exp(xr) * jnp.sin(xi)
    den = a_re * a_re + a_im * a_im
    qr = ((pr - 1.0) * a_re + pi * a_im) / den
    qi = (pi * a_re - (pr - 1.0) * a_im) / den
    bbar_re = qr[..., None] * b_re - qi[..., None] * b_im
    bbar_im = qr[..., None] * b_im + qi[..., None] * b_re

    flat = lambda z: z.reshape(1, S5_COLS)
    powers = [(flat(pr), flat(pi))]
    for _ in range(3):
        zr, zi = powers[-1]
        powers.append((zr * flat(pr) - zi * flat(pi), zr * flat(pi) + zi * flat(pr)))
    row = jnp.arange(SUBLANE)[:, None]
    planes = [(jnp.where(row == 0, flat(pr), 0.0), jnp.where(row == 0, flat(pi), 0.0))]
    for sh in (1, 2, 4):
        zr, zi = powers[sh - 1]
        planes.append((jnp.where(row >= sh, zr, 0.0), jnp.where(row >= sh, zi, 0.0)))
    tab = jnp.stack([jnp.stack(p) for p in planes]).astype(F32)

    pg = S5_GROUPS // S5_PANELS
    eye = jnp.eye(pg, dtype=F32)

    def bdiag_in(m):
        m = m.reshape(S5_PANELS, pg, S5_STATE, S5_GROUP)
        return jnp.einsum('pgnc,gh->pgchn', m, eye).reshape(S5_PANELS, LANE, S5_PANEL_COLS)

    def bdiag_out(m):
        m = m.reshape(S5_PANELS, pg, S5_GROUP, S5_STATE)
        return jnp.einsum('pgcn,gh->pgnhc', m, eye).reshape(S5_PANELS, S5_PANEL_COLS, LANE)

    return (tab, bdiag_in(bbar_re).astype(MX), bdiag_in(bbar_im).astype(MX),
            bdiag_out(c_re).astype(MX), bdiag_out(-c_im).astype(MX))


def _s5_mixer(za, tabs, d_skip, w_glu, *, tm):
    S = za.shape[0]
    tab, bre, bim, cre, cim = tabs
    full = lambda shape: pl.BlockSpec(shape, lambda i: (0,) * len(shape))
    return pl.pallas_call(
        _s5_kernel,
        grid=(S // tm,),
        in_specs=[pl.BlockSpec((tm, BRANCH_W), lambda i: (i, ZA_S5)),
                  full(bre.shape), full(bim.shape), full(tab.shape), full(cre.shape),
                  full(cim.shape), full((1, BRANCH_W)), full(w_glu.shape)],
        out_specs=pl.BlockSpec((tm, BRANCH_W), lambda i: (i, 0)),
        out_shape=jax.ShapeDtypeStruct((S, BRANCH_W), MX),
        scratch_shapes=[pltpu.VMEM((tm, S5_COLS), F32), pltpu.VMEM((tm, S5_COLS), F32),
                        pltpu.VMEM((2, SUBLANE, S5_COLS), F32)],
        compiler_params=_params("arbitrary"),
        name="s5",
    )(za, bre, bim, tab, cre, cim, d_skip.reshape(1, BRANCH_W), w_glu)


def _pool_kernel(x_ref, w_ref, scale_ref, out_ref, ext_ref):
    tm = x_ref.shape[0]
    i = pl.program_id(0)

    @pl.when(i == 0)
    def _():
        ext_ref[0:POOL_HALO, :] = jnp.zeros((POOL_HALO, BRANCH_W), F32)

    x = x_ref[...]
    ext_ref[POOL_HALO:, :] = x
    t = i * tm + lax.broadcasted_iota(jnp.int32, (tm, 1), 0)
    for gi, w in enumerate(POOL_WINDOWS):
        cols = slice(gi * POOL_GROUP, (gi + 1) * POOL_GROUP)
        acc = x[:, cols]
        for d in range(1, w):
            acc = acc + ext_ref[POOL_HALO - d:POOL_HALO - d + tm, cols]
        count = jnp.minimum(t + 1, w).astype(F32)
        diff = (acc / count - x[:, cols]).astype(MX)
        y = jnp.dot(diff, w_ref[gi], preferred_element_type=F32)
        out_ref[:, cols] = (y * scale_ref[:, cols]).astype(MX)
    ext_ref[0:POOL_HALO, :] = x[tm - POOL_HALO:, :]


def _pool_mixer(za, w_pool, pool_scale, *, tm):
    S = za.shape[0]
    return pl.pallas_call(
        _pool_kernel,
        grid=(S // tm,),
        in_specs=[pl.BlockSpec((tm, BRANCH_W), lambda i: (i, ZA_POOL)),
                  pl.BlockSpec(w_pool.shape, lambda i: (0, 0, 0)),
                  pl.BlockSpec((1, BRANCH_W), lambda i: (0, 0))],
        out_specs=pl.BlockSpec((tm, BRANCH_W), lambda i: (i, 0)),
        out_shape=jax.ShapeDtypeStruct((S, BRANCH_W), MX),
        scratch_shapes=[pltpu.VMEM((tm + POOL_HALO, BRANCH_W), F32)],
        compiler_params=_params("arbitrary"),
        name="pool",
    )(za, w_pool, pool_scale.reshape(1, BRANCH_W))


def _rotary_tables(S):
    dh, C = RET_HEAD_DIM, RET_CHUNK
    angle = np.repeat(1.0 / (10000.0 ** np.linspace(0.0, 1.0, dh // 2)), 2)[None, :]
    within = np.arange(C, dtype=np.float64)[:, None] * angle
    start = (np.arange(S // C, dtype=np.float64) * C)[:, None] * angle
    tab = lambda a: jnp.asarray(a, dtype=F32)
    return (tab(np.cos(within)), tab(np.sin(within)),
            tab(np.cos(start)[:, None, :]), tab(np.sin(start)[:, None, :]))


def _ret_consts():
    H, dh, C = RET_HEADS, RET_HEAD_DIM, RET_CHUNK
    log_g = jnp.log(1.0 - 2.0 ** (-5.0 - jnp.arange(H, dtype=F32)))
    idx = jnp.arange(C, dtype=F32)
    rel = idx[:, None] - idx[None, :]
    decay = jnp.where(rel >= 0, jnp.exp(jnp.maximum(rel, 0.0)[None] * log_g[:, None, None]), 0.0)
    xi = jnp.exp((idx + 1.0)[None, :] * log_g[:, None])[..., None]
    zeta = jnp.exp((C - 1.0 - idx)[None, :] * log_g[:, None])[..., None]
    g_chunk = jnp.broadcast_to(jnp.exp(C * log_g)[:, None, None], (H, 1, dh))
    return decay, xi, zeta, g_chunk


def _rotate_pairs(x, even):
    n = x.shape[-1]
    return jnp.where(even, -pltpu.roll(x, n - 1, 1), pltpu.roll(x, 1, 1))


def _ret_kernel(q_ref, k_ref, v_ref, g_ref, ca_ref, sa_ref, cb_ref, sb_ref, decay_ref, xi_ref,
                zeta_ref, gch_ref, out_ref, state_ref):
    C, dh = RET_CHUNK, RET_HEAD_DIM

    @pl.when(pl.program_id(0) == 0)
    def _():
        state_ref[...] = jnp.zeros_like(state_ref)

    ca, sa, cb, sb = ca_ref[...], sa_ref[...], cb_ref[0], sb_ref[0]
    cos = ca * cb - sa * sb
    sin = sa * cb + ca * sb
    even = (lax.broadcasted_iota(jnp.int32, (C, dh), 1) % 2) == 0
    for hd in range(RET_HEADS):
        cols = slice(hd * dh, (hd + 1) * dh)
        q = q_ref[:, cols]
        k = k_ref[:, cols]
        vb = v_ref[:, cols].astype(MX)
        q = q * cos + _rotate_pairs(q, even) * sin
        k = (k * cos + _rotate_pairs(k, even) * sin) * dh ** -0.5
        qb = q.astype(MX)
        s = lax.dot_general(qb, k.astype(MX), (((1,), (1,)), ((), ())),
                            preferred_element_type=F32) * decay_ref[hd]
        inner = jnp.dot(s.astype(MX), vb, preferred_element_type=F32)
        state = state_ref[hd]
        cross = jnp.dot(qb, state.astype(MX), preferred_element_type=F32) * xi_ref[hd]
        kz = (k * zeta_ref[hd]).astype(MX)
        state_ref[hd] = state * gch_ref[hd] + lax.dot_general(
            kz, vb, (((0,), (0,)), ((), ())), preferred_element_type=F32)
        y = inner + cross
        y = y * lax.rsqrt(jnp.mean(y * y, axis=-1, keepdims=True) + NORM_EPS)
        g = g_ref[:, cols]
        out_ref[:, cols] = (g * _sigmoid(g) * y).astype(MX)


def _ret_mixer(za):
    S = za.shape[0]
    C = RET_CHUNK
    ca, sa, cb, sb = _rotary_tables(S)
    consts = (ca, sa, cb, sb) + _ret_consts()
    col = lambda c: pl.BlockSpec((C, BRANCH_W), lambda i: (i, c))
    full = lambda a: pl.BlockSpec(a.shape, lambda i: (0,) * a.ndim)
    start = lambda a: pl.BlockSpec((1,) + a.shape[1:], lambda i: (i, 0, 0))
    return pl.pallas_call(
        _ret_kernel,
        grid=(S // C,),
        in_specs=[col(ZA_RQ), col(ZA_RK), col(ZA_RV), col(ZA_RG), full(ca), full(sa), start(cb),
                  start(sb)] + [full(a) for a in consts[4:]],
        out_specs=pl.BlockSpec((C, BRANCH_W), lambda i: (i, 0)),
        out_shape=jax.ShapeDtypeStruct((S, BRANCH_W), MX),
        scratch_shapes=[pltpu.VMEM((RET_HEADS, RET_HEAD_DIM, RET_HEAD_DIM), F32)],
        compiler_params=_params("arbitrary"),
        name="retention",
    )(za, za, za, za, *consts)


NSA_R = NSA_HEADS // NSA_KV_GROUPS
NSA_TQ = 128
SLC_TQ = 512
WIN_TQ = 256
NSA_TK = 512
CMP_CH = 256
P_OFF = SUBLANE
NT_DIMS = (((1,), (1,)), ((), ()))


def _compress_kernel(x_ref, w_ref, out_ref):
    out_ref[0] = jnp.dot(x_ref[0], w_ref[...], preferred_element_type=F32).astype(out_ref.dtype)


def _compress_t_kernel(x_ref, wt_ref, out_ref):
    out_ref[0] = lax.dot_general(wt_ref[...], x_ref[0], NT_DIMS,
                                 preferred_element_type=F32).astype(out_ref.dtype)


def _compress(x, w, *, transposed):
    G, nc, kk = x.shape
    if transposed:
        kern, out_block = _compress_t_kernel, (1, w.shape[0], nc)
    else:
        kern, out_block = _compress_kernel, (1, nc, w.shape[1])
    return pl.pallas_call(
        kern,
        grid=(G,),
        in_specs=[pl.BlockSpec((1, nc, kk), lambda g: (g, 0, 0)),
                  pl.BlockSpec(w.shape, lambda g: (0, 0))],
        out_specs=pl.BlockSpec(out_block, lambda g: (g, 0, 0)),
        out_shape=jax.ShapeDtypeStruct((G,) + out_block[1:], x.dtype),
        compiler_params=_params("arbitrary"),
        name="nsa_compress",
    )(x, w)


def _group_queries(q_ref, g, extra_scale=1.0):
    hd = NSA_HEAD_DIM
    base = g * NSA_R * hd
    qs = [q_ref[:, base + r * hd:base + (r + 1) * hd] for r in range(NSA_R)]
    return (jnp.concatenate(qs, axis=0) * (hd ** -0.5 * extra_scale)).astype(MX)


def _cmp_select_kernel(q_ref, kc_ref, vct_ref, ocmp_ref, sel_ref, s_sc, p_sc, ot_sc):
    i = pl.program_id(0)
    tq, hd = NSA_TQ, NSA_HEAD_DIM
    nc = kc_ref.shape[1]
    ns = sel_ref.shape[2]
    ch = min(CMP_CH, nc)
    nch = jnp.minimum((tq // CMP_STRIDE * (i + 1) + ch - 1) // ch, nc // ch)
    t4 = i * tq + lax.broadcasted_iota(jnp.int32, (1, NSA_R * tq), 1) % tq
    t1 = i * tq + lax.broadcasted_iota(jnp.int32, (1, tq), 1)
    nrow = lax.broadcasted_iota(jnp.int32, (ch, 1), 0)
    jrow = lax.broadcasted_iota(jnp.int32, (ns, 1), 0)
    jfull = jnp.broadcast_to(jrow.astype(F32), (ns, tq))

    def rows(c):
        return pl.ds(pl.multiple_of(c * ch, ch), ch)

    def visible(c):
        return ((c * ch + nrow) * CMP_STRIDE + (CMP_LEN - 1)) <= t4

    groups = range(NSA_KV_GROUPS)
    qgs = [_group_queries(q_ref, g) for g in groups]
    lanes = (1, NSA_R * tq)

    def scores(c, ms):
        vis = visible(c)
        out = []
        for g in groups:
            s = lax.dot_general(kc_ref[g, rows(c), :], qgs[g], NT_DIMS, preferred_element_type=F32)
            s = jnp.where(vis, s, NEG_INF)
            s_sc[g, rows(c), :] = s
            out.append(jnp.maximum(ms[g], jnp.max(s, axis=0, keepdims=True)))
        return tuple(out)

    ms = lax.fori_loop(0, nch, scores, tuple(jnp.full(lanes, NEG_INF, F32) for _ in groups))

    def expsum(c, ls):
        vis = visible(c)
        out = []
        for g in groups:
            e = jnp.where(vis, jnp.exp(s_sc[g, rows(c), :] - ms[g]), 0.0)
            s_sc[g, rows(c), :] = e
            out.append(ls[g] + jnp.sum(e, axis=0, keepdims=True))
        return tuple(out)

    ls = lax.fori_loop(0, nch, expsum, tuple(jnp.zeros(lanes, F32) for _ in groups))
    invs = [1.0 / jnp.maximum(l, 1e-30) for l in ls]
    p_sc[...] = jnp.zeros_like(p_sc)
    ot_sc[...] = jnp.zeros_like(ot_sc)

    def weigh(c, carry):
        for g in groups:
            p = s_sc[g, rows(c), :] * invs[g]
            p_sc[g, pl.ds(pl.multiple_of(P_OFF + c * ch, SUBLANE), ch), :] = (
                p[:, 0:tq] + p[:, tq:2 * tq] + p[:, 2 * tq:3 * tq] + p[:, 3 * tq:4 * tq])
            ot_sc[g] += jnp.dot(vct_ref[g, :, rows(c)], p.astype(MX), preferred_element_type=F32)
        return carry

    lax.fori_loop(0, nch, weigh, 0)
    for h in range(NSA_HEADS):
        g, r = divmod(h, NSA_R)
        o = ot_sc[g, :, r * tq:(r + 1) * tq].T
        ocmp_ref[:, h * hd:(h + 1) * hd] = o[:, :hd]

    ratio = SLC_BLOCK // CMP_STRIDE
    valid = jrow * SLC_BLOCK <= t1
    bonus = jnp.where((jrow == 0) | (jrow == t1 // SLC_BLOCK), FORCE_BONUS, 0.0)
    xs = []
    for g in groups:
        imp = p_sc[g, pl.ds(P_OFF - 1, ns, stride=ratio), :]
        for k in range(ratio):
            imp = imp + p_sc[g, pl.ds(P_OFF + k, ns, stride=ratio), :]
        xs.append(jnp.where(valid, imp + bonus, NEG_INF))
    for _ in range(min(SLC_TOP_N, ns)):
        for g in groups:
            mx = jnp.max(xs[g], axis=0, keepdims=True)
            cand = jnp.where(xs[g] == mx, jfull, float(ns))
            xs[g] = jnp.where(cand == jnp.min(cand, axis=0, keepdims=True), -jnp.inf, xs[g])
    for g in groups:
        sel_ref[g] = jnp.where(xs[g] == -jnp.inf, 1.0, 0.0).T.astype(BF16)


def _cmp_select(za, kc, vct):
    S = za.shape[0]
    G, nc, hd = kc.shape
    ns = S // SLC_BLOCK
    tq = NSA_TQ
    return pl.pallas_call(
        _cmp_select_kernel,
        grid=(S // tq,),
        in_specs=[pl.BlockSpec((tq, BRANCH_W), lambda i: (i, ZA_Q)),
                  pl.BlockSpec(kc.shape, lambda i: (0, 0, 0)),
                  pl.BlockSpec(vct.shape, lambda i: (0, 0, 0))],
        out_specs=[pl.BlockSpec((tq, BRANCH_W), lambda i: (i, 0)),
                   pl.BlockSpec((G, tq, ns), lambda i: (0, i, 0))],
        out_shape=[jax.ShapeDtypeStruct((S, BRANCH_W), F32),
                   jax.ShapeDtypeStruct((G, S, ns), BF16)],
        scratch_shapes=[pltpu.VMEM((G, nc, NSA_R * tq), F32), pltpu.VMEM((G, P_OFF + nc, tq), F32),
                        pltpu.VMEM((G, LANE, NSA_R * tq), F32)],
        compiler_params=_params("parallel"),
        name="nsa_cmp_select",
    )(za, kc, vct)


SLC_MASK = 2.0 ** 100
LOG2E = math.log2(math.e)


def _slc_kernel(q_ref, kx_ref, vs_ref, sel_ref, out_ref, lhs_sc, m_sc, acc_sc, *, tk):
    i = pl.program_id(1)
    tq, hd = SLC_TQ, NSA_HEAD_DIM
    ksel = kx_ref.shape[2] - LANE
    bpt = tk // SLC_BLOCK
    qg = _group_queries(q_ref, 0, LOG2E)
    pad = jnp.zeros((NSA_R * tq, LANE - hd), MX)
    for h in range(sel_ref.shape[2] // ksel):
        unsel = (1.0 - sel_ref[0, :, h * ksel:(h + 1) * ksel].astype(F32)).astype(MX)
        lhs_sc[h] = jnp.concatenate([qg, pad, jnp.concatenate([unsel] * NSA_R, axis=0)], axis=1)
    t = i * tq + lax.broadcasted_iota(jnp.int32, (tq, 1), 0)
    lane = lax.broadcasted_iota(jnp.int32, (1, tk), 1)
    m_sc[...] = jnp.full_like(m_sc, NEG_INF)
    acc_sc[...] = jnp.zeros_like(acc_sc)

    def tile(kt, diagonal):
        k0 = pl.multiple_of(kt * tk, tk)
        s = lax.dot_general(lhs_sc[(kt * bpt) // ksel], kx_ref[0, pl.ds(k0, tk), :], NT_DIMS,
                            preferred_element_type=F32)
        if diagonal:
            hidden = jnp.where(k0 + lane <= t, 0.0, NEG_INF)
            s = s + jnp.concatenate([hidden] * NSA_R, axis=0)
        m_prev = m_sc[...]
        m_new = jnp.maximum(m_prev, jnp.max(s, axis=-1, keepdims=True))
        p = jnp.exp2(s - jnp.concatenate([m_new] * (tk // LANE), axis=1))
        acc_sc[...] = jnp.exp2(m_prev - m_new) * acc_sc[...] + jnp.dot(
            p.astype(MX), vs_ref[0, pl.ds(k0, tk), :], preferred_element_type=F32)
        m_sc[...] = m_new

    last = (i * tq) // tk

    def quad(j, carry):
        for u in range(4):
            tile(4 * j + u, False)
        return carry

    lax.fori_loop(0, last // 4, quad, 0)
    done = (last // 4) * 4

    @pl.when(last - done >= 2)
    def _():
        tile(done, False)
        tile(done + 1, False)

    @pl.when((last - done) % 2 == 1)
    def _():
        tile(last - 1, False)

    tile(last, True)
    acc = acc_sc[...]
    o = acc[:, :hd] / acc[:, hd:]
    for r in range(NSA_R):
        out_ref[:, r * hd:(r + 1) * hd] = o[r * tq:(r + 1) * tq, :]


def _slc_attention(za, ks, vs, sel):
    S = za.shape[0]
    G = ks.shape[0]
    ns = sel.shape[2]
    tq, hd = SLC_TQ, NSA_HEAD_DIM
    tk = min(NSA_TK, S)
    ksel = min(LANE, ns)
    own_block = (jnp.arange(S)[:, None] // SLC_BLOCK) % ksel == jnp.arange(ksel)[None, :]
    kx = jnp.concatenate([ks, jnp.zeros((G, S, LANE - hd), ks.dtype),
                          jnp.broadcast_to(jnp.where(own_block, -SLC_MASK, 0.0).astype(ks.dtype),
                                           (G, S, ksel))], axis=2)
    gw = NSA_R * hd
    return pl.pallas_call(
        functools.partial(_slc_kernel, tk=tk),
        grid=(G, S // tq),
        in_specs=[pl.BlockSpec((tq, gw), lambda g, i: (i, g)),
                  pl.BlockSpec((1, S, LANE + ksel), lambda g, i: (g, 0, 0)),
                  pl.BlockSpec((1, S, LANE), lambda g, i: (g, 0, 0)),
                  pl.BlockSpec((1, tq, ns), lambda g, i: (g, i, 0))],
        out_specs=pl.BlockSpec((tq, gw), lambda g, i: (i, g)),
        out_shape=jax.ShapeDtypeStruct((S, BRANCH_W), F32),
        scratch_shapes=[pltpu.VMEM((ns // ksel, NSA_R * tq, LANE + ksel), MX),
                        pltpu.VMEM((NSA_R * tq, LANE), F32), pltpu.VMEM((NSA_R * tq, LANE), F32)],
        compiler_params=_params("arbitrary", "arbitrary"),
        name="nsa_selected",
    )(za, kx, vs, sel)


def _win_kernel(q_ref, kw_ref, vw_ref, gate_ref, ocmp_ref, oslc_ref, out_ref):
    i = pl.program_id(1)
    tq, hd = WIN_TQ, NSA_HEAD_DIM
    span = tq + WIN
    qg = _group_queries(q_ref, 0, LOG2E)
    r0 = pl.multiple_of(i * tq, tq)
    t = i * tq + lax.broadcasted_iota(jnp.int32, (tq, 1), 0)
    pos = i * tq - WIN + lax.broadcasted_iota(jnp.int32, (1, span), 1)
    ok = (pos <= t) & (pos > t - WIN) & (pos >= 0)
    s = lax.dot_general(qg, kw_ref[0, pl.ds(r0, span), :], NT_DIMS, preferred_element_type=F32)
    s = s + jnp.concatenate([jnp.where(ok, 0.0, NEG_INF)] * NSA_R, axis=0)
    e = jnp.exp2(s - jnp.max(s, axis=-1, keepdims=True))
    acc = jnp.dot(e.astype(MX), vw_ref[0, pl.ds(r0, span), :], preferred_element_type=F32)
    ow = acc[:, :hd] / acc[:, hd:]
    sig = _sigmoid(gate_ref[...])
    for r in range(NSA_R):
        cols = slice(r * hd, (r + 1) * hd)
        out_ref[:, cols] = (sig[:, r:r + 1] * ocmp_ref[:, cols]
                            + sig[:, NSA_R + r:NSA_R + r + 1] * oslc_ref[:, cols]
                            + sig[:, 2 * NSA_R + r:2 * NSA_R + r + 1] * ow[r * tq:(r + 1) * tq, :]
                            ).astype(MX)


def _win_combine(za, kw, vw, ocmp, oslc):
    S = za.shape[0]
    G = kw.shape[0]
    tq, hd = WIN_TQ, NSA_HEAD_DIM
    gw = NSA_R * hd
    blk = lambda: pl.BlockSpec((tq, gw), lambda g, i: (i, g))
    return pl.pallas_call(
        _win_kernel,
        grid=(G, S // tq),
        in_specs=[blk(),
                  pl.BlockSpec((1, S + WIN, hd), lambda g, i: (g, 0, 0)),
                  pl.BlockSpec((1, S + WIN, LANE), lambda g, i: (g, 0, 0)),
                  pl.BlockSpec((tq, GATE_PAD), lambda g, i: (i, ZA_G_OFF // GATE_PAD + g)),
                  blk(), blk()],
        out_specs=blk(),
        out_shape=jax.ShapeDtypeStruct((S, BRANCH_W), MX),
        compiler_params=_params("arbitrary", "arbitrary"),
        name="nsa_window_combine",
    )(za, kw, vw, za, ocmp, oslc)


def _nsa_mixer(za, w_cmp_k, w_cmp_v):
    S = za.shape[0]
    G, hd = NSA_KV_GROUPS, NSA_HEAD_DIM
    nc = S // CMP_STRIDE

    def kv(idx):
        off = ZA_KV_OFF + idx * G * hd
        return jnp.stack([za[:, off + g * hd:off + (g + 1) * hd] for g in range(G)])

    def blocks(x):
        half = x.reshape(G, nc, CMP_STRIDE * hd)
        nxt = jnp.concatenate([half[:, 1:], jnp.zeros_like(half[:, :1])], axis=1)
        return jnp.concatenate([half, nxt], axis=2).astype(MX)

    def with_ones(x):
        return jnp.concatenate([x, jnp.ones((G, S, LANE - hd), x.dtype)], axis=2).astype(MX)

    front = lambda x: jnp.pad(x, ((0, 0), (WIN, 0), (0, 0)))
    wk = w_cmp_k.reshape(CMP_LEN * hd, hd).astype(MX)
    wvt = jnp.pad(w_cmp_v.reshape(CMP_LEN * hd, hd).T, ((0, LANE - hd), (0, 0))).astype(MX)
    kc = _compress(blocks(kv(0)), wk, transposed=False)
    vct = _compress(blocks(kv(1)), wvt, transposed=True)
    ocmp, sel = _cmp_select(za, kc, vct)
    oslc = _slc_attention(za, kv(2).astype(MX), with_ones(kv(3)), sel)
    return _win_combine(za, front(kv(4).astype(MX)), front(with_ones(kv(5))), ocmp, oslc)


def _pack_w_in(w):
    seg = lambda off, n: w[:, off:off + n]
    R = NSA_HEADS // NSA_KV_GROUPS
    gates = []
    for g in range(NSA_KV_GROUPS):
        cols = [OFF_G + br * NSA_HEADS + g * R + r for br in range(3) for r in range(R)]
        gates.append(jnp.pad(w[:, jnp.array(cols)], ((0, 0), (0, GATE_PAD - len(cols)))))
    wa = jnp.concatenate([seg(OFF_Q, BRANCH_W), seg(OFF_S5, BRANCH_W), seg(OFF_POOL, BRANCH_W),
                          seg(OFF_RQ, BRANCH_W), seg(OFF_RK, BRANCH_W), seg(OFF_RV, BRANCH_W),
                          seg(OFF_RG, BRANCH_W), seg(OFF_KV, NSA_KV_COLS)] + gates, axis=1)
    return wa.astype(MX), w[:, OFF_MERGE:].astype(MX)


def kernel(x, p, g_mix, w_in, w_cmp_k, w_cmp_v, s5_a_re, s5_a_im, s5_log_dt, s5_b_re, s5_b_im,
           s5_c_re, s5_c_im, s5_d, s5_w_glu, pool_w, pool_scale, w_branch, w_out, g_mlp,
           w_mlp_up, w_mlp_down, w_ple_gate, w_ple_proj, g_final):
    B, S, D = x.shape
    assert B == 1 and D == D_MODEL
    depth = w_in.shape[0]
    h = x.reshape(S, D)
    for i in range(depth):
        wa, wm = _pack_w_in(w_in[i])
        za, u = _inproj(h, g_mix[i].reshape(1, D), wa, tm=1024, tn=768)

        o_nsa = _nsa_mixer(za, w_cmp_k[i], w_cmp_v[i])
        s5_tabs = _s5_tables(s5_a_re[i], s5_a_im[i], s5_log_dt[i], s5_b_re[i], s5_b_im[i],
                             s5_c_re[i], s5_c_im[i])
        o_s5 = _s5_mixer(za, s5_tabs, s5_d[i], s5_w_glu[i].astype(MX), tm=512)
        o_pool = _pool_mixer(za, pool_w[i].astype(MX), pool_scale[i], tm=1024)
        o_ret = _ret_mixer(za)

        merged = _merge(u, (o_nsa, o_s5, o_pool, o_ret), wm, w_branch[i].astype(MX),
                        tm=1024, tn=256)
        h = _resid_mm(h, merged, w_out[i].astype(MX), tm=1024, tn=1024)
        h = _mlp(h, g_mlp[i].reshape(1, D), w_mlp_up[i], w_mlp_down[i], tm=1024, tf=512)
        h = _ple(h, p[i].reshape(S, PLE_DIM), w_ple_gate[i].astype(MX),
                 w_ple_proj[i].astype(MX), g_final.reshape(1, D), tm=512,
                 final_norm=(i == depth - 1))
    return h.reshape(B, S, D)
```

```python
import functools
import math

import jax
import jax.numpy as jnp
import numpy as np
from jax import lax
from jax.experimental import pallas as pl
from jax.experimental.pallas import tpu as pltpu

F32 = jnp.float32
BF16 = jnp.bfloat16
MX = BF16

D_MODEL = 2048
N_BRANCH = 4
BRANCH_W = D_MODEL // N_BRANCH
NSA_HEADS = 8
NSA_KV_GROUPS = 2
NSA_HEAD_DIM = BRANCH_W // NSA_HEADS
CMP_LEN = 32
CMP_STRIDE = 16
SLC_BLOCK = 64
SLC_TOP_N = 16
WIN = 512
Q_BLOCK = 128
S5_GROUP = 16
S5_GROUPS = BRANCH_W // S5_GROUP
S5_STATE = 64
S5_COLS = S5_GROUPS * S5_STATE
POOL_WINDOWS = (2, 4, 8, 16)
POOL_GROUP = BRANCH_W // len(POOL_WINDOWS)
POOL_HALO = 16
RET_HEADS = 4
RET_HEAD_DIM = BRANCH_W // RET_HEADS
RET_CHUNK = 128
D_FF = 4 * D_MODEL
PLE_DIM = 256
NORM_EPS = 1e-6
NEG_INF = -1e30
FORCE_BONUS = 1e4

NSA_KV_COLS = 6 * NSA_KV_GROUPS * NSA_HEAD_DIM
NSA_GATE_COLS = 3 * NSA_HEADS
OFF_Q = 0
OFF_KV = OFF_Q + BRANCH_W
OFF_G = OFF_KV + NSA_KV_COLS
OFF_S5 = OFF_G + NSA_GATE_COLS
OFF_POOL = OFF_S5 + BRANCH_W
OFF_RQ = OFF_POOL + BRANCH_W
OFF_RK = OFF_RQ + BRANCH_W
OFF_RV = OFF_RK + BRANCH_W
OFF_RG = OFF_RV + BRANCH_W
OFF_MERGE = OFF_RG + BRANCH_W

LANE = 128
SUBLANE = 8
GATE_PAD = LANE
ZA_Q, ZA_S5, ZA_POOL, ZA_RQ, ZA_RK, ZA_RV, ZA_RG = range(7)
ZA_KV_OFF = 7 * BRANCH_W
ZA_G_OFF = ZA_KV_OFF + NSA_KV_COLS
ZA_COLS = ZA_G_OFF + NSA_KV_GROUPS * GATE_PAD

VMEM_LIMIT = 56 * 1024 * 1024


def _params(*sem):
    return pltpu.CompilerParams(dimension_semantics=sem, vmem_limit_bytes=VMEM_LIMIT)


def _rms(x, g):
    return x * lax.rsqrt(jnp.mean(x * x, axis=-1, keepdims=True) + NORM_EPS) * g


def _sigmoid(x):
    return 1.0 / (1.0 + jnp.exp(-x))


def _inproj_kernel(h_ref, g_ref, w_ref, z_ref, u_ref):
    @pl.when(pl.program_id(1) == 0)
    def _():
        u_ref[...] = _rms(h_ref[...], g_ref[...]).astype(MX)

    z_ref[...] = jnp.dot(u_ref[...], w_ref[...], preferred_element_type=F32)


def _inproj(h, g, w, *, tm, tn):
    S, D = h.shape
    N = w.shape[1]
    return pl.pallas_call(
        _inproj_kernel,
        grid=(S // tm, N // tn),
        in_specs=[pl.BlockSpec((tm, D), lambda i, j: (i, 0)),
                  pl.BlockSpec((1, D), lambda i, j: (0, 0)),
                  pl.BlockSpec((D, tn), lambda i, j: (0, j))],
        out_specs=[pl.BlockSpec((tm, tn), lambda i, j: (i, j)),
                   pl.BlockSpec((tm, D), lambda i, j: (i, 0))],
        out_shape=[jax.ShapeDtypeStruct((S, N), F32), jax.ShapeDtypeStruct((S, D), MX)],
        compiler_params=_params("parallel", "arbitrary"),
        name="inproj",
    )(h, g, w)


def _merge_kernel(u_ref, o0_ref, o1_ref, o2_ref, o3_ref, wm0_ref, wm1_ref, wm2_ref, wm3_ref,
                  wb_ref, out_ref):
    u = u_ref[...]
    acc = None
    for j, (o_ref, wm_ref) in enumerate(((o0_ref, wm0_ref), (o1_ref, wm1_ref),
                                         (o2_ref, wm2_ref), (o3_ref, wm3_ref))):
        gate = _sigmoid(jnp.dot(u, wm_ref[...], preferred_element_type=F32))
        br = jnp.dot(o_ref[...], wb_ref[j], preferred_element_type=F32)
        acc = gate * br if acc is None else acc + gate * br
    out_ref[...] = acc.astype(MX)


def _merge(u, outs, wm, wb, *, tm, tn):
    S, D = u.shape
    nt = D // tn
    o_specs = [pl.BlockSpec((tm, BRANCH_W), lambda i, c: (i, 0)) for _ in range(N_BRANCH)]
    wm_specs = [pl.BlockSpec((D, tn), functools.partial(lambda i, c, j: (0, j * nt + c), j=j))
                for j in range(N_BRANCH)]
    return pl.pallas_call(
        _merge_kernel,
        grid=(S // tm, nt),
        in_specs=[pl.BlockSpec((tm, D), lambda i, c: (i, 0))] + o_specs + wm_specs
                 + [pl.BlockSpec((N_BRANCH, BRANCH_W, tn), lambda i, c: (0, 0, c))],
        out_specs=pl.BlockSpec((tm, tn), lambda i, c: (i, c)),
        out_shape=jax.ShapeDtypeStruct((S, D), MX),
        compiler_params=_params("parallel", "arbitrary"),
        name="merge",
    )(u, *outs, wm, wm, wm, wm, wb)


def _resid_mm_kernel(h_ref, x_ref, w_ref, out_ref):
    out_ref[...] = h_ref[...] + jnp.dot(x_ref[...], w_ref[...], preferred_element_type=F32)


def _resid_mm(h, x, w, *, tm, tn):
    S, D = h.shape
    K = x.shape[1]
    return pl.pallas_call(
        _resid_mm_kernel,
        grid=(S // tm, D // tn),
        in_specs=[pl.BlockSpec((tm, tn), lambda i, c: (i, c)),
                  pl.BlockSpec((tm, K), lambda i, c: (i, 0)),
                  pl.BlockSpec((K, tn), lambda i, c: (0, c))],
        out_specs=pl.BlockSpec((tm, tn), lambda i, c: (i, c)),
        out_shape=jax.ShapeDtypeStruct((S, D), F32),
        compiler_params=_params("parallel", "arbitrary"),
        name="resid_mm",
    )(h, x, w)


def _mlp_kernel(h_ref, g_ref, wu_ref, wd_ref, out_ref, v_ref):
    @pl.when(pl.program_id(1) == 0)
    def _():
        h = h_ref[...]
        v_ref[...] = _rms(h, g_ref[...]).astype(MX)
        out_ref[...] = h

    hid = jnp.maximum(jnp.dot(v_ref[...], wu_ref[...].astype(MX), preferred_element_type=F32), 0.0)
    out_ref[...] += jnp.dot((hid * hid).astype(MX), wd_ref[...].astype(MX),
                            preferred_element_type=F32)


def _mlp(h, g, wu, wd, layer, *, tm, tf):
    S, D = h.shape
    return pl.pallas_call(
        _mlp_kernel,
        grid=(S // tm, wu.shape[2] // tf),
        in_specs=[pl.BlockSpec((tm, D), lambda i, f: (i, 0)),
                  pl.BlockSpec((1, D), lambda i, f: (0, 0)),
                  pl.BlockSpec((None, D, tf), lambda i, f: (layer, 0, f)),
                  pl.BlockSpec((None, tf, D), lambda i, f: (layer, f, 0))],
        out_specs=pl.BlockSpec((tm, D), lambda i, f: (i, 0)),
        out_shape=jax.ShapeDtypeStruct((S, D), F32),
        scratch_shapes=[pltpu.VMEM((tm, D), MX)],
        compiler_params=_params("parallel", "arbitrary"),
        name="mlp",
    )(h, g, wu, wd)


def _ple_kernel(h_ref, p_ref, wg_ref, wp_ref, gf_ref, out_ref, *, final_norm):
    h = h_ref[...]
    gate = _sigmoid(jnp.dot(h.astype(MX), wg_ref[...], preferred_element_type=F32))
    emb = jnp.dot(p_ref[...].astype(MX), wp_ref[...], preferred_element_type=F32)
    y = h + gate * emb
    out_ref[...] = _rms(y, gf_ref[...]) if final_norm else y


def _ple(h, p, wg, wp, gf, *, tm, final_norm):
    S, D = h.shape
    P = p.shape[1]
    return pl.pallas_call(
        functools.partial(_ple_kernel, final_norm=final_norm),
        grid=(S // tm,),
        in_specs=[pl.BlockSpec((tm, D), lambda i: (i, 0)),
                  pl.BlockSpec((tm, P), lambda i: (i, 0)),
                  pl.BlockSpec((D, D), lambda i: (0, 0)),
                  pl.BlockSpec((P, D), lambda i: (0, 0)),
                  pl.BlockSpec((1, D), lambda i: (0, 0))],
        out_specs=pl.BlockSpec((tm, D), lambda i: (i, 0)),
        out_shape=jax.ShapeDtypeStruct((S, D), F32),
        compiler_params=_params("parallel"),
        name="ple",
    )(h, p, wg, wp, gf)


S5_PANELS = BRANCH_W // LANE
S5_PANEL_COLS = S5_COLS // S5_PANELS
T_A0, T_A1, T_A2, T_A4 = range(4)


def _cmul_add(xr, xi, ar, ai, br, bi):
    return xr + (ar * br - ai * bi), xi + (ar * bi + ai * br)


def _s5_kernel(u_ref, bre_ref, bim_ref, tab_ref, cre_ref, cim_ref, d_ref, wglu_ref, out_ref,
               hre_ref, him_ref, carry_ref):
    tm = u_ref.shape[0]

    @pl.when(pl.program_id(0) == 0)
    def _():
        carry_ref[...] = jnp.zeros_like(carry_ref)

    u = u_ref[...]
    ub = u.astype(MX)
    for k in range(S5_PANELS):
        uk = ub[:, k * LANE:(k + 1) * LANE]
        cols = slice(k * S5_PANEL_COLS, (k + 1) * S5_PANEL_COLS)
        hre_ref[:, cols] = jnp.dot(uk, bre_ref[k], preferred_element_type=F32)
        him_ref[:, cols] = jnp.dot(uk, bim_ref[k], preferred_element_type=F32)

    def body(r, carry):
        rows = pl.ds(pl.multiple_of(r * SUBLANE, SUBLANE), SUBLANE)
        for cb in range(S5_COLS // LANE):
            cols = slice(cb * LANE, (cb + 1) * LANE)
            xr, xi = _cmul_add(hre_ref[rows, cols], him_ref[rows, cols],
                               tab_ref[T_A0, 0, :, cols], tab_ref[T_A0, 1, :, cols],
                               carry_ref[0, :, cols], carry_ref[1, :, cols])
            for k, sh in ((T_A1, 1), (T_A2, 2), (T_A4, 4)):
                xr, xi = _cmul_add(xr, xi, tab_ref[k, 0, :, cols], tab_ref[k, 1, :, cols],
                                   pltpu.roll(xr, sh, 0), pltpu.roll(xi, sh, 0))
            hre_ref[rows, cols] = xr
            him_ref[rows, cols] = xi
            carry_ref[0, :, cols] = jnp.broadcast_to(xr[SUBLANE - 1:SUBLANE, :], xr.shape)
            carry_ref[1, :, cols] = jnp.broadcast_to(xi[SUBLANE - 1:SUBLANE, :], xi.shape)
        return carry

    lax.fori_loop(0, tm // SUBLANE, body, 0)

    ys = []
    for k in range(S5_PANELS):
        cols = slice(k * S5_PANEL_COLS, (k + 1) * S5_PANEL_COLS)
        ys.append(jnp.dot(hre_ref[:, cols].astype(MX), cre_ref[k], preferred_element_type=F32)
                  + jnp.dot(him_ref[:, cols].astype(MX), cim_ref[k], preferred_element_type=F32))
    y = jnp.concatenate(ys, axis=1) + d_ref[...] * u
    y = 0.5 * y * (1.0 + jnp.tanh(math.sqrt(2.0 / math.pi) * (y + 0.044715 * (y * y * y))))
    gl = jnp.dot(y.astype(MX), wglu_ref[...], preferred_element_type=F32)
    out_ref[...] = (gl[:, :BRANCH_W] * _sigmoid(gl[:, BRANCH_W:])).astype(MX)


def _s5_tables(a_re, a_im, log_dt, b_re, b_im, c_re, c_im):
    dt = jnp.exp(log_dt)[:, None]
    xr, xi = a_re * dt, a_im * dt

    pr = jnp.exp(xr) * jnp.cos(xi)
    pi = jnp.exp(xr) * jnp.sin(xi)
    den = a_re * a_re + a_im * a_im
    qr = ((pr - 1.0) * a_re + pi * a_im) / den
    qi = (pi * a_re - (pr - 1.0) * a_im) / den
    bbar_re = qr[..., None] * b_re - qi[..., None] * b_im
    bbar_im = qr[..., None] * b_im + qi[..., None] * b_re

    flat = lambda z: z.reshape(1, S5_COLS)
    powers = [(flat(pr), flat(pi))]
    for _ in range(3):
        zr, zi = powers[-1]
        powers.append((zr * flat(pr) - zi * flat(pi), zr * flat(pi) + zi * flat(pr)))
    row = jnp.arange(SUBLANE)[:, None]
    planes = [(jnp.where(row == 0, flat(pr), 0.0), jnp.where(row == 0, flat(pi), 0.0))]
    for sh in (1, 2, 4):
        zr, zi = powers[sh - 1]
        planes.append((jnp.where(row >= sh, zr, 0.0), jnp.where(row >= sh, zi, 0.0)))
    tab = jnp.stack([jnp.stack(p) for p in planes]).astype(F32)

    pg = S5_GROUPS // S5_PANELS
    eye = jnp.eye(pg, dtype=F32)

    def bdiag_in(m):
        m = m.reshape(S5_PANELS, pg, S5_STATE, S5_GROUP)
        return jnp.einsum('pgnc,gh->pgchn', m, eye).reshape(S5_PANELS, LANE, S5_PANEL_COLS)

    def bdiag_out(m):
        m = m.reshape(S5_PANELS, pg, S5_GROUP, S5_STATE)
        return jnp.einsum('pgcn,gh->pgnhc', m, eye).reshape(S5_PANELS, S5_PANEL_COLS, LANE)

    return (tab, bdiag_in(bbar_re).astype(MX), bdiag_in(bbar_im).astype(MX),
            bdiag_out(c_re).astype(MX), bdiag_out(-c_im).astype(MX))


def _s5_mixer(za, tabs, d_skip, w_glu, *, tm):
    S = za.shape[0]
    tab, bre, bim, cre, cim = tabs
    full = lambda shape: pl.BlockSpec(shape, lambda i: (0,) * len(shape))
    return pl.pallas_call(
        _s5_kernel,
        grid=(S // tm,),
        in_specs=[pl.BlockSpec((tm, BRANCH_W), lambda i: (i, ZA_S5)),
                  full(bre.shape), full(bim.shape), full(tab.shape), full(cre.shape),
                  full(cim.shape), full((1, BRANCH_W)), full(w_glu.shape)],
        out_specs=pl.BlockSpec((tm, BRANCH_W), lambda i: (i, 0)),
        out_shape=jax.ShapeDtypeStruct((S, BRANCH_W), MX),
        scratch_shapes=[pltpu.VMEM((tm, S5_COLS), F32), pltpu.VMEM((tm, S5_COLS), F32),
                        pltpu.VMEM((2, SUBLANE, S5_COLS), F32)],
        compiler_params=_params("arbitrary"),
        name="s5",
    )(za, bre, bim, tab, cre, cim, d_skip.reshape(1, BRANCH_W), w_glu)


def _pool_kernel(x_ref, w_ref, scale_ref, out_ref, ext_ref):
    tm = x_ref.shape[0]
    i = pl.program_id(0)

    @pl.when(i == 0)
    def _():
        ext_ref[0:POOL_HALO, :] = jnp.zeros((POOL_HALO, BRANCH_W), F32)

    x = x_ref[...]
    ext_ref[POOL_HALO:, :] = x
    t = i * tm + lax.broadcasted_iota(jnp.int32, (tm, 1), 0)
    for gi, w in enumerate(POOL_WINDOWS):
        cols = slice(gi * POOL_GROUP, (gi + 1) * POOL_GROUP)
        acc = x[:, cols]
        for d in range(1, w):
            acc = acc + ext_ref[POOL_HALO - d:POOL_HALO - d + tm, cols]
        count = jnp.minimum(t + 1, w).astype(F32)
        diff = (acc / count - x[:, cols]).astype(MX)
        y = jnp.dot(diff, w_ref[gi], preferred_element_type=F32)
        out_ref[:, cols] = (y * scale_ref[:, cols]).astype(MX)
    ext_ref[0:POOL_HALO, :] = x[tm - POOL_HALO:, :]


def _pool_mixer(za, w_pool, pool_scale, *, tm):
    S = za.shape[0]
    return pl.pallas_call(
        _pool_kernel,
        grid=(S // tm,),
        in_specs=[pl.BlockSpec((tm, BRANCH_W), lambda i: (i, ZA_POOL)),
                  pl.BlockSpec(w_pool.shape, lambda i: (0, 0, 0)),
                  pl.BlockSpec((1, BRANCH_W), lambda i: (0, 0))],
        out_specs=pl.BlockSpec((tm, BRANCH_W), lambda i: (i, 0)),
        out_shape=jax.ShapeDtypeStruct((S, BRANCH_W), MX),
        scratch_shapes=[pltpu.VMEM((tm + POOL_HALO, BRANCH_W), F32)],
        compiler_params=_params("arbitrary"),
        name="pool",
    )(za, w_pool, pool_scale.reshape(1, BRANCH_W))


def _rotary_tables(S):
    dh, C = RET_HEAD_DIM, RET_CHUNK
    angle = np.repeat(1.0 / (10000.0 ** np.linspace(0.0, 1.0, dh // 2)), 2)[None, :]
    within = np.arange(C, dtype=np.float64)[:, None] * angle
    start = (np.arange(S // C, dtype=np.float64) * C)[:, None] * angle
    tab = lambda a: jnp.asarray(a, dtype=F32)
    return (tab(np.cos(within)), tab(np.sin(within)),
            tab(np.cos(start)[:, None, :]), tab(np.sin(start)[:, None, :]))


def _ret_consts():
    H, dh, C = RET_HEADS, RET_HEAD_DIM, RET_CHUNK
    log_g = jnp.log(1.0 - 2.0 ** (-5.0 - jnp.arange(H, dtype=F32)))
    idx = jnp.arange(C, dtype=F32)
    rel = idx[:, None] - idx[None, :]
    decay = jnp.where(rel >= 0, jnp.exp(jnp.maximum(rel, 0.0)[None] * log_g[:, None, None]), 0.0)
    xi = jnp.exp((idx + 1.0)[None, :] * log_g[:, None])[..., None]
    zeta = jnp.exp((C - 1.0 - idx)[None, :] * log_g[:, None])[..., None]
    g_chunk = jnp.broadcast_to(jnp.exp(C * log_g)[:, None, None], (H, 1, dh))
    return decay, xi, zeta, g_chunk


def _rotate_pairs(x, even):
    n = x.shape[-1]
    return jnp.where(even, -pltpu.roll(x, n - 1, 1), pltpu.roll(x, 1, 1))


def _ret_kernel(q_ref, k_ref, v_ref, g_ref, ca_ref, sa_ref, cb_ref, sb_ref, decay_ref, xi_ref,
                zeta_ref, gch_ref, out_ref, state_ref):
    C, dh = RET_CHUNK, RET_HEAD_DIM

    @pl.when(pl.program_id(0) == 0)
    def _():
        state_ref[...] = jnp.zeros_like(state_ref)

    ca, sa, cb, sb = ca_ref[...], sa_ref[...], cb_ref[0], sb_ref[0]
    cos = ca * cb - sa * sb
    sin = sa * cb + ca * sb
    even = (lax.broadcasted_iota(jnp.int32, (C, dh), 1) % 2) == 0
    for hd in range(RET_HEADS):
        cols = slice(hd * dh, (hd + 1) * dh)
        q = q_ref[:, cols]
        k = k_ref[:, cols]
        vb = v_ref[:, cols].astype(MX)
        q = q * cos + _rotate_pairs(q, even) * sin
        k = (k * cos + _rotate_pairs(k, even) * sin) * dh ** -0.5
        qb = q.astype(MX)
        s = lax.dot_general(qb, k.astype(MX), (((1,), (1,)), ((), ())),
                            preferred_element_type=F32) * decay_ref[hd]
        inner = jnp.dot(s.astype(MX), vb, preferred_element_type=F32)
        state = state_ref[hd]
        cross = jnp.dot(qb, state.astype(MX), preferred_element_type=F32) * xi_ref[hd]
        kz = (k * zeta_ref[hd]).astype(MX)
        state_ref[hd] = state * gch_ref[hd] + lax.dot_general(
            kz, vb, (((0,), (0,)), ((), ())), preferred_element_type=F32)
        y = inner + cross
        y = y * lax.rsqrt(jnp.mean(y * y, axis=-1, keepdims=True) + NORM_EPS)
        g = g_ref[:, cols]
        out_ref[:, cols] = (g * _sigmoid(g) * y).astype(MX)


def _ret_mixer(za):
    S = za.shape[0]
    C = RET_CHUNK
    ca, sa, cb, sb = _rotary_tables(S)
    consts = (ca, sa, cb, sb) + _ret_consts()
    col = lambda c: pl.BlockSpec((C, BRANCH_W), lambda i: (i, c))
    full = lambda a: pl.BlockSpec(a.shape, lambda i: (0,) * a.ndim)
    start = lambda a: pl.BlockSpec((1,) + a.shape[1:], lambda i: (i, 0, 0))
    return pl.pallas_call(
        _ret_kernel,
        grid=(S // C,),
        in_specs=[col(ZA_RQ), col(ZA_RK), col(ZA_RV), col(ZA_RG), full(ca), full(sa), start(cb),
                  start(sb)] + [full(a) for a in consts[4:]],
        out_specs=pl.BlockSpec((C, BRANCH_W), lambda i: (i, 0)),
        out_shape=jax.ShapeDtypeStruct((S, BRANCH_W), MX),
        scratch_shapes=[pltpu.VMEM((RET_HEADS, RET_HEAD_DIM, RET_HEAD_DIM), F32)],
        compiler_params=_params("arbitrary"),
        name="retention",
    )(za, za, za, za, *consts)


NSA_R = NSA_HEADS // NSA_KV_GROUPS
NSA_TQ = 128
SLC_TQ = 512
WIN_TQ = 256
NSA_TK = 512
CMP_CH = 256
P_OFF = SUBLANE
NT_DIMS = (((1,), (1,)), ((), ()))


def _compress_kernel(x_ref, w_ref, out_ref):
    out_ref[0] = jnp.dot(x_ref[0], w_ref[...], preferred_element_type=F32).astype(out_ref.dtype)


def _compress_t_kernel(x_ref, wt_ref, out_ref):
    out_ref[0] = lax.dot_general(wt_ref[...], x_ref[0], NT_DIMS,
                                 preferred_element_type=F32).astype(out_ref.dtype)


def _compress(x, w, *, transposed):
    G, nc, kk = x.shape
    if transposed:
        kern, out_block = _compress_t_kernel, (1, w.shape[0], nc)
    else:
        kern, out_block = _compress_kernel, (1, nc, w.shape[1])
    return pl.pallas_call(
        kern,
        grid=(G,),
        in_specs=[pl.BlockSpec((1, nc, kk), lambda g: (g, 0, 0)),
                  pl.BlockSpec(w.shape, lambda g: (0, 0))],
        out_specs=pl.BlockSpec(out_block, lambda g: (g, 0, 0)),
        out_shape=jax.ShapeDtypeStruct((G,) + out_block[1:], x.dtype),
        compiler_params=_params("arbitrary"),
        name="nsa_compress",
    )(x, w)


def _group_queries(q_ref, g, extra_scale=1.0):
    hd = NSA_HEAD_DIM
    base = g * NSA_R * hd
    qs = [q_ref[:, base + r * hd:base + (r + 1) * hd] for r in range(NSA_R)]
    return (jnp.concatenate(qs, axis=0) * (hd ** -0.5 * extra_scale)).astype(MX)


def _cmp_select_kernel(q_ref, kc_ref, vct_ref, ocmp_ref, sel_ref, s_sc, p_sc, ot_sc):
    i = pl.program_id(0)
    tq, hd = NSA_TQ, NSA_HEAD_DIM
    nc = kc_ref.shape[1]
    ns = sel_ref.shape[2]
    ch = min(CMP_CH, nc)
    nch = jnp.minimum((tq // CMP_STRIDE * (i + 1) + ch - 1) // ch, nc // ch)
    t4 = i * tq + lax.broadcasted_iota(jnp.int32, (1, NSA_R * tq), 1) % tq
    t1 = i * tq + lax.broadcasted_iota(jnp.int32, (1, tq), 1)
    nrow = lax.broadcasted_iota(jnp.int32, (ch, 1), 0)

    def rows(c):
        return pl.ds(pl.multiple_of(c * ch, ch), ch)

    def visible(c):
        return ((c * ch + nrow) * CMP_STRIDE + (CMP_LEN - 1)) <= t4

    groups = range(NSA_KV_GROUPS)
    qgs = [_group_queries(q_ref, g) for g in groups]
    lanes = (1, NSA_R * tq)

    def scores(c, ms):
        vis = visible(c)
        out = []
        for g in groups:
            s = lax.dot_general(kc_ref[g, rows(c), :], qgs[g], NT_DIMS, preferred_element_type=F32)
            s = jnp.where(vis, s, NEG_INF)
            s_sc[g, rows(c), :] = s
            out.append(jnp.maximum(ms[g], jnp.max(s, axis=0, keepdims=True)))
        return tuple(out)

    ms = lax.fori_loop(0, nch, scores, tuple(jnp.full(lanes, NEG_INF, F32) for _ in groups))

    def expsum(c, ls):
        vis = visible(c)
        out = []
        for g in groups:
            e = jnp.where(vis, jnp.exp(s_sc[g, rows(c), :] - ms[g]), 0.0)
            s_sc[g, rows(c), :] = e
            out.append(ls[g] + jnp.sum(e, axis=0, keepdims=True))
        return tuple(out)

    ls = lax.fori_loop(0, nch, expsum, tuple(jnp.zeros(lanes, F32) for _ in groups))
    invs = [1.0 / jnp.maximum(l, 1e-30) for l in ls]
    p_sc[...] = jnp.zeros_like(p_sc)
    ot_sc[...] = jnp.zeros_like(ot_sc)

    def weigh(c, carry):
        for g in groups:
            p = s_sc[g, rows(c), :] * invs[g]
            p_sc[g, pl.ds(pl.multiple_of(P_OFF + c * ch, SUBLANE), ch), :] = (
                p[:, 0:tq] + p[:, tq:2 * tq] + p[:, 2 * tq:3 * tq] + p[:, 3 * tq:4 * tq])
            ot_sc[g] += jnp.dot(vct_ref[g, :, rows(c)], p.astype(MX), preferred_element_type=F32)
        return carry

    lax.fori_loop(0, nch, weigh, 0)
    for h in range(NSA_HEADS):
        g, r = divmod(h, NSA_R)
        o = ot_sc[g, :, r * tq:(r + 1) * tq].T
        ocmp_ref[:, h * hd:(h + 1) * hd] = o[:, :hd]

    ratio = SLC_BLOCK // CMP_STRIDE

    def pick(nr):
        jrow = lax.broadcasted_iota(jnp.int32, (nr, 1), 0)
        jfull = jnp.broadcast_to(jrow.astype(F32), (nr, tq))
        valid = jrow * SLC_BLOCK <= t1
        bonus = jnp.where((jrow == 0) | (jrow == t1 // SLC_BLOCK), FORCE_BONUS, 0.0)
        xs = []
        for g in groups:
            imp = p_sc[g, pl.ds(P_OFF - 1, nr, stride=ratio), :]
            for k in range(ratio):
                imp = imp + p_sc[g, pl.ds(P_OFF + k, nr, stride=ratio), :]
            xs.append(jnp.where(valid, imp + bonus, NEG_INF))
        for _ in range(min(SLC_TOP_N, ns)):
            for g in groups:
                mx = jnp.max(xs[g], axis=0, keepdims=True)
                cand = jnp.where(xs[g] == mx, jfull, float(nr))
                xs[g] = jnp.where(cand == jnp.min(cand, axis=0, keepdims=True), -jnp.inf, xs[g])
        for g in groups:
            picked = jnp.where(xs[g] == -jnp.inf, 1.0, 0.0).T.astype(BF16)
            if nr < ns:
                picked = jnp.concatenate([picked, jnp.zeros((tq, ns - nr), BF16)], axis=1)
            sel_ref[g] = picked

    half = ns // 2
    if half >= SLC_TOP_N:
        early = (i + 1) * (tq // SLC_BLOCK) <= half

        @pl.when(early)
        def _():
            pick(half)

        @pl.when(jnp.logical_not(early))
        def _():
            pick(ns)
    else:
        pick(ns)


def _cmp_select(za, kc, vct):
    S = za.shape[0]
    G, nc, hd = kc.shape
    ns = S // SLC_BLOCK
    tq = NSA_TQ
    return pl.pallas_call(
        _cmp_select_kernel,
        grid=(S // tq,),
        in_specs=[pl.BlockSpec((tq, BRANCH_W), lambda i: (i, ZA_Q)),
                  pl.BlockSpec(kc.shape, lambda i: (0, 0, 0)),
                  pl.BlockSpec(vct.shape, lambda i: (0, 0, 0))],
        out_specs=[pl.BlockSpec((tq, BRANCH_W), lambda i: (i, 0)),
                   pl.BlockSpec((G, tq, ns), lambda i: (0, i, 0))],
        out_shape=[jax.ShapeDtypeStruct((S, BRANCH_W), F32),
                   jax.ShapeDtypeStruct((G, S, ns), BF16)],
        scratch_shapes=[pltpu.VMEM((G, nc, NSA_R * tq), F32), pltpu.VMEM((G, P_OFF + nc, tq), F32),
                        pltpu.VMEM((G, LANE, NSA_R * tq), F32)],
        compiler_params=_params("parallel"),
        name="nsa_cmp_select",
    )(za, kc, vct)


SLC_MASK = 2.0 ** 100
LOG2E = math.log2(math.e)


def _slc_kernel(q_ref, kx_ref, vs_ref, sel_ref, out_ref, lhs_sc, m_sc, acc_sc, *, tk):
    i = pl.program_id(1)
    tq, hd = SLC_TQ, NSA_HEAD_DIM
    ksel = kx_ref.shape[2] - LANE
    bpt = tk // SLC_BLOCK
    qg = _group_queries(q_ref, 0, LOG2E)
    pad = jnp.zeros((NSA_R * tq, LANE - hd), MX)
    for h in range(sel_ref.shape[2] // ksel):
        unsel = (1.0 - sel_ref[0, :, h * ksel:(h + 1) * ksel].astype(F32)).astype(MX)
        lhs_sc[h] = jnp.concatenate([qg, pad, jnp.concatenate([unsel] * NSA_R, axis=0)], axis=1)
    t = i * tq + lax.broadcasted_iota(jnp.int32, (tq, 1), 0)
    lane = lax.broadcasted_iota(jnp.int32, (1, tk), 1)
    m_sc[...] = jnp.full_like(m_sc, NEG_INF)
    acc_sc[...] = jnp.zeros_like(acc_sc)

    def tile(kt, diagonal):
        k0 = pl.multiple_of(kt * tk, tk)
        s = lax.dot_general(lhs_sc[(kt * bpt) // ksel], kx_ref[0, pl.ds(k0, tk), :], NT_DIMS,
                            preferred_element_type=F32)
        if diagonal:
            hidden = jnp.where(k0 + lane <= t, 0.0, NEG_INF)
            s = s + jnp.concatenate([hidden] * NSA_R, axis=0)
        m_prev = m_sc[...]
        m_new = jnp.maximum(m_prev, jnp.max(s, axis=-1, keepdims=True))
        p = jnp.exp2(s - jnp.concatenate([m_new] * (tk // LANE), axis=1))
        acc_sc[...] = jnp.exp2(m_prev - m_new) * acc_sc[...] + jnp.dot(
            p.astype(MX), vs_ref[0, pl.ds(k0, tk), :], preferred_element_type=F32)
        m_sc[...] = m_new

    last = (i * tq) // tk

    def quad(j, carry):
        for u in range(4):
            tile(4 * j + u, False)
        return carry

    lax.fori_loop(0, last // 4, quad, 0)
    done = (last // 4) * 4

    @pl.when(last - done >= 2)
    def _():
        tile(done, False)
        tile(done + 1, False)

    @pl.when((last - done) % 2 == 1)
    def _():
        tile(last - 1, False)

    tile(last, True)
    acc = acc_sc[...]
    o = acc[:, :hd] / acc[:, hd:]
    for r in range(NSA_R):
        out_ref[:, r * hd:(r + 1) * hd] = o[r * tq:(r + 1) * tq, :]


def _slc_attention(za, ks, vs, sel):
    S = za.shape[0]
    G = ks.shape[0]
    ns = sel.shape[2]
    tq, hd = SLC_TQ, NSA_HEAD_DIM
    tk = min(NSA_TK, S)
    ksel = min(LANE, ns)
    own_block = (jnp.arange(S)[:, None] // SLC_BLOCK) % ksel == jnp.arange(ksel)[None, :]
    kx = jnp.concatenate([ks, jnp.zeros((G, S, LANE - hd), ks.dtype),
                          jnp.broadcast_to(jnp.where(own_block, -SLC_MASK, 0.0).astype(ks.dtype),
                                           (G, S, ksel))], axis=2)
    gw = NSA_R * hd
    return pl.pallas_call(
        functools.partial(_slc_kernel, tk=tk),
        grid=(G, S // tq),
        in_specs=[pl.BlockSpec((tq, gw), lambda g, i: (i, g)),
                  pl.BlockSpec((1, S, LANE + ksel), lambda g, i: (g, 0, 0)),
                  pl.BlockSpec((1, S, LANE), lambda g, i: (g, 0, 0)),
                  pl.BlockSpec((1, tq, ns), lambda g, i: (g, i, 0))],
        out_specs=pl.BlockSpec((tq, gw), lambda g, i: (i, g)),
        out_shape=jax.ShapeDtypeStruct((S, BRANCH_W), F32),
        scratch_shapes=[pltpu.VMEM((ns // ksel, NSA_R * tq, LANE + ksel), MX),
                        pltpu.VMEM((NSA_R * tq, LANE), F32), pltpu.VMEM((NSA_R * tq, LANE), F32)],
        compiler_params=_params("arbitrary", "arbitrary"),
        name="nsa_selected",
    )(za, kx, vs, sel)


def _win_kernel(q_ref, kw_ref, vw_ref, gate_ref, ocmp_ref, oslc_ref, out_ref):
    i = pl.program_id(1)
    tq, hd = WIN_TQ, NSA_HEAD_DIM
    span = tq + WIN
    qg = _group_queries(q_ref, 0, LOG2E)
    r0 = pl.multiple_of(i * tq, tq)
    t = i * tq + lax.broadcasted_iota(jnp.int32, (tq, 1), 0)
    pos = i * tq - WIN + lax.broadcasted_iota(jnp.int32, (1, span), 1)
    ok = (pos <= t) & (pos > t - WIN) & (pos >= 0)
    s = lax.dot_general(qg, kw_ref[0, pl.ds(r0, span), :], NT_DIMS, preferred_element_type=F32)
    s = s + jnp.concatenate([jnp.where(ok, 0.0, NEG_INF)] * NSA_R, axis=0)
    e = jnp.exp2(s - jnp.max(s, axis=-1, keepdims=True))
    acc = jnp.dot(e.astype(MX), vw_ref[0, pl.ds(r0, span), :], preferred_element_type=F32)
    ow = acc[:, :hd] / acc[:, hd:]
    sig = _sigmoid(gate_ref[...])
    for r in range(NSA_R):
        cols = slice(r * hd, (r + 1) * hd)
        out_ref[:, cols] = (sig[:, r:r + 1] * ocmp_ref[:, cols]
                            + sig[:, NSA_R + r:NSA_R + r + 1] * oslc_ref[:, cols]
                            + sig[:, 2 * NSA_R + r:2 * NSA_R + r + 1] * ow[r * tq:(r + 1) * tq, :]
                            ).astype(MX)


def _win_combine(za, kw, vw, ocmp, oslc):
    S = za.shape[0]
    G = kw.shape[0]
    tq, hd = WIN_TQ, NSA_HEAD_DIM
    gw = NSA_R * hd
    blk = lambda: pl.BlockSpec((tq, gw), lambda g, i: (i, g))
    return pl.pallas_call(
        _win_kernel,
        grid=(G, S // tq),
        in_specs=[blk(),
                  pl.BlockSpec((1, S + WIN, hd), lambda g, i: (g, 0, 0)),
                  pl.BlockSpec((1, S + WIN, LANE), lambda g, i: (g, 0, 0)),
                  pl.BlockSpec((tq, GATE_PAD), lambda g, i: (i, ZA_G_OFF // GATE_PAD + g)),
                  blk(), blk()],
        out_specs=blk(),
        out_shape=jax.ShapeDtypeStruct((S, BRANCH_W), MX),
        compiler_params=_params("arbitrary", "arbitrary"),
        name="nsa_window_combine",
    )(za, kw, vw, za, ocmp, oslc)


def _nsa_mixer(za, w_cmp_k, w_cmp_v):
    S = za.shape[0]
    G, hd = NSA_KV_GROUPS, NSA_HEAD_DIM
    nc = S // CMP_STRIDE

    def kv(idx):
        off = ZA_KV_OFF + idx * G * hd
        return jnp.stack([za[:, off + g * hd:off + (g + 1) * hd] for g in range(G)])

    def blocks(x):
        half = x.reshape(G, nc, CMP_STRIDE * hd)
        nxt = jnp.concatenate([half[:, 1:], jnp.zeros_like(half[:, :1])], axis=1)
        return jnp.concatenate([half, nxt], axis=2).astype(MX)

    def with_ones(x):
        return jnp.concatenate([x, jnp.ones((G, S, LANE - hd), x.dtype)], axis=2).astype(MX)

    front = lambda x: jnp.pad(x, ((0, 0), (WIN, 0), (0, 0)))
    wk = w_cmp_k.reshape(CMP_LEN * hd, hd).astype(MX)
    wvt = jnp.pad(w_cmp_v.reshape(CMP_LEN * hd, hd).T, ((0, LANE - hd), (0, 0))).astype(MX)
    kc = _compress(blocks(kv(0)), wk, transposed=False)
    vct = _compress(blocks(kv(1)), wvt, transposed=True)
    ocmp, sel = _cmp_select(za, kc, vct)
    oslc = _slc_attention(za, kv(2).astype(MX), with_ones(kv(3)), sel)
    return _win_combine(za, front(kv(4).astype(MX)), front(with_ones(kv(5))), ocmp, oslc)


def _pack_w_in(w):
    seg = lambda off, n: w[:, off:off + n]
    R = NSA_HEADS // NSA_KV_GROUPS
    gates = []
    for g in range(NSA_KV_GROUPS):
        cols = [OFF_G + br * NSA_HEADS + g * R + r for br in range(3) for r in range(R)]
        gates.append(jnp.pad(w[:, jnp.array(cols)], ((0, 0), (0, GATE_PAD - len(cols)))))
    wa = jnp.concatenate([seg(OFF_Q, BRANCH_W), seg(OFF_S5, BRANCH_W), seg(OFF_POOL, BRANCH_W),
                          seg(OFF_RQ, BRANCH_W), seg(OFF_RK, BRANCH_W), seg(OFF_RV, BRANCH_W),
                          seg(OFF_RG, BRANCH_W), seg(OFF_KV, NSA_KV_COLS)] + gates, axis=1)
    return wa.astype(MX), w[:, OFF_MERGE:].astype(MX)


def kernel(x, p, g_mix, w_in, w_cmp_k, w_cmp_v, s5_a_re, s5_a_im, s5_log_dt, s5_b_re, s5_b_im,
           s5_c_re, s5_c_im, s5_d, s5_w_glu, pool_w, pool_scale, w_branch, w_out, g_mlp,
           w_mlp_up, w_mlp_down, w_ple_gate, w_ple_proj, g_final):
    B, S, D = x.shape
    assert B == 1 and D == D_MODEL
    depth = w_in.shape[0]
    h = x.reshape(S, D)
    w_in_mx = lax.optimization_barrier(w_in.astype(MX))
    for i in range(depth):
        wa, wm = _pack_w_in(w_in_mx[i])
        za, u = _inproj(h, g_mix[i].reshape(1, D), wa, tm=1024, tn=768)

        o_nsa = _nsa_mixer(za, w_cmp_k[i], w_cmp_v[i])
        s5_tabs = _s5_tables(s5_a_re[i], s5_a_im[i], s5_log_dt[i], s5_b_re[i], s5_b_im[i],
                             s5_c_re[i], s5_c_im[i])
        o_s5 = _s5_mixer(za, s5_tabs, s5_d[i], s5_w_glu[i].astype(MX), tm=512)
        o_pool = _pool_mixer(za, pool_w[i].astype(MX), pool_scale[i], tm=1024)
        o_ret = _ret_mixer(za)

        merged = _merge(u, (o_nsa, o_s5, o_pool, o_ret), wm, w_branch[i].astype(MX),
                        tm=1024, tn=256)
        h = _resid_mm(h, merged, w_out[i].astype(MX), tm=1024, tn=1024)
        h = _mlp(h, g_mlp[i].reshape(1, D), w_mlp_up, w_mlp_down, i, tm=1024, tf=512)
        h = _ple(h, p[i].reshape(S, PLE_DIM), w_ple_gate[i].astype(MX),
                 w_ple_proj[i].astype(MX), g_final.reshape(1, D), tm=512,
                 final_norm=(i == depth - 1))
    return h.reshape(B, S, D)
```

```python
import functools
import math

import jax
import jax.numpy as jnp
import numpy as np
from jax import lax
from jax.experimental import pallas as pl
from jax.experimental.pallas import tpu as pltpu

F32 = jnp.float32
BF16 = jnp.bfloat16
MX = BF16

D_MODEL = 2048
N_BRANCH = 4
BRANCH_W = D_MODEL // N_BRANCH
NSA_HEADS = 8
NSA_KV_GROUPS = 2
NSA_HEAD_DIM = BRANCH_W // NSA_HEADS
CMP_LEN = 32
CMP_STRIDE = 16
SLC_BLOCK = 64
SLC_TOP_N = 16
WIN = 512
Q_BLOCK = 128
S5_GROUP = 16
S5_GROUPS = BRANCH_W // S5_GROUP
S5_STATE = 64
S5_COLS = S5_GROUPS * S5_STATE
POOL_WINDOWS = (2, 4, 8, 16)
POOL_GROUP = BRANCH_W // len(POOL_WINDOWS)
POOL_HALO = 16
RET_HEADS = 4
RET_HEAD_DIM = BRANCH_W // RET_HEADS
RET_CHUNK = 128
D_FF = 4 * D_MODEL
PLE_DIM = 256
NORM_EPS = 1e-6
NEG_INF = -1e30
FORCE_BONUS = 1e4

NSA_KV_COLS = 6 * NSA_KV_GROUPS * NSA_HEAD_DIM
NSA_GATE_COLS = 3 * NSA_HEADS
OFF_Q = 0
OFF_KV = OFF_Q + BRANCH_W
OFF_G = OFF_KV + NSA_KV_COLS
OFF_S5 = OFF_G + NSA_GATE_COLS
OFF_POOL = OFF_S5 + BRANCH_W
OFF_RQ = OFF_POOL + BRANCH_W
OFF_RK = OFF_RQ + BRANCH_W
OFF_RV = OFF_RK + BRANCH_W
OFF_RG = OFF_RV + BRANCH_W
OFF_MERGE = OFF_RG + BRANCH_W

LANE = 128
SUBLANE = 8
GATE_PAD = LANE
ZA_Q, ZA_S5, ZA_POOL, ZA_RQ, ZA_RK, ZA_RV, ZA_RG = range(7)
ZA_KV_OFF = 7 * BRANCH_W
ZA_G_OFF = ZA_KV_OFF + NSA_KV_COLS
ZA_COLS = ZA_G_OFF + NSA_KV_GROUPS * GATE_PAD

VMEM_LIMIT = 56 * 1024 * 1024


def _params(*sem):
    return pltpu.CompilerParams(dimension_semantics=sem, vmem_limit_bytes=VMEM_LIMIT)


def _rms(x, g):
    return x * lax.rsqrt(jnp.mean(x * x, axis=-1, keepdims=True) + NORM_EPS) * g


def _sigmoid(x):
    return 1.0 / (1.0 + jnp.exp(-x))


def _inproj_kernel(h_ref, g_ref, w_ref, z_ref, u_ref):
    @pl.when(pl.program_id(1) == 0)
    def _():
        u_ref[...] = _rms(h_ref[...], g_ref[...]).astype(MX)

    z_ref[...] = jnp.dot(u_ref[...], w_ref[...], preferred_element_type=F32)


def _inproj(h, g, w, *, tm, tn):
    S, D = h.shape
    N = w.shape[1]
    return pl.pallas_call(
        _inproj_kernel,
        grid=(S // tm, N // tn),
        in_specs=[pl.BlockSpec((tm, D), lambda i, j: (i, 0)),
                  pl.BlockSpec((1, D), lambda i, j: (0, 0)),
                  pl.BlockSpec((D, tn), lambda i, j: (0, j))],
        out_specs=[pl.BlockSpec((tm, tn), lambda i, j: (i, j)),
                   pl.BlockSpec((tm, D), lambda i, j: (i, 0))],
        out_shape=[jax.ShapeDtypeStruct((S, N), F32), jax.ShapeDtypeStruct((S, D), MX)],
        compiler_params=_params("parallel", "arbitrary"),
        name="inproj",
    )(h, g, w)


def _merge_kernel(u_ref, o0_ref, o1_ref, o2_ref, o3_ref, wm0_ref, wm1_ref, wm2_ref, wm3_ref,
                  wb_ref, out_ref):
    u = u_ref[...]
    acc = None
    for j, (o_ref, wm_ref) in enumerate(((o0_ref, wm0_ref), (o1_ref, wm1_ref),
                                         (o2_ref, wm2_ref), (o3_ref, wm3_ref))):
        gate = _sigmoid(jnp.dot(u, wm_ref[...], preferred_element_type=F32))
        br = jnp.dot(o_ref[...], wb_ref[j], preferred_element_type=F32)
        acc = gate * br if acc is None else acc + gate * br
    out_ref[...] = acc.astype(MX)


def _merge(u, outs, wm, wb, *, tm, tn):
    S, D = u.shape
    nt = D // tn
    o_specs = [pl.BlockSpec((tm, BRANCH_W), lambda i, c: (i, 0)) for _ in range(N_BRANCH)]
    wm_specs = [pl.BlockSpec((D, tn), functools.partial(lambda i, c, j: (0, j * nt + c), j=j))
                for j in range(N_BRANCH)]
    return pl.pallas_call(
        _merge_kernel,
        grid=(S // tm, nt),
        in_specs=[pl.BlockSpec((tm, D), lambda i, c: (i, 0))] + o_specs + wm_specs
                 + [pl.BlockSpec((N_BRANCH, BRANCH_W, tn), lambda i, c: (0, 0, c))],
        out_specs=pl.BlockSpec((tm, tn), lambda i, c: (i, c)),
        out_shape=jax.ShapeDtypeStruct((S, D), MX),
        compiler_params=_params("parallel", "arbitrary"),
        name="merge",
    )(u, *outs, wm, wm, wm, wm, wb)


def _resid_mm_kernel(h_ref, x_ref, w_ref, out_ref):
    out_ref[...] = h_ref[...] + jnp.dot(x_ref[...], w_ref[...], preferred_element_type=F32)


def _resid_mm(h, x, w, *, tm, tn):
    S, D = h.shape
    K = x.shape[1]
    return pl.pallas_call(
        _resid_mm_kernel,
        grid=(S // tm, D // tn),
        in_specs=[pl.BlockSpec((tm, tn), lambda i, c: (i, c)),
                  pl.BlockSpec((tm, K), lambda i, c: (i, 0)),
                  pl.BlockSpec((K, tn), lambda i, c: (0, c))],
        out_specs=pl.BlockSpec((tm, tn), lambda i, c: (i, c)),
        out_shape=jax.ShapeDtypeStruct((S, D), F32),
        compiler_params=_params("parallel", "arbitrary"),
        name="resid_mm",
    )(h, x, w)


def _mlp_kernel(h_ref, g_ref, wu_ref, wd_ref, out_ref, v_ref):
    @pl.when(pl.program_id(1) == 0)
    def _():
        h = h_ref[...]
        v_ref[...] = _rms(h, g_ref[...]).astype(MX)
        out_ref[...] = h

    hid = jnp.maximum(jnp.dot(v_ref[...], wu_ref[...].astype(MX), preferred_element_type=F32), 0.0)
    out_ref[...] += jnp.dot((hid * hid).astype(MX), wd_ref[...].astype(MX),
                            preferred_element_type=F32)


def _mlp(h, g, wu, wd, layer, *, tm, tf):
    S, D = h.shape
    return pl.pallas_call(
        _mlp_kernel,
        grid=(S // tm, wu.shape[2] // tf),
        in_specs=[pl.BlockSpec((tm, D), lambda i, f: (i, 0)),
                  pl.BlockSpec((1, D), lambda i, f: (0, 0)),
                  pl.BlockSpec((None, D, tf), lambda i, f: (layer, 0, f)),
                  pl.BlockSpec((None, tf, D), lambda i, f: (layer, f, 0))],
        out_specs=pl.BlockSpec((tm, D), lambda i, f: (i, 0)),
        out_shape=jax.ShapeDtypeStruct((S, D), F32),
        scratch_shapes=[pltpu.VMEM((tm, D), MX)],
        compiler_params=_params("parallel", "arbitrary"),
        name="mlp",
    )(h, g, wu, wd)


def _ple_kernel(h_ref, p_ref, wg_ref, wp_ref, gf_ref, out_ref, *, final_norm):
    h = h_ref[...]
    gate = _sigmoid(jnp.dot(h.astype(MX), wg_ref[...], preferred_element_type=F32))
    emb = jnp.dot(p_ref[...].astype(MX), wp_ref[...], preferred_element_type=F32)
    y = h + gate * emb
    out_ref[...] = _rms(y, gf_ref[...]) if final_norm else y


def _ple(h, p, wg, wp, gf, *, tm, final_norm):
    S, D = h.shape
    P = p.shape[1]
    return pl.pallas_call(
        functools.partial(_ple_kernel, final_norm=final_norm),
        grid=(S // tm,),
        in_specs=[pl.BlockSpec((tm, D), lambda i: (i, 0)),
                  pl.BlockSpec((tm, P), lambda i: (i, 0)),
                  pl.BlockSpec((D, D), lambda i: (0, 0)),
                  pl.BlockSpec((P, D), lambda i: (0, 0)),
                  pl.BlockSpec((1, D), lambda i: (0, 0))],
        out_specs=pl.BlockSpec((tm, D), lambda i: (i, 0)),
        out_shape=jax.ShapeDtypeStruct((S, D), F32),
        compiler_params=_params("parallel"),
        name="ple",
    )(h, p, wg, wp, gf)


S5_PANELS = BRANCH_W // LANE
S5_PANEL_COLS = S5_COLS // S5_PANELS
T_A0, T_A1, T_A2, T_A4 = range(4)


def _cmul_add(xr, xi, ar, ai, br, bi):
    return xr + (ar * br - ai * bi), xi + (ar * bi + ai * br)


def _s5_kernel(u_ref, bre_ref, bim_ref, tab_ref, cre_ref, cim_ref, d_ref, wglu_ref, out_ref,
               hre_ref, him_ref, carry_ref):
    tm = u_ref.shape[0]

    @pl.when(pl.program_id(0) == 0)
    def _():
        carry_ref[...] = jnp.zeros_like(carry_ref)

    u = u_ref[...]
    ub = u.astype(MX)
    for k in range(S5_PANELS):
        uk = ub[:, k * LANE:(k + 1) * LANE]
        cols = slice(k * S5_PANEL_COLS, (k + 1) * S5_PANEL_COLS)
        hre_ref[:, cols] = jnp.dot(uk, bre_ref[k], preferred_element_type=F32)
        him_ref[:, cols] = jnp.dot(uk, bim_ref[k], preferred_element_type=F32)

    def body(r, carry):
        rows = pl.ds(pl.multiple_of(r * SUBLANE, SUBLANE), SUBLANE)
        for cb in range(S5_COLS // LANE):
            cols = slice(cb * LANE, (cb + 1) * LANE)
            xr, xi = _cmul_add(hre_ref[rows, cols], him_ref[rows, cols],
                               tab_ref[T_A0, 0, :, cols], tab_ref[T_A0, 1, :, cols],
                               carry_ref[0, :, cols], carry_ref[1, :, cols])
            for k, sh in ((T_A1, 1), (T_A2, 2), (T_A4, 4)):
                xr, xi = _cmul_add(xr, xi, tab_ref[k, 0, :, cols], tab_ref[k, 1, :, cols],
                                   pltpu.roll(xr, sh, 0), pltpu.roll(xi, sh, 0))
            hre_ref[rows, cols] = xr
            him_ref[rows, cols] = xi
            carry_ref[0, :, cols] = jnp.broadcast_to(xr[SUBLANE - 1:SUBLANE, :], xr.shape)
            carry_ref[1, :, cols] = jnp.broadcast_to(xi[SUBLANE - 1:SUBLANE, :], xi.shape)
        return carry

    lax.fori_loop(0, tm // SUBLANE, body, 0)

    ys = []
    for k in range(S5_PANELS):
        cols = slice(k * S5_PANEL_COLS, (k + 1) * S5_PANEL_COLS)
        ys.append(jnp.dot(hre_ref[:, cols].astype(MX), cre_ref[k], preferred_element_type=F32)
                  + jnp.dot(him_ref[:, cols].astype(MX), cim_ref[k], preferred_element_type=F32))
    y = jnp.concatenate(ys, axis=1) + d_ref[...] * u
    y = 0.5 * y * (1.0 + jnp.tanh(math.sqrt(2.0 / math.pi) * (y + 0.044715 * (y * y * y))))
    gl = jnp.dot(y.astype(MX), wglu_ref[...], preferred_element_type=F32)
    out_ref[...] = (gl[:, :BRANCH_W] * _sigmoid(gl[:, BRANCH_W:])).astype(MX)


def _s5_tables(a_re, a_im, log_dt, b_re, b_im, c_re, c_im):
    dt = jnp.exp(log_dt)[:, None]
    xr, xi = a_re * dt, a_im * dt

    pr = jnp.exp(xr) * jnp.cos(xi)
    pi = jnp.exp(xr) * jnp.sin(xi)
    den = a_re * a_re + a_im * a_im
    qr = ((pr - 1.0) * a_re + pi * a_im) / den
    qi = (pi * a_re - (pr - 1.0) * a_im) / den
    bbar_re = qr[..., None] * b_re - qi[..., None] * b_im
    bbar_im = qr[..., None] * b_im + qi[..., None] * b_re

    flat = lambda z: z.reshape(1, S5_COLS)
    powers = [(flat(pr), flat(pi))]
    for _ in range(3):
        zr, zi = powers[-1]
        powers.append((zr * flat(pr) - zi * flat(pi), zr * flat(pi) + zi * flat(pr)))
    row = jnp.arange(SUBLANE)[:, None]
    planes = [(jnp.where(row == 0, flat(pr), 0.0), jnp.where(row == 0, flat(pi), 0.0))]
    for sh in (1, 2, 4):
        zr, zi = powers[sh - 1]
        planes.append((jnp.where(row >= sh, zr, 0.0), jnp.where(row >= sh, zi, 0.0)))
    tab = jnp.stack([jnp.stack(p) for p in planes]).astype(F32)

    pg = S5_GROUPS // S5_PANELS
    eye = jnp.eye(pg, dtype=F32)

    def bdiag_in(m):
        m = m.reshape(S5_PANELS, pg, S5_STATE, S5_GROUP)
        return jnp.einsum('pgnc,gh->pgchn', m, eye).reshape(S5_PANELS, LANE, S5_PANEL_COLS)

    def bdiag_out(m):
        m = m.reshape(S5_PANELS, pg, S5_GROUP, S5_STATE)
        return jnp.einsum('pgcn,gh->pgnhc', m, eye).reshape(S5_PANELS, S5_PANEL_COLS, LANE)

    return (tab, bdiag_in(bbar_re).astype(MX), bdiag_in(bbar_im).astype(MX),
            bdiag_out(c_re).astype(MX), bdiag_out(-c_im).astype(MX))


def _s5_mixer(za, tabs, d_skip, w_glu, *, tm):
    S = za.shape[0]
    tab, bre, bim, cre, cim = tabs
    full = lambda shape: pl.BlockSpec(shape, lambda i: (0,) * len(shape))
    return pl.pallas_call(
        _s5_kernel,
        grid=(S // tm,),
        in_specs=[pl.BlockSpec((tm, BRANCH_W), lambda i: (i, ZA_S5)),
                  full(bre.shape), full(bim.shape), full(tab.shape), full(cre.shape),
                  full(cim.shape), full((1, BRANCH_W)), full(w_glu.shape)],
        out_specs=pl.BlockSpec((tm, BRANCH_W), lambda i: (i, 0)),
        out_shape=jax.ShapeDtypeStruct((S, BRANCH_W), MX),
        scratch_shapes=[pltpu.VMEM((tm, S5_COLS), F32), pltpu.VMEM((tm, S5_COLS), F32),
                        pltpu.VMEM((2, SUBLANE, S5_COLS), F32)],
        compiler_params=_params("arbitrary"),
        name="s5",
    )(za, bre, bim, tab, cre, cim, d_skip.reshape(1, BRANCH_W), w_glu)


def _pool_kernel(x_ref, w_ref, scale_ref, out_ref, ext_ref):
    tm = x_ref.shape[0]
    i = pl.program_id(0)

    @pl.when(i == 0)
    def _():
        ext_ref[0:POOL_HALO, :] = jnp.zeros((POOL_HALO, BRANCH_W), F32)

    x = x_ref[...]
    ext_ref[POOL_HALO:, :] = x
    t = i * tm + lax.broadcasted_iota(jnp.int32, (tm, 1), 0)
    for gi, w in enumerate(POOL_WINDOWS):
        cols = slice(gi * POOL_GROUP, (gi + 1) * POOL_GROUP)
        acc = x[:, cols]
        for d in range(1, w):
            acc = acc + ext_ref[POOL_HALO - d:POOL_HALO - d + tm, cols]
        count = jnp.minimum(t + 1, w).astype(F32)
        diff = (acc / count - x[:, cols]).astype(MX)
        y = jnp.dot(diff, w_ref[gi], preferred_element_type=F32)
        out_ref[:, cols] = (y * scale_ref[:, cols]).astype(MX)
    ext_ref[0:POOL_HALO, :] = x[tm - POOL_HALO:, :]


def _pool_mixer(za, w_pool, pool_scale, *, tm):
    S = za.shape[0]
    return pl.pallas_call(
        _pool_kernel,
        grid=(S // tm,),
        in_specs=[pl.BlockSpec((tm, BRANCH_W), lambda i: (i, ZA_POOL)),
                  pl.BlockSpec(w_pool.shape, lambda i: (0, 0, 0)),
                  pl.BlockSpec((1, BRANCH_W), lambda i: (0, 0))],
        out_specs=pl.BlockSpec((tm, BRANCH_W), lambda i: (i, 0)),
        out_shape=jax.ShapeDtypeStruct((S, BRANCH_W), MX),
        scratch_shapes=[pltpu.VMEM((tm + POOL_HALO, BRANCH_W), F32)],
        compiler_params=_params("arbitrary"),
        name="pool",
    )(za, w_pool, pool_scale.reshape(1, BRANCH_W))


def _rotary_tables(S):
    dh, C = RET_HEAD_DIM, RET_CHUNK
    angle = np.repeat(1.0 / (10000.0 ** np.linspace(0.0, 1.0, dh // 2)), 2)[None, :]
    within = np.arange(C, dtype=np.float64)[:, None] * angle
    start = (np.arange(S // C, dtype=np.float64) * C)[:, None] * angle
    tab = lambda a: jnp.asarray(a, dtype=F32)
    return (tab(np.cos(within)), tab(np.sin(within)),
            tab(np.cos(start)[:, None, :]), tab(np.sin(start)[:, None, :]))


def _ret_consts():
    H, dh, C = RET_HEADS, RET_HEAD_DIM, RET_CHUNK
    log_g = jnp.log(1.0 - 2.0 ** (-5.0 - jnp.arange(H, dtype=F32)))
    idx = jnp.arange(C, dtype=F32)
    rel = idx[:, None] - idx[None, :]
    decay = jnp.where(rel >= 0, jnp.exp(jnp.maximum(rel, 0.0)[None] * log_g[:, None, None]), 0.0)
    xi = jnp.exp((idx + 1.0)[None, :] * log_g[:, None])[..., None]
    zeta = jnp.exp((C - 1.0 - idx)[None, :] * log_g[:, None])[..., None]
    g_chunk = jnp.broadcast_to(jnp.exp(C * log_g)[:, None, None], (H, 1, dh))
    return decay, xi, zeta, g_chunk


def _rotate_pairs(x, even):
    n = x.shape[-1]
    return jnp.where(even, -pltpu.roll(x, n - 1, 1), pltpu.roll(x, 1, 1))


def _ret_kernel(q_ref, k_ref, v_ref, g_ref, ca_ref, sa_ref, cb_ref, sb_ref, decay_ref, xi_ref,
                zeta_ref, gch_ref, out_ref, state_ref):
    C, dh = RET_CHUNK, RET_HEAD_DIM

    @pl.when(pl.program_id(0) == 0)
    def _():
        state_ref[...] = jnp.zeros_like(state_ref)

    ca, sa, cb, sb = ca_ref[...], sa_ref[...], cb_ref[0], sb_ref[0]
    cos = ca * cb - sa * sb
    sin = sa * cb + ca * sb
    even = (lax.broadcasted_iota(jnp.int32, (C, dh), 1) % 2) == 0
    for hd in range(RET_HEADS):
        cols = slice(hd * dh, (hd + 1) * dh)
        q = q_ref[:, cols]
        k = k_ref[:, cols]
        vb = v_ref[:, cols].astype(MX)
        q = q * cos + _rotate_pairs(q, even) * sin
        k = (k * cos + _rotate_pairs(k, even) * sin) * dh ** -0.5
        qb = q.astype(MX)
        s = lax.dot_general(qb, k.astype(MX), (((1,), (1,)), ((), ())),
                            preferred_element_type=F32) * decay_ref[hd]
        inner = jnp.dot(s.astype(MX), vb, preferred_element_type=F32)
        state = state_ref[hd]
        cross = jnp.dot(qb, state.astype(MX), preferred_element_type=F32) * xi_ref[hd]
        kz = (k * zeta_ref[hd]).astype(MX)
        state_ref[hd] = state * gch_ref[hd] + lax.dot_general(
            kz, vb, (((0,), (0,)), ((), ())), preferred_element_type=F32)
        y = inner + cross
        y = y * lax.rsqrt(jnp.mean(y * y, axis=-1, keepdims=True) + NORM_EPS)
        g = g_ref[:, cols]
        out_ref[:, cols] = (g * _sigmoid(g) * y).astype(MX)


def _ret_mixer(za):
    S = za.shape[0]
    C = RET_CHUNK
    ca, sa, cb, sb = _rotary_tables(S)
    consts = (ca, sa, cb, sb) + _ret_consts()
    col = lambda c: pl.BlockSpec((C, BRANCH_W), lambda i: (i, c))
    full = lambda a: pl.BlockSpec(a.shape, lambda i: (0,) * a.ndim)
    start = lambda a: pl.BlockSpec((1,) + a.shape[1:], lambda i: (i, 0, 0))
    return pl.pallas_call(
        _ret_kernel,
        grid=(S // C,),
        in_specs=[col(ZA_RQ), col(ZA_RK), col(ZA_RV), col(ZA_RG), full(ca), full(sa), start(cb),
                  start(sb)] + [full(a) for a in consts[4:]],
        out_specs=pl.BlockSpec((C, BRANCH_W), lambda i: (i, 0)),
        out_shape=jax.ShapeDtypeStruct((S, BRANCH_W), MX),
        scratch_shapes=[pltpu.VMEM((RET_HEADS, RET_HEAD_DIM, RET_HEAD_DIM), F32)],
        compiler_params=_params("arbitrary"),
        name="retention",
    )(za, za, za, za, *consts)


NSA_R = NSA_HEADS // NSA_KV_GROUPS
NSA_TQ = 128
SLC_TQ = 512
WIN_TQ = 256
NSA_TK = 512
CMP_CH = 256
P_OFF = SUBLANE
NT_DIMS = (((1,), (1,)), ((), ()))


def _compress_kernel(x_ref, w_ref, out_ref):
    out_ref[0] = jnp.dot(x_ref[0], w_ref[...], preferred_element_type=F32).astype(out_ref.dtype)


def _compress_t_kernel(x_ref, wt_ref, out_ref):
    out_ref[0] = lax.dot_general(wt_ref[...], x_ref[0], NT_DIMS,
                                 preferred_element_type=F32).astype(out_ref.dtype)


def _compress(x, w, *, transposed):
    G, nc, kk = x.shape
    if transposed:
        kern, out_block = _compress_t_kernel, (1, w.shape[0], nc)
    else:
        kern, out_block = _compress_kernel, (1, nc, w.shape[1])
    return pl.pallas_call(
        kern,
        grid=(G,),
        in_specs=[pl.BlockSpec((1, nc, kk), lambda g: (g, 0, 0)),
                  pl.BlockSpec(w.shape, lambda g: (0, 0))],
        out_specs=pl.BlockSpec(out_block, lambda g: (g, 0, 0)),
        out_shape=jax.ShapeDtypeStruct((G,) + out_block[1:], x.dtype),
        compiler_params=_params("arbitrary"),
        name="nsa_compress",
    )(x, w)


def _group_queries(q_ref, g, extra_scale=1.0):
    hd = NSA_HEAD_DIM
    base = g * NSA_R * hd
    qs = [q_ref[:, base + r * hd:base + (r + 1) * hd] for r in range(NSA_R)]
    return (jnp.concatenate(qs, axis=0) * (hd ** -0.5 * extra_scale)).astype(MX)


def _cmp_select_kernel(q_ref, kc_ref, vct_ref, ocmp_ref, sel_ref, s_sc, p_sc, ot_sc):
    i = pl.program_id(0)
    tq, hd = NSA_TQ, NSA_HEAD_DIM
    nc = kc_ref.shape[1]
    ns = sel_ref.shape[2]
    ch = min(CMP_CH, nc)
    nch = jnp.minimum((tq // CMP_STRIDE * (i + 1) + ch - 1) // ch, nc // ch)
    t4 = i * tq + lax.broadcasted_iota(jnp.int32, (1, NSA_R * tq), 1) % tq
    t1 = i * tq + lax.broadcasted_iota(jnp.int32, (1, tq), 1)
    nrow = lax.broadcasted_iota(jnp.int32, (ch, 1), 0)

    def rows(c):
        return pl.ds(pl.multiple_of(c * ch, ch), ch)

    def visible(c):
        return ((c * ch + nrow) * CMP_STRIDE + (CMP_LEN - 1)) <= t4

    groups = range(NSA_KV_GROUPS)
    qgs = [_group_queries(q_ref, g) for g in groups]
    lanes = (1, NSA_R * tq)

    def scores(c, ms):
        vis = visible(c)
        out = []
        for g in groups:
            s = lax.dot_general(kc_ref[g, rows(c), :], qgs[g], NT_DIMS, preferred_element_type=F32)
            s = jnp.where(vis, s, NEG_INF)
            s_sc[g, rows(c), :] = s
            out.append(jnp.maximum(ms[g], jnp.max(s, axis=0, keepdims=True)))
        return tuple(out)

    ms = lax.fori_loop(0, nch, scores, tuple(jnp.full(lanes, NEG_INF, F32) for _ in groups))

    def expsum(c, ls):
        vis = visible(c)
        out = []
        for g in groups:
            e = jnp.where(vis, jnp.exp(s_sc[g, rows(c), :] - ms[g]), 0.0)
            s_sc[g, rows(c), :] = e
            out.append(ls[g] + jnp.sum(e, axis=0, keepdims=True))
        return tuple(out)

    ls = lax.fori_loop(0, nch, expsum, tuple(jnp.zeros(lanes, F32) for _ in groups))
    invs = [1.0 / jnp.maximum(l, 1e-30) for l in ls]
    p_sc[...] = jnp.zeros_like(p_sc)
    ot_sc[...] = jnp.zeros_like(ot_sc)

    def weigh(c, carry):
        for g in groups:
            p = s_sc[g, rows(c), :] * invs[g]
            p_sc[g, pl.ds(pl.multiple_of(P_OFF + c * ch, SUBLANE), ch), :] = (
                p[:, 0:tq] + p[:, tq:2 * tq] + p[:, 2 * tq:3 * tq] + p[:, 3 * tq:4 * tq])
            ot_sc[g] += jnp.dot(vct_ref[g, :, rows(c)], p.astype(MX), preferred_element_type=F32)
        return carry

    lax.fori_loop(0, nch, weigh, 0)
    for h in range(NSA_HEADS):
        g, r = divmod(h, NSA_R)
        o = ot_sc[g, :, r * tq:(r + 1) * tq].T
        ocmp_ref[:, h * hd:(h + 1) * hd] = o[:, :hd]

    ratio = SLC_BLOCK // CMP_STRIDE

    def pick(nr):
        jrow = lax.broadcasted_iota(jnp.int32, (nr, 1), 0)
        jfull = jnp.broadcast_to(jrow.astype(F32), (nr, tq))
        valid = jrow * SLC_BLOCK <= t1
        bonus = jnp.where((jrow == 0) | (jrow == t1 // SLC_BLOCK), FORCE_BONUS, 0.0)
        xs = []
        for g in groups:
            imp = p_sc[g, pl.ds(P_OFF - 1, nr, stride=ratio), :]
            for k in range(ratio):
                imp = imp + p_sc[g, pl.ds(P_OFF + k, nr, stride=ratio), :]
            xs.append(jnp.where(valid, imp + bonus, NEG_INF))
        for _ in range(min(SLC_TOP_N, ns)):
            for g in groups:
                mx = jnp.max(xs[g], axis=0, keepdims=True)
                cand = jnp.where(xs[g] == mx, jfull, float(nr))
                xs[g] = jnp.where(cand == jnp.min(cand, axis=0, keepdims=True), -jnp.inf, xs[g])
        for g in groups:
            picked = jnp.where(xs[g] == -jnp.inf, 1.0, 0.0).T.astype(BF16)
            if nr < ns:
                picked = jnp.concatenate([picked, jnp.zeros((tq, ns - nr), BF16)], axis=1)
            sel_ref[g] = picked

    half = ns // 2
    if half >= SLC_TOP_N:
        early = (i + 1) * (tq // SLC_BLOCK) <= half

        @pl.when(early)
        def _():
            pick(half)

        @pl.when(jnp.logical_not(early))
        def _():
            pick(ns)
    else:
        pick(ns)


def _cmp_select(za, kc, vct):
    S = za.shape[0]
    G, nc, hd = kc.shape
    ns = S // SLC_BLOCK
    tq = NSA_TQ
    return pl.pallas_call(
        _cmp_select_kernel,
        grid=(S // tq,),
        in_specs=[pl.BlockSpec((tq, BRANCH_W), lambda i: (i, ZA_Q)),
                  pl.BlockSpec(kc.shape, lambda i: (0, 0, 0)),
                  pl.BlockSpec(vct.shape, lambda i: (0, 0, 0))],
        out_specs=[pl.BlockSpec((tq, BRANCH_W), lambda i: (i, 0)),
                   pl.BlockSpec((G, tq, ns), lambda i: (0, i, 0))],
        out_shape=[jax.ShapeDtypeStruct((S, BRANCH_W), F32),
                   jax.ShapeDtypeStruct((G, S, ns), BF16)],
        scratch_shapes=[pltpu.VMEM((G, nc, NSA_R * tq), F32), pltpu.VMEM((G, P_OFF + nc, tq), F32),
                        pltpu.VMEM((G, LANE, NSA_R * tq), F32)],
        compiler_params=_params("parallel"),
        name="nsa_cmp_select",
    )(za, kc, vct)


SLC_MASK = 2.0 ** 100
LOG2E = math.log2(math.e)


def _slc_kernel(q_ref, kx_ref, vs_ref, sel_ref, out_ref, lhs_sc, m_sc, acc_sc, *, tk):
    i = pl.program_id(1)
    tq, hd = SLC_TQ, NSA_HEAD_DIM
    ksel = kx_ref.shape[2] - LANE
    bpt = tk // SLC_BLOCK
    qg = _group_queries(q_ref, 0, LOG2E)
    pad = jnp.zeros((NSA_R * tq, LANE - hd), MX)
    for h in range(sel_ref.shape[2] // ksel):
        unsel = (1.0 - sel_ref[0, :, h * ksel:(h + 1) * ksel].astype(F32)).astype(MX)
        lhs_sc[h] = jnp.concatenate([qg, pad, jnp.concatenate([unsel] * NSA_R, axis=0)], axis=1)
    t = i * tq + lax.broadcasted_iota(jnp.int32, (tq, 1), 0)
    lane = lax.broadcasted_iota(jnp.int32, (1, tk), 1)
    m_sc[...] = jnp.full_like(m_sc, NEG_INF)
    acc_sc[...] = jnp.zeros_like(acc_sc)

    def tile(kt, diagonal):
        k0 = pl.multiple_of(kt * tk, tk)
        s = lax.dot_general(lhs_sc[(kt * bpt) // ksel], kx_ref[0, pl.ds(k0, tk), :], NT_DIMS,
                            preferred_element_type=F32)
        if diagonal:
            hidden = jnp.where(k0 + lane <= t, 0.0, NEG_INF)
            s = s + jnp.concatenate([hidden] * NSA_R, axis=0)
        m_prev = m_sc[...]
        m_new = jnp.maximum(m_prev, jnp.max(s, axis=-1, keepdims=True))
        p = jnp.exp2(s - jnp.concatenate([m_new] * (tk // LANE), axis=1))
        acc_sc[...] = jnp.exp2(m_prev - m_new) * acc_sc[...] + jnp.dot(
            p.astype(MX), vs_ref[0, pl.ds(k0, tk), :], preferred_element_type=F32)
        m_sc[...] = m_new

    last = (i * tq) // tk

    def quad(j, carry):
        for u in range(4):
            tile(4 * j + u, False)
        return carry

    lax.fori_loop(0, last // 4, quad, 0)
    done = (last // 4) * 4

    @pl.when(last - done >= 2)
    def _():
        tile(done, False)
        tile(done + 1, False)

    @pl.when((last - done) % 2 == 1)
    def _():
        tile(last - 1, False)

    tile(last, True)
    acc = acc_sc[...]
    o = acc[:, :hd] / acc[:, hd:]
    for r in range(NSA_R):
        out_ref[:, r * hd:(r + 1) * hd] = o[r * tq:(r + 1) * tq, :]


def _slc_attention(za, ks, vs, sel):
    S = za.shape[0]
    G = ks.shape[0]
    ns = sel.shape[2]
    tq, hd = SLC_TQ, NSA_HEAD_DIM
    tk = min(NSA_TK, S)
    ksel = min(LANE, ns)
    own_block = (jnp.arange(S)[:, None] // SLC_BLOCK) % ksel == jnp.arange(ksel)[None, :]
    kx = jnp.concatenate([ks, jnp.zeros((G, S, LANE - hd), ks.dtype),
                          jnp.broadcast_to(jnp.where(own_block, -SLC_MASK, 0.0).astype(ks.dtype),
                                           (G, S, ksel))], axis=2)
    gw = NSA_R * hd
    return pl.pallas_call(
        functools.partial(_slc_kernel, tk=tk),
        grid=(G, S // tq),
        in_specs=[pl.BlockSpec((tq, gw), lambda g, i: (i, g)),
                  pl.BlockSpec((1, S, LANE + ksel), lambda g, i: (g, 0, 0)),
                  pl.BlockSpec((1, S, LANE), lambda g, i: (g, 0, 0)),
                  pl.BlockSpec((1, tq, ns), lambda g, i: (g, i, 0))],
        out_specs=pl.BlockSpec((tq, gw), lambda g, i: (i, g)),
        out_shape=jax.ShapeDtypeStruct((S, BRANCH_W), F32),
        scratch_shapes=[pltpu.VMEM((ns // ksel, NSA_R * tq, LANE + ksel), MX),
                        pltpu.VMEM((NSA_R * tq, LANE), F32), pltpu.VMEM((NSA_R * tq, LANE), F32)],
        compiler_params=_params("arbitrary", "arbitrary"),
        name="nsa_selected",
    )(za, kx, vs, sel)


def _win_kernel(q_ref, kw_ref, vw_ref, gate_ref, ocmp_ref, oslc_ref, out_ref):
    i = pl.program_id(1)
    tq, hd = WIN_TQ, NSA_HEAD_DIM
    span = tq + WIN
    qg = _group_queries(q_ref, 0, LOG2E)
    r0 = pl.multiple_of(i * tq, tq)
    t = i * tq + lax.broadcasted_iota(jnp.int32, (tq, 1), 0)
    pos = i * tq - WIN + lax.broadcasted_iota(jnp.int32, (1, span), 1)
    ok = (pos <= t) & (pos > t - WIN) & (pos >= 0)
    s = lax.dot_general(qg, kw_ref[0, pl.ds(r0, span), :], NT_DIMS, preferred_element_type=F32)
    s = s + jnp.concatenate([jnp.where(ok, 0.0, NEG_INF)] * NSA_R, axis=0)
    e = jnp.exp2(s - jnp.max(s, axis=-1, keepdims=True))
    acc = jnp.dot(e.astype(MX), vw_ref[0, pl.ds(r0, span), :], preferred_element_type=F32)
    ow = acc[:, :hd] / acc[:, hd:]
    sig = _sigmoid(gate_ref[...])
    for r in range(NSA_R):
        cols = slice(r * hd, (r + 1) * hd)
        out_ref[:, cols] = (sig[:, r:r + 1] * ocmp_ref[:, cols]
                            + sig[:, NSA_R + r:NSA_R + r + 1] * oslc_ref[:, cols]
                            + sig[:, 2 * NSA_R + r:2 * NSA_R + r + 1] * ow[r * tq:(r + 1) * tq, :]
                            ).astype(MX)


def _win_combine(za, kw, vw, ocmp, oslc):
    S = za.shape[0]
    G = kw.shape[0]
    tq, hd = WIN_TQ, NSA_HEAD_DIM
    gw = NSA_R * hd
    blk = lambda: pl.BlockSpec((tq, gw), lambda g, i: (i, g))
    return pl.pallas_call(
        _win_kernel,
        grid=(G, S // tq),
        in_specs=[blk(),
                  pl.BlockSpec((1, S + WIN, hd), lambda g, i: (g, 0, 0)),
                  pl.BlockSpec((1, S + WIN, LANE), lambda g, i: (g, 0, 0)),
                  pl.BlockSpec((tq, GATE_PAD), lambda g, i: (i, ZA_G_OFF // GATE_PAD + g)),
                  blk(), blk()],
        out_specs=blk(),
        out_shape=jax.ShapeDtypeStruct((S, BRANCH_W), MX),
        compiler_params=_params("arbitrary", "arbitrary"),
        name="nsa_window_combine",
    )(za, kw, vw, za, ocmp, oslc)


def _nsa_mixer(za, w_cmp_k, w_cmp_v):
    S = za.shape[0]
    G, hd = NSA_KV_GROUPS, NSA_HEAD_DIM
    nc = S // CMP_STRIDE
    kvz = lax.optimization_barrier(za[:, ZA_KV_OFF:ZA_KV_OFF + NSA_KV_COLS].astype(MX))

    def kv(idx):
        off = idx * G * hd
        return jnp.stack([kvz[:, off + g * hd:off + (g + 1) * hd] for g in range(G)])

    def blocks(x):
        half = x.reshape(G, nc, CMP_STRIDE * hd)
        nxt = jnp.concatenate([half[:, 1:], jnp.zeros_like(half[:, :1])], axis=1)
        return jnp.concatenate([half, nxt], axis=2).astype(MX)

    def with_ones(x):
        return jnp.concatenate([x, jnp.ones((G, S, LANE - hd), x.dtype)], axis=2).astype(MX)

    front = lambda x: jnp.pad(x, ((0, 0), (WIN, 0), (0, 0)))
    wk = w_cmp_k.reshape(CMP_LEN * hd, hd).astype(MX)
    wvt = jnp.pad(w_cmp_v.reshape(CMP_LEN * hd, hd).T, ((0, LANE - hd), (0, 0))).astype(MX)
    kc = _compress(blocks(kv(0)), wk, transposed=False)
    vct = _compress(blocks(kv(1)), wvt, transposed=True)
    ocmp, sel = _cmp_select(za, kc, vct)
    oslc = _slc_attention(za, kv(2).astype(MX), with_ones(kv(3)), sel)
    return _win_combine(za, front(kv(4).astype(MX)), front(with_ones(kv(5))), ocmp, oslc)


def _pack_w_in(w):
    seg = lambda off, n: w[:, off:off + n]
    R = NSA_HEADS // NSA_KV_GROUPS
    gates = []
    for g in range(NSA_KV_GROUPS):
        cols = [OFF_G + br * NSA_HEADS + g * R + r for br in range(3) for r in range(R)]
        gates.append(jnp.pad(w[:, jnp.array(cols)], ((0, 0), (0, GATE_PAD - len(cols)))))
    wa = jnp.concatenate([seg(OFF_Q, BRANCH_W), seg(OFF_S5, BRANCH_W), seg(OFF_POOL, BRANCH_W),
                          seg(OFF_RQ, BRANCH_W), seg(OFF_RK, BRANCH_W), seg(OFF_RV, BRANCH_W),
                          seg(OFF_RG, BRANCH_W), seg(OFF_KV, NSA_KV_COLS)] + gates, axis=1)
    return wa.astype(MX), w[:, OFF_MERGE:].astype(MX)


def kernel(x, p, g_mix, w_in, w_cmp_k, w_cmp_v, s5_a_re, s5_a_im, s5_log_dt, s5_b_re, s5_b_im,
           s5_c_re, s5_c_im, s5_d, s5_w_glu, pool_w, pool_scale, w_branch, w_out, g_mlp,
           w_mlp_up, w_mlp_down, w_ple_gate, w_ple_proj, g_final):
    B, S, D = x.shape
    assert B == 1 and D == D_MODEL
    depth = w_in.shape[0]
    h = x.reshape(S, D)
    for i in range(depth):
        wa, wm = _pack_w_in(w_in[i])
        za, u = _inproj(h, g_mix[i].reshape(1, D), wa, tm=1024, tn=768)

        o_nsa = _nsa_mixer(za, w_cmp_k[i], w_cmp_v[i])
        s5_tabs = _s5_tables(s5_a_re[i], s5_a_im[i], s5_log_dt[i], s5_b_re[i], s5_b_im[i],
                             s5_c_re[i], s5_c_im[i])
        o_s5 = _s5_mixer(za, s5_tabs, s5_d[i], s5_w_glu[i].astype(MX), tm=512)
        o_pool = _pool_mixer(za, pool_w[i].astype(MX), pool_scale[i], tm=1024)
        o_ret = _ret_mixer(za)

        merged = _merge(u, (o_nsa, o_s5, o_pool, o_ret), wm, w_branch[i].astype(MX),
                        tm=1024, tn=512)
        h = _resid_mm(h, merged, w_out[i].astype(MX), tm=1024, tn=1024)
        h = _mlp(h, g_mlp[i].reshape(1, D), w_mlp_up, w_mlp_down, i, tm=1024, tf=512)
        h = _ple(h, p[i].reshape(S, PLE_DIM), w_ple_gate[i].astype(MX),
                 w_ple_proj[i].astype(MX), g_final.reshape(1, D), tm=512,
                 final_norm=(i == depth - 1))
    return h.reshape(B, S, D)
```

```python
import functools
import math

import jax
import jax.numpy as jnp
import numpy as np
from jax import lax
from jax.experimental import pallas as pl
from jax.experimental.pallas import tpu as pltpu

F32 = jnp.float32
BF16 = jnp.bfloat16
MX = BF16

D_MODEL = 2048
N_BRANCH = 4
BRANCH_W = D_MODEL // N_BRANCH
NSA_HEADS = 8
NSA_KV_GROUPS = 2
NSA_HEAD_DIM = BRANCH_W // NSA_HEADS
CMP_LEN = 32
CMP_STRIDE = 16
SLC_BLOCK = 64
SLC_TOP_N = 16
WIN = 512
Q_BLOCK = 128
S5_GROUP = 16
S5_GROUPS = BRANCH_W // S5_GROUP
S5_STATE = 64
S5_COLS = S5_GROUPS * S5_STATE
POOL_WINDOWS = (2, 4, 8, 16)
POOL_GROUP = BRANCH_W // len(POOL_WINDOWS)
POOL_HALO = 16
RET_HEADS = 4
RET_HEAD_DIM = BRANCH_W // RET_HEADS
RET_CHUNK = 128
D_FF = 4 * D_MODEL
PLE_DIM = 256
NORM_EPS = 1e-6
NEG_INF = -1e30
FORCE_BONUS = 1e4

NSA_KV_COLS = 6 * NSA_KV_GROUPS * NSA_HEAD_DIM
NSA_GATE_COLS = 3 * NSA_HEADS
OFF_Q = 0
OFF_KV = OFF_Q + BRANCH_W
OFF_G = OFF_KV + NSA_KV_COLS
OFF_S5 = OFF_G + NSA_GATE_COLS
OFF_POOL = OFF_S5 + BRANCH_W
OFF_RQ = OFF_POOL + BRANCH_W
OFF_RK = OFF_RQ + BRANCH_W
OFF_RV = OFF_RK + BRANCH_W
OFF_RG = OFF_RV + BRANCH_W
OFF_MERGE = OFF_RG + BRANCH_W

LANE = 128
SUBLANE = 8
GATE_PAD = LANE
ZA_Q, ZA_S5, ZA_POOL, ZA_RQ, ZA_RK, ZA_RV, ZA_RG = range(7)
ZA_KV_OFF = 7 * BRANCH_W
ZA_G_OFF = ZA_KV_OFF + NSA_KV_COLS
ZA_COLS = ZA_G_OFF + NSA_KV_GROUPS * GATE_PAD

VMEM_LIMIT = 56 * 1024 * 1024


def _params(*sem):
    return pltpu.CompilerParams(dimension_semantics=sem, vmem_limit_bytes=VMEM_LIMIT)


def _rms(x, g):
    return x * lax.rsqrt(jnp.mean(x * x, axis=-1, keepdims=True) + NORM_EPS) * g


def _sigmoid(x):
    return 1.0 / (1.0 + jnp.exp(-x))


def _inproj_kernel(h_ref, g_ref, w_ref, z_ref, u_ref):
    @pl.when(pl.program_id(1) == 0)
    def _():
        u_ref[...] = _rms(h_ref[...], g_ref[...]).astype(MX)

    z_ref[...] = jnp.dot(u_ref[...], w_ref[...], preferred_element_type=F32)


def _inproj(h, g, w, *, tm, tn):
    S, D = h.shape
    N = w.shape[1]
    return pl.pallas_call(
        _inproj_kernel,
        grid=(S // tm, N // tn),
        in_specs=[pl.BlockSpec((tm, D), lambda i, j: (i, 0)),
                  pl.BlockSpec((1, D), lambda i, j: (0, 0)),
                  pl.BlockSpec((D, tn), lambda i, j: (0, j))],
        out_specs=[pl.BlockSpec((tm, tn), lambda i, j: (i, j)),
                   pl.BlockSpec((tm, D), lambda i, j: (i, 0))],
        out_shape=[jax.ShapeDtypeStruct((S, N), F32), jax.ShapeDtypeStruct((S, D), MX)],
        compiler_params=_params("parallel", "arbitrary"),
        name="inproj",
    )(h, g, w)


def _merge_kernel(u_ref, o0_ref, o1_ref, o2_ref, o3_ref, wm0_ref, wm1_ref, wm2_ref, wm3_ref,
                  wb_ref, out_ref):
    u = u_ref[...]
    acc = None
    for j, (o_ref, wm_ref) in enumerate(((o0_ref, wm0_ref), (o1_ref, wm1_ref),
                                         (o2_ref, wm2_ref), (o3_ref, wm3_ref))):
        gate = _sigmoid(jnp.dot(u, wm_ref[...], preferred_element_type=F32))
        br = jnp.dot(o_ref[...], wb_ref[j], preferred_element_type=F32)
        acc = gate * br if acc is None else acc + gate * br
    out_ref[...] = acc.astype(MX)


def _merge(u, outs, wm, wb, *, tm, tn):
    S, D = u.shape
    nt = D // tn
    o_specs = [pl.BlockSpec((tm, BRANCH_W), lambda i, c: (i, 0)) for _ in range(N_BRANCH)]
    wm_specs = [pl.BlockSpec((D, tn), functools.partial(lambda i, c, j: (0, j * nt + c), j=j))
                for j in range(N_BRANCH)]
    return pl.pallas_call(
        _merge_kernel,
        grid=(S // tm, nt),
        in_specs=[pl.BlockSpec((tm, D), lambda i, c: (i, 0))] + o_specs + wm_specs
                 + [pl.BlockSpec((N_BRANCH, BRANCH_W, tn), lambda i, c: (0, 0, c))],
        out_specs=pl.BlockSpec((tm, tn), lambda i, c: (i, c)),
        out_shape=jax.ShapeDtypeStruct((S, D), MX),
        compiler_params=_params("parallel", "arbitrary"),
        name="merge",
    )(u, *outs, wm, wm, wm, wm, wb)


def _resid_mm_kernel(h_ref, x_ref, w_ref, out_ref):
    out_ref[...] = h_ref[...] + jnp.dot(x_ref[...], w_ref[...], preferred_element_type=F32)


def _resid_mm(h, x, w, *, tm, tn):
    S, D = h.shape
    K = x.shape[1]
    return pl.pallas_call(
        _resid_mm_kernel,
        grid=(S // tm, D // tn),
        in_specs=[pl.BlockSpec((tm, tn), lambda i, c: (i, c)),
                  pl.BlockSpec((tm, K), lambda i, c: (i, 0)),
                  pl.BlockSpec((K, tn), lambda i, c: (0, c))],
        out_specs=pl.BlockSpec((tm, tn), lambda i, c: (i, c)),
        out_shape=jax.ShapeDtypeStruct((S, D), F32),
        compiler_params=_params("parallel", "arbitrary"),
        name="resid_mm",
    )(h, x, w)


def _mlp_kernel(h_ref, g_ref, wu_ref, wd_ref, out_ref, v_ref):
    @pl.when(pl.program_id(1) == 0)
    def _():
        h = h_ref[...]
        v_ref[...] = _rms(h, g_ref[...]).astype(MX)
        out_ref[...] = h

    hid = jnp.maximum(jnp.dot(v_ref[...], wu_ref[...].astype(MX), preferred_element_type=F32), 0.0)
    out_ref[...] += jnp.dot((hid * hid).astype(MX), wd_ref[...].astype(MX),
                            preferred_element_type=F32)


def _mlp(h, g, wu, wd, layer, *, tm, tf):
    S, D = h.shape
    return pl.pallas_call(
        _mlp_kernel,
        grid=(S // tm, wu.shape[2] // tf),
        in_specs=[pl.BlockSpec((tm, D), lambda i, f: (i, 0)),
                  pl.BlockSpec((1, D), lambda i, f: (0, 0)),
                  pl.BlockSpec((None, D, tf), lambda i, f: (layer, 0, f)),
                  pl.BlockSpec((None, tf, D), lambda i, f: (layer, f, 0))],
        out_specs=pl.BlockSpec((tm, D), lambda i, f: (i, 0)),
        out_shape=jax.ShapeDtypeStruct((S, D), F32),
        scratch_shapes=[pltpu.VMEM((tm, D), MX)],
        compiler_params=_params("parallel", "arbitrary"),
        name="mlp",
    )(h, g, wu, wd)


def _ple_kernel(h_ref, p_ref, wg_ref, wp_ref, gf_ref, out_ref, *, final_norm):
    h = h_ref[...]
    gate = _sigmoid(jnp.dot(h.astype(MX), wg_ref[...], preferred_element_type=F32))
    emb = jnp.dot(p_ref[...].astype(MX), wp_ref[...], preferred_element_type=F32)
    y = h + gate * emb
    out_ref[...] = _rms(y, gf_ref[...]) if final_norm else y


def _ple(h, p, wg, wp, gf, *, tm, final_norm):
    S, D = h.shape
    P = p.shape[1]
    return pl.pallas_call(
        functools.partial(_ple_kernel, final_norm=final_norm),
        grid=(S // tm,),
        in_specs=[pl.BlockSpec((tm, D), lambda i: (i, 0)),
                  pl.BlockSpec((tm, P), lambda i: (i, 0)),
                  pl.BlockSpec((D, D), lambda i: (0, 0)),
                  pl.BlockSpec((P, D), lambda i: (0, 0)),
                  pl.BlockSpec((1, D), lambda i: (0, 0))],
        out_specs=pl.BlockSpec((tm, D), lambda i: (i, 0)),
        out_shape=jax.ShapeDtypeStruct((S, D), F32),
        compiler_params=_params("parallel"),
        name="ple",
    )(h, p, wg, wp, gf)


S5_PANELS = BRANCH_W // LANE
S5_PANEL_COLS = S5_COLS // S5_PANELS
T_A0, T_A1, T_A2, T_A4 = range(4)


def _cmul_add(xr, xi, ar, ai, br, bi):
    return xr + (ar * br - ai * bi), xi + (ar * bi + ai * br)


def _s5_kernel(u_ref, bre_ref, bim_ref, tab_ref, cre_ref, cim_ref, d_ref, wglu_ref, out_ref,
               hre_ref, him_ref, carry_ref):
    tm = u_ref.shape[0]

    @pl.when(pl.program_id(0) == 0)
    def _():
        carry_ref[...] = jnp.zeros_like(carry_ref)

    u = u_ref[...]
    ub = u.astype(MX)
    for k in range(S5_PANELS):
        uk = ub[:, k * LANE:(k + 1) * LANE]
        cols = slice(k * S5_PANEL_COLS, (k + 1) * S5_PANEL_COLS)
        hre_ref[:, cols] = jnp.dot(uk, bre_ref[k], preferred_element_type=F32)
        him_ref[:, cols] = jnp.dot(uk, bim_ref[k], preferred_element_type=F32)

    def body(r, carry):
        rows = pl.ds(pl.multiple_of(r * SUBLANE, SUBLANE), SUBLANE)
        for cb in range(S5_COLS // LANE):
            cols = slice(cb * LANE, (cb + 1) * LANE)
            xr, xi = _cmul_add(hre_ref[rows, cols], him_ref[rows, cols],
                               tab_ref[T_A0, 0, :, cols], tab_ref[T_A0, 1, :, cols],
                               carry_ref[0, :, cols], carry_ref[1, :, cols])
            for k, sh in ((T_A1, 1), (T_A2, 2), (T_A4, 4)):
                xr, xi = _cmul_add(xr, xi, tab_ref[k, 0, :, cols], tab_ref[k, 1, :, cols],
                                   pltpu.roll(xr, sh, 0), pltpu.roll(xi, sh, 0))
            hre_ref[rows, cols] = xr
            him_ref[rows, cols] = xi
            carry_ref[0, :, cols] = jnp.broadcast_to(xr[SUBLANE - 1:SUBLANE, :], xr.shape)
            carry_ref[1, :, cols] = jnp.broadcast_to(xi[SUBLANE - 1:SUBLANE, :], xi.shape)
        return carry

    lax.fori_loop(0, tm // SUBLANE, body, 0)

    ys = []
    for k in range(S5_PANELS):
        cols = slice(k * S5_PANEL_COLS, (k + 1) * S5_PANEL_COLS)
        ys.append(jnp.dot(hre_ref[:, cols].astype(MX), cre_ref[k], preferred_element_type=F32)
                  + jnp.dot(him_ref[:, cols].astype(MX), cim_ref[k], preferred_element_type=F32))
    y = jnp.concatenate(ys, axis=1) + d_ref[...] * u
    y = 0.5 * y * (1.0 + jnp.tanh(math.sqrt(2.0 / math.pi) * (y + 0.044715 * (y * y * y))))
    gl = jnp.dot(y.astype(MX), wglu_ref[...], preferred_element_type=F32)
    out_ref[...] = (gl[:, :BRANCH_W] * _sigmoid(gl[:, BRANCH_W:])).astype(MX)


def _s5_tables(a_re, a_im, log_dt, b_re, b_im, c_re, c_im):
    dt = jnp.exp(log_dt)[:, None]
    xr, xi = a_re * dt, a_im * dt

    pr = jnp.exp(xr) * jnp.cos(xi)
    pi = jnp.exp(xr) * jnp.sin(xi)
    den = a_re * a_re + a_im * a_im
    qr = ((pr - 1.0) * a_re + pi * a_im) / den
    qi = (pi * a_re - (pr - 1.0) * a_im) / den
    bbar_re = qr[..., None] * b_re - qi[..., None] * b_im
    bbar_im = qr[..., None] * b_im + qi[..., None] * b_re

    flat = lambda z: z.reshape(1, S5_COLS)
    powers = [(flat(pr), flat(pi))]
    for _ in range(3):
        zr, zi = powers[-1]
        powers.append((zr * flat(pr) - zi * flat(pi), zr * flat(pi) + zi * flat(pr)))
    row = jnp.arange(SUBLANE)[:, None]
    planes = [(jnp.where(row == 0, flat(pr), 0.0), jnp.where(row == 0, flat(pi), 0.0))]
    for sh in (1, 2, 4):
        zr, zi = powers[sh - 1]
        planes.append((jnp.where(row >= sh, zr, 0.0), jnp.where(row >= sh, zi, 0.0)))
    tab = jnp.stack([jnp.stack(p) for p in planes]).astype(F32)

    pg = S5_GROUPS // S5_PANELS
    eye = jnp.eye(pg, dtype=F32)

    def bdiag_in(m):
        m = m.reshape(S5_PANELS, pg, S5_STATE, S5_GROUP)
        return jnp.einsum('pgnc,gh->pgchn', m, eye).reshape(S5_PANELS, LANE, S5_PANEL_COLS)

    def bdiag_out(m):
        m = m.reshape(S5_PANELS, pg, S5_GROUP, S5_STATE)
        return jnp.einsum('pgcn,gh->pgnhc', m, eye).reshape(S5_PANELS, S5_PANEL_COLS, LANE)

    return (tab, bdiag_in(bbar_re).astype(MX), bdiag_in(bbar_im).astype(MX),
            bdiag_out(c_re).astype(MX), bdiag_out(-c_im).astype(MX))


def _s5_mixer(za, tabs, d_skip, w_glu, *, tm):
    S = za.shape[0]
    tab, bre, bim, cre, cim = tabs
    full = lambda shape: pl.BlockSpec(shape, lambda i: (0,) * len(shape))
    return pl.pallas_call(
        _s5_kernel,
        grid=(S // tm,),
        in_specs=[pl.BlockSpec((tm, BRANCH_W), lambda i: (i, ZA_S5)),
                  full(bre.shape), full(bim.shape), full(tab.shape), full(cre.shape),
                  full(cim.shape), full((1, BRANCH_W)), full(w_glu.shape)],
        out_specs=pl.BlockSpec((tm, BRANCH_W), lambda i: (i, 0)),
        out_shape=jax.ShapeDtypeStruct((S, BRANCH_W), MX),
        scratch_shapes=[pltpu.VMEM((tm, S5_COLS), F32), pltpu.VMEM((tm, S5_COLS), F32),
                        pltpu.VMEM((2, SUBLANE, S5_COLS), F32)],
        compiler_params=_params("arbitrary"),
        name="s5",
    )(za, bre, bim, tab, cre, cim, d_skip.reshape(1, BRANCH_W), w_glu)


def _pool_kernel(x_ref, w_ref, scale_ref, out_ref, ext_ref):
    tm = x_ref.shape[0]
    i = pl.program_id(0)

    @pl.when(i == 0)
    def _():
        ext_ref[0:POOL_HALO, :] = jnp.zeros((POOL_HALO, BRANCH_W), F32)

    x = x_ref[...]
    ext_ref[POOL_HALO:, :] = x
    t = i * tm + lax.broadcasted_iota(jnp.int32, (tm, 1), 0)
    for gi, w in enumerate(POOL_WINDOWS):
        cols = slice(gi * POOL_GROUP, (gi + 1) * POOL_GROUP)
        acc = x[:, cols]
        for d in range(1, w):
            acc = acc + ext_ref[POOL_HALO - d:POOL_HALO - d + tm, cols]
        count = jnp.minimum(t + 1, w).astype(F32)
        diff = (acc / count - x[:, cols]).astype(MX)
        y = jnp.dot(diff, w_ref[gi], preferred_element_type=F32)
        out_ref[:, cols] = (y * scale_ref[:, cols]).astype(MX)
    ext_ref[0:POOL_HALO, :] = x[tm - POOL_HALO:, :]


def _pool_mixer(za, w_pool, pool_scale, *, tm):
    S = za.shape[0]
    return pl.pallas_call(
        _pool_kernel,
        grid=(S // tm,),
        in_specs=[pl.BlockSpec((tm, BRANCH_W), lambda i: (i, ZA_POOL)),
                  pl.BlockSpec(w_pool.shape, lambda i: (0, 0, 0)),
                  pl.BlockSpec((1, BRANCH_W), lambda i: (0, 0))],
        out_specs=pl.BlockSpec((tm, BRANCH_W), lambda i: (i, 0)),
        out_shape=jax.ShapeDtypeStruct((S, BRANCH_W), MX),
        scratch_shapes=[pltpu.VMEM((tm + POOL_HALO, BRANCH_W), F32)],
        compiler_params=_params("arbitrary"),
        name="pool",
    )(za, w_pool, pool_scale.reshape(1, BRANCH_W))


def _rotary_tables(S):
    dh, C = RET_HEAD_DIM, RET_CHUNK
    angle = np.repeat(1.0 / (10000.0 ** np.linspace(0.0, 1.0, dh // 2)), 2)[None, :]
    within = np.arange(C, dtype=np.float64)[:, None] * angle
    start = (np.arange(S // C, dtype=np.float64) * C)[:, None] * angle
    tab = lambda a: jnp.asarray(a, dtype=F32)
    return (tab(np.cos(within)), tab(np.sin(within)),
            tab(np.cos(start)[:, None, :]), tab(np.sin(start)[:, None, :]))


def _ret_consts():
    H, dh, C = RET_HEADS, RET_HEAD_DIM, RET_CHUNK
    log_g = jnp.log(1.0 - 2.0 ** (-5.0 - jnp.arange(H, dtype=F32)))
    idx = jnp.arange(C, dtype=F32)
    rel = idx[:, None] - idx[None, :]
    decay = jnp.where(rel >= 0, jnp.exp(jnp.maximum(rel, 0.0)[None] * log_g[:, None, None]), 0.0)
    xi = jnp.exp((idx + 1.0)[None, :] * log_g[:, None])[..., None]
    zeta = jnp.exp((C - 1.0 - idx)[None, :] * log_g[:, None])[..., None]
    g_chunk = jnp.broadcast_to(jnp.exp(C * log_g)[:, None, None], (H, 1, dh))
    return decay, xi, zeta, g_chunk


def _rotate_pairs(x, even):
    n = x.shape[-1]
    return jnp.where(even, -pltpu.roll(x, n - 1, 1), pltpu.roll(x, 1, 1))


def _ret_kernel(q_ref, k_ref, v_ref, g_ref, ca_ref, sa_ref, cb_ref, sb_ref, decay_ref, xi_ref,
                zeta_ref, gch_ref, out_ref, state_ref):
    C, dh = RET_CHUNK, RET_HEAD_DIM

    @pl.when(pl.program_id(0) == 0)
    def _():
        state_ref[...] = jnp.zeros_like(state_ref)

    ca, sa, cb, sb = ca_ref[...], sa_ref[...], cb_ref[0], sb_ref[0]
    cos = ca * cb - sa * sb
    sin = sa * cb + ca * sb
    even = (lax.broadcasted_iota(jnp.int32, (C, dh), 1) % 2) == 0
    for hd in range(RET_HEADS):
        cols = slice(hd * dh, (hd + 1) * dh)
        q = q_ref[:, cols]
        k = k_ref[:, cols]
        vb = v_ref[:, cols].astype(MX)
        q = q * cos + _rotate_pairs(q, even) * sin
        k = (k * cos + _rotate_pairs(k, even) * sin) * dh ** -0.5
        qb = q.astype(MX)
        s = lax.dot_general(qb, k.astype(MX), (((1,), (1,)), ((), ())),
                            preferred_element_type=F32) * decay_ref[hd]
        inner = jnp.dot(s.astype(MX), vb, preferred_element_type=F32)
        state = state_ref[hd]
        cross = jnp.dot(qb, state.astype(MX), preferred_element_type=F32) * xi_ref[hd]
        kz = (k * zeta_ref[hd]).astype(MX)
        state_ref[hd] = state * gch_ref[hd] + lax.dot_general(
            kz, vb, (((0,), (0,)), ((), ())), preferred_element_type=F32)
        y = inner + cross
        y = y * lax.rsqrt(jnp.mean(y * y, axis=-1, keepdims=True) + NORM_EPS)
        g = g_ref[:, cols]
        out_ref[:, cols] = (g * _sigmoid(g) * y).astype(MX)


def _ret_mixer(za):
    S = za.shape[0]
    C = RET_CHUNK
    ca, sa, cb, sb = _rotary_tables(S)
    consts = (ca, sa, cb, sb) + _ret_consts()
    col = lambda c: pl.BlockSpec((C, BRANCH_W), lambda i: (i, c))
    full = lambda a: pl.BlockSpec(a.shape, lambda i: (0,) * a.ndim)
    start = lambda a: pl.BlockSpec((1,) + a.shape[1:], lambda i: (i, 0, 0))
    return pl.pallas_call(
        _ret_kernel,
        grid=(S // C,),
        in_specs=[col(ZA_RQ), col(ZA_RK), col(ZA_RV), col(ZA_RG), full(ca), full(sa), start(cb),
                  start(sb)] + [full(a) for a in consts[4:]],
        out_specs=pl.BlockSpec((C, BRANCH_W), lambda i: (i, 0)),
        out_shape=jax.ShapeDtypeStruct((S, BRANCH_W), MX),
        scratch_shapes=[pltpu.VMEM((RET_HEADS, RET_HEAD_DIM, RET_HEAD_DIM), F32)],
        compiler_params=_params("arbitrary"),
        name="retention",
    )(za, za, za, za, *consts)


NSA_R = NSA_HEADS // NSA_KV_GROUPS
NSA_TQ = 256
SLC_TQ = 512
WIN_TQ = 256
NSA_TK = 512
CMP_CH = 256
P_OFF = SUBLANE
NT_DIMS = (((1,), (1,)), ((), ()))


def _compress_kernel(x_ref, w_ref, out_ref):
    out_ref[0] = jnp.dot(x_ref[0], w_ref[...], preferred_element_type=F32).astype(out_ref.dtype)


def _compress_t_kernel(x_ref, wt_ref, out_ref):
    out_ref[0] = lax.dot_general(wt_ref[...], x_ref[0], NT_DIMS,
                                 preferred_element_type=F32).astype(out_ref.dtype)


def _compress(x, w, *, transposed):
    G, nc, kk = x.shape
    if transposed:
        kern, out_block = _compress_t_kernel, (1, w.shape[0], nc)
    else:
        kern, out_block = _compress_kernel, (1, nc, w.shape[1])
    return pl.pallas_call(
        kern,
        grid=(G,),
        in_specs=[pl.BlockSpec((1, nc, kk), lambda g: (g, 0, 0)),
                  pl.BlockSpec(w.shape, lambda g: (0, 0))],
        out_specs=pl.BlockSpec(out_block, lambda g: (g, 0, 0)),
        out_shape=jax.ShapeDtypeStruct((G,) + out_block[1:], x.dtype),
        compiler_params=_params("arbitrary"),
        name="nsa_compress",
    )(x, w)


def _group_queries(q_ref, g, extra_scale=1.0):
    hd = NSA_HEAD_DIM
    base = g * NSA_R * hd
    qs = [q_ref[:, base + r * hd:base + (r + 1) * hd] for r in range(NSA_R)]
    return (jnp.concatenate(qs, axis=0) * (hd ** -0.5 * extra_scale)).astype(MX)


def _cmp_select_kernel(q_ref, kc_ref, vct_ref, ocmp_ref, sel_ref, s_sc, p_sc, ot_sc):
    i = pl.program_id(0)
    tq, hd = NSA_TQ, NSA_HEAD_DIM
    nc = kc_ref.shape[1]
    ns = sel_ref.shape[2]
    ch = min(CMP_CH, nc)
    nch = jnp.minimum((tq // CMP_STRIDE * (i + 1) + ch - 1) // ch, nc // ch)
    t4 = i * tq + lax.broadcasted_iota(jnp.int32, (1, NSA_R * tq), 1) % tq
    t1 = i * tq + lax.broadcasted_iota(jnp.int32, (1, tq), 1)
    nrow = lax.broadcasted_iota(jnp.int32, (ch, 1), 0)

    def rows(c):
        return pl.ds(pl.multiple_of(c * ch, ch), ch)

    def visible(c):
        return ((c * ch + nrow) * CMP_STRIDE + (CMP_LEN - 1)) <= t4

    groups = range(NSA_KV_GROUPS)
    qgs = [_group_queries(q_ref, g) for g in groups]
    lanes = (1, NSA_R * tq)

    def scores(c, ms):
        vis = visible(c)
        out = []
        for g in groups:
            s = lax.dot_general(kc_ref[g, rows(c), :], qgs[g], NT_DIMS, preferred_element_type=F32)
            s = jnp.where(vis, s, NEG_INF)
            s_sc[g, rows(c), :] = s
            out.append(jnp.maximum(ms[g], jnp.max(s, axis=0, keepdims=True)))
        return tuple(out)

    ms = lax.fori_loop(0, nch, scores, tuple(jnp.full(lanes, NEG_INF, F32) for _ in groups))

    def expsum(c, ls):
        vis = visible(c)
        out = []
        for g in groups:
            e = jnp.where(vis, jnp.exp(s_sc[g, rows(c), :] - ms[g]), 0.0)
            s_sc[g, rows(c), :] = e
            out.append(ls[g] + jnp.sum(e, axis=0, keepdims=True))
        return tuple(out)

    ls = lax.fori_loop(0, nch, expsum, tuple(jnp.zeros(lanes, F32) for _ in groups))
    invs = [1.0 / jnp.maximum(l, 1e-30) for l in ls]
    p_sc[...] = jnp.zeros_like(p_sc)
    ot_sc[...] = jnp.zeros_like(ot_sc)

    def weigh(c, carry):
        for g in groups:
            p = s_sc[g, rows(c), :] * invs[g]
            psum = p[:, 0:tq] + p[:, tq:2 * tq] + p[:, 2 * tq:3 * tq] + p[:, 3 * tq:4 * tq]
            for w in range(tq // LANE):
                p_sc[g, w, pl.ds(pl.multiple_of(P_OFF + c * ch, SUBLANE), ch), :] = (
                    psum[:, w * LANE:(w + 1) * LANE])
            ot_sc[g] += jnp.dot(vct_ref[g, :, rows(c)], p.astype(MX), preferred_element_type=F32)
        return carry

    lax.fori_loop(0, nch, weigh, 0)
    for h in range(NSA_HEADS):
        g, r = divmod(h, NSA_R)
        o = ot_sc[g, :, r * tq:(r + 1) * tq].T
        ocmp_ref[:, h * hd:(h + 1) * hd] = o[:, :hd]

    ratio = SLC_BLOCK // CMP_STRIDE

    def pick(nr):
        jrow = lax.broadcasted_iota(jnp.int32, (nr, 1), 0)
        jfull = jnp.broadcast_to(jrow.astype(F32), (nr, tq))
        valid = jrow * SLC_BLOCK <= t1
        bonus = jnp.where((jrow == 0) | (jrow == t1 // SLC_BLOCK), FORCE_BONUS, 0.0)
        xs = []
        for g in groups:
            slabs = []
            for w in range(tq // LANE):
                imp = p_sc[g, w, pl.ds(P_OFF - 1, nr, stride=ratio), :]
                for k in range(ratio):
                    imp = imp + p_sc[g, w, pl.ds(P_OFF + k, nr, stride=ratio), :]
                slabs.append(imp)
            xs.append(jnp.where(valid, jnp.concatenate(slabs, axis=1) + bonus, NEG_INF))
        for _ in range(min(SLC_TOP_N, ns)):
            for g in groups:
                mx = jnp.max(xs[g], axis=0, keepdims=True)
                cand = jnp.where(xs[g] == mx, jfull, float(nr))
                xs[g] = jnp.where(cand == jnp.min(cand, axis=0, keepdims=True), -jnp.inf, xs[g])
        for g in groups:
            picked = jnp.where(xs[g] == -jnp.inf, 1.0, 0.0).T.astype(BF16)
            if nr < ns:
                picked = jnp.concatenate([picked, jnp.zeros((tq, ns - nr), BF16)], axis=1)
            sel_ref[g] = picked

    half = ns // 2
    if half >= SLC_TOP_N:
        early = (i + 1) * (tq // SLC_BLOCK) <= half

        @pl.when(early)
        def _():
            pick(half)

        @pl.when(jnp.logical_not(early))
        def _():
            pick(ns)
    else:
        pick(ns)


def _cmp_select(za, kc, vct):
    S = za.shape[0]
    G, nc, hd = kc.shape
    ns = S // SLC_BLOCK
    tq = NSA_TQ
    return pl.pallas_call(
        _cmp_select_kernel,
        grid=(S // tq,),
        in_specs=[pl.BlockSpec((tq, BRANCH_W), lambda i: (i, ZA_Q)),
                  pl.BlockSpec(kc.shape, lambda i: (0, 0, 0)),
                  pl.BlockSpec(vct.shape, lambda i: (0, 0, 0))],
        out_specs=[pl.BlockSpec((tq, BRANCH_W), lambda i: (i, 0)),
                   pl.BlockSpec((G, tq, ns), lambda i: (0, i, 0))],
        out_shape=[jax.ShapeDtypeStruct((S, BRANCH_W), F32),
                   jax.ShapeDtypeStruct((G, S, ns), BF16)],
        scratch_shapes=[pltpu.VMEM((G, nc, NSA_R * tq), F32),
                        pltpu.VMEM((G, tq // LANE, P_OFF + nc, LANE), F32),
                        pltpu.VMEM((G, LANE, NSA_R * tq), F32)],
        compiler_params=_params("parallel"),
        name="nsa_cmp_select",
    )(za, kc, vct)


SLC_MASK = 2.0 ** 100
LOG2E = math.log2(math.e)


def _slc_kernel(q_ref, kx_ref, vs_ref, sel_ref, out_ref, lhs_sc, m_sc, acc_sc, *, tk):
    i = pl.program_id(1)
    tq, hd = SLC_TQ, NSA_HEAD_DIM
    ksel = kx_ref.shape[2] - LANE
    bpt = tk // SLC_BLOCK
    qg = _group_queries(q_ref, 0, LOG2E)
    pad = jnp.zeros((NSA_R * tq, LANE - hd), MX)
    for h in range(sel_ref.shape[2] // ksel):
        unsel = (1.0 - sel_ref[0, :, h * ksel:(h + 1) * ksel].astype(F32)).astype(MX)
        lhs_sc[h] = jnp.concatenate([qg, pad, jnp.concatenate([unsel] * NSA_R, axis=0)], axis=1)
    t = i * tq + lax.broadcasted_iota(jnp.int32, (tq, 1), 0)
    lane = lax.broadcasted_iota(jnp.int32, (1, tk), 1)
    m_sc[...] = jnp.full_like(m_sc, NEG_INF)
    acc_sc[...] = jnp.zeros_like(acc_sc)

    def tile(kt, diagonal):
        k0 = pl.multiple_of(kt * tk, tk)
        s = lax.dot_general(lhs_sc[(kt * bpt) // ksel], kx_ref[0, pl.ds(k0, tk), :], NT_DIMS,
                            preferred_element_type=F32)
        if diagonal:
            hidden = jnp.where(k0 + lane <= t, 0.0, NEG_INF)
            s = s + jnp.concatenate([hidden] * NSA_R, axis=0)
        m_prev = m_sc[...]
        m_new = jnp.maximum(m_prev, jnp.max(s, axis=-1, keepdims=True))
        p = jnp.exp2(s - jnp.concatenate([m_new] * (tk // LANE), axis=1))
        acc_sc[...] = jnp.exp2(m_prev - m_new) * acc_sc[...] + jnp.dot(
            p.astype(MX), vs_ref[0, pl.ds(k0, tk), :], preferred_element_type=F32)
        m_sc[...] = m_new

    last = (i * tq) // tk

    def quad(j, carry):
        for u in range(4):
            tile(4 * j + u, False)
        return carry

    lax.fori_loop(0, last // 4, quad, 0)
    done = (last // 4) * 4

    @pl.when(last - done >= 2)
    def _():
        tile(done, False)
        tile(done + 1, False)

    @pl.when((last - done) % 2 == 1)
    def _():
        tile(last - 1, False)

    tile(last, True)
    acc = acc_sc[...]
    o = acc[:, :hd] / acc[:, hd:]
    for r in range(NSA_R):
        out_ref[:, r * hd:(r + 1) * hd] = o[r * tq:(r + 1) * tq, :]


def _slc_attention(za, ks, vs, sel):
    S = za.shape[0]
    G = ks.shape[0]
    ns = sel.shape[2]
    tq, hd = SLC_TQ, NSA_HEAD_DIM
    tk = min(NSA_TK, S)
    ksel = min(LANE, ns)
    own_block = (jnp.arange(S)[:, None] // SLC_BLOCK) % ksel == jnp.arange(ksel)[None, :]
    kx = jnp.concatenate([ks, jnp.zeros((G, S, LANE - hd), ks.dtype),
                          jnp.broadcast_to(jnp.where(own_block, -SLC_MASK, 0.0).astype(ks.dtype),
                                           (G, S, ksel))], axis=2)
    gw = NSA_R * hd
    return pl.pallas_call(
        functools.partial(_slc_kernel, tk=tk),
        grid=(G, S // tq),
        in_specs=[pl.BlockSpec((tq, gw), lambda g, i: (i, g)),
                  pl.BlockSpec((1, S, LANE + ksel), lambda g, i: (g, 0, 0)),
                  pl.BlockSpec((1, S, LANE), lambda g, i: (g, 0, 0)),
                  pl.BlockSpec((1, tq, ns), lambda g, i: (g, i, 0))],
        out_specs=pl.BlockSpec((tq, gw), lambda g, i: (i, g)),
        out_shape=jax.ShapeDtypeStruct((S, BRANCH_W), F32),
        scratch_shapes=[pltpu.VMEM((ns // ksel, NSA_R * tq, LANE + ksel), MX),
                        pltpu.VMEM((NSA_R * tq, LANE), F32), pltpu.VMEM((NSA_R * tq, LANE), F32)],
        compiler_params=_params("arbitrary", "arbitrary"),
        name="nsa_selected",
    )(za, kx, vs, sel)


def _win_kernel(q_ref, kw_ref, vw_ref, gate_ref, ocmp_ref, oslc_ref, out_ref):
    i = pl.program_id(1)
    tq, hd = WIN_TQ, NSA_HEAD_DIM
    span = tq + WIN
    qg = _group_queries(q_ref, 0, LOG2E)
    r0 = pl.multiple_of(i * tq, tq)
    t = i * tq + lax.broadcasted_iota(jnp.int32, (tq, 1), 0)
    pos = i * tq - WIN + lax.broadcasted_iota(jnp.int32, (1, span), 1)
    ok = (pos <= t) & (pos > t - WIN) & (pos >= 0)
    s = lax.dot_general(qg, kw_ref[0, pl.ds(r0, span), :], NT_DIMS, preferred_element_type=F32)
    s = s + jnp.concatenate([jnp.where(ok, 0.0, NEG_INF)] * NSA_R, axis=0)
    e = jnp.exp2(s - jnp.max(s, axis=-1, keepdims=True))
    acc = jnp.dot(e.astype(MX), vw_ref[0, pl.ds(r0, span), :], preferred_element_type=F32)
    ow = acc[:, :hd] / acc[:, hd:]
    sig = _sigmoid(gate_ref[...])
    for r in range(NSA_R):
        cols = slice(r * hd, (r + 1) * hd)
        out_ref[:, cols] = (sig[:, r:r + 1] * ocmp_ref[:, cols]
                            + sig[:, NSA_R + r:NSA_R + r + 1] * oslc_ref[:, cols]
                            + sig[:, 2 * NSA_R + r:2 * NSA_R + r + 1] * ow[r * tq:(r + 1) * tq, :]
                            ).astype(MX)


def _win_combine(za, kw, vw, ocmp, oslc):
    S = za.shape[0]
    G = kw.shape[0]
    tq, hd = WIN_TQ, NSA_HEAD_DIM
    gw = NSA_R * hd
    blk = lambda: pl.BlockSpec((tq, gw), lambda g, i: (i, g))
    return pl.pallas_call(
        _win_kernel,
        grid=(G, S // tq),
        in_specs=[blk(),
                  pl.BlockSpec((1, S + WIN, hd), lambda g, i: (g, 0, 0)),
                  pl.BlockSpec((1, S + WIN, LANE), lambda g, i: (g, 0, 0)),
                  pl.BlockSpec((tq, GATE_PAD), lambda g, i: (i, ZA_G_OFF // GATE_PAD + g)),
                  blk(), blk()],
        out_specs=blk(),
        out_shape=jax.ShapeDtypeStruct((S, BRANCH_W), MX),
        compiler_params=_params("arbitrary", "arbitrary"),
        name="nsa_window_combine",
    )(za, kw, vw, za, ocmp, oslc)


def _nsa_mixer(za, w_cmp_k, w_cmp_v):
    S = za.shape[0]
    G, hd = NSA_KV_GROUPS, NSA_HEAD_DIM
    nc = S // CMP_STRIDE
    kvz = lax.optimization_barrier(za[:, ZA_KV_OFF:ZA_KV_OFF + NSA_KV_COLS].astype(MX))

    def kv(idx):
        off = idx * G * hd
        return jnp.stack([kvz[:, off + g * hd:off + (g + 1) * hd] for g in range(G)])

    def blocks(x):
        half = x.reshape(G, nc, CMP_STRIDE * hd)
        nxt = jnp.concatenate([half[:, 1:], jnp.zeros_like(half[:, :1])], axis=1)
        return jnp.concatenate([half, nxt], axis=2).astype(MX)

    def with_ones(x):
        return jnp.concatenate([x, jnp.ones((G, S, LANE - hd), x.dtype)], axis=2).astype(MX)

    front = lambda x: jnp.pad(x, ((0, 0), (WIN, 0), (0, 0)))
    wk = w_cmp_k.reshape(CMP_LEN * hd, hd).astype(MX)
    wvt = jnp.pad(w_cmp_v.reshape(CMP_LEN * hd, hd).T, ((0, LANE - hd), (0, 0))).astype(MX)
    kc = _compress(blocks(kv(0)), wk, transposed=False)
    vct = _compress(blocks(kv(1)), wvt, transposed=True)
    ocmp, sel = _cmp_select(za, kc, vct)
    oslc = _slc_attention(za, kv(2).astype(MX), with_ones(kv(3)), sel)
    return _win_combine(za, front(kv(4).astype(MX)), front(with_ones(kv(5))), ocmp, oslc)


def _pack_w_in(w):
    seg = lambda off, n: w[:, off:off + n]
    R = NSA_HEADS // NSA_KV_GROUPS
    gates = []
    for g in range(NSA_KV_GROUPS):
        cols = [OFF_G + br * NSA_HEADS + g * R + r for br in range(3) for r in range(R)]
        gates.append(jnp.pad(w[:, jnp.array(cols)], ((0, 0), (0, GATE_PAD - len(cols)))))
    wa = jnp.concatenate([seg(OFF_Q, BRANCH_W), seg(OFF_S5, BRANCH_W), seg(OFF_POOL, BRANCH_W),
                          seg(OFF_RQ, BRANCH_W), seg(OFF_RK, BRANCH_W), seg(OFF_RV, BRANCH_W),
                          seg(OFF_RG, BRANCH_W), seg(OFF_KV, NSA_KV_COLS)] + gates, axis=1)
    return wa.astype(MX), w[:, OFF_MERGE:].astype(MX)


def kernel(x, p, g_mix, w_in, w_cmp_k, w_cmp_v, s5_a_re, s5_a_im, s5_log_dt, s5_b_re, s5_b_im,
           s5_c_re, s5_c_im, s5_d, s5_w_glu, pool_w, pool_scale, w_branch, w_out, g_mlp,
           w_mlp_up, w_mlp_down, w_ple_gate, w_ple_proj, g_final):
    B, S, D = x.shape
    assert B == 1 and D == D_MODEL
    depth = w_in.shape[0]
    h = x.reshape(S, D)
    for i in range(depth):
        wa, wm = _pack_w_in(w_in[i])
        za, u = _inproj(h, g_mix[i].reshape(1, D), wa, tm=1024, tn=1536)

        o_nsa = _nsa_mixer(za, w_cmp_k[i], w_cmp_v[i])
        s5_tabs = _s5_tables(s5_a_re[i], s5_a_im[i], s5_log_dt[i], s5_b_re[i], s5_b_im[i],
                             s5_c_re[i], s5_c_im[i])
        o_s5 = _s5_mixer(za, s5_tabs, s5_d[i], s5_w_glu[i].astype(MX), tm=512)
        o_pool = _pool_mixer(za, pool_w[i].astype(MX), pool_scale[i], tm=1024)
        o_ret = _ret_mixer(za)

        merged = _merge(u, (o_nsa, o_s5, o_pool, o_ret), wm, w_branch[i].astype(MX),
                        tm=1024, tn=512)
        h = _resid_mm(h, merged, w_out[i].astype(MX), tm=1024, tn=1024)
        h = _mlp(h, g_mlp[i].reshape(1, D), w_mlp_up, w_mlp_down, i, tm=1024, tf=512)
        h = _ple(h, p[i].reshape(S, PLE_DIM), w_ple_gate[i].astype(MX),
                 w_ple_proj[i].astype(MX), g_final.reshape(1, D), tm=512,
                 final_norm=(i == depth - 1))
    return h.reshape(B, S, D)
```

```python
import functools
import math

import jax
import jax.numpy as jnp
import numpy as np
from jax import lax
from jax.experimental import pallas as pl
from jax.experimental.pallas import tpu as pltpu

F32 = jnp.float32
BF16 = jnp.bfloat16
MX = BF16

D_MODEL = 2048
N_BRANCH = 4
BRANCH_W = D_MODEL // N_BRANCH
NSA_HEADS = 8
NSA_KV_GROUPS = 2
NSA_HEAD_DIM = BRANCH_W // NSA_HEADS
CMP_LEN = 32
CMP_STRIDE = 16
SLC_BLOCK = 64
SLC_TOP_N = 16
WIN = 512
Q_BLOCK = 128
S5_GROUP = 16
S5_GROUPS = BRANCH_W // S5_GROUP
S5_STATE = 64
S5_COLS = S5_GROUPS * S5_STATE
POOL_WINDOWS = (2, 4, 8, 16)
POOL_GROUP = BRANCH_W // len(POOL_WINDOWS)
POOL_HALO = 16
RET_HEADS = 4
RET_HEAD_DIM = BRANCH_W // RET_HEADS
RET_CHUNK = 128
D_FF = 4 * D_MODEL
PLE_DIM = 256
NORM_EPS = 1e-6
NEG_INF = -1e30
FORCE_BONUS = 1e4

NSA_KV_COLS = 6 * NSA_KV_GROUPS * NSA_HEAD_DIM
NSA_GATE_COLS = 3 * NSA_HEADS
OFF_Q = 0
OFF_KV = OFF_Q + BRANCH_W
OFF_G = OFF_KV + NSA_KV_COLS
OFF_S5 = OFF_G + NSA_GATE_COLS
OFF_POOL = OFF_S5 + BRANCH_W
OFF_RQ = OFF_POOL + BRANCH_W
OFF_RK = OFF_RQ + BRANCH_W
OFF_RV = OFF_RK + BRANCH_W
OFF_RG = OFF_RV + BRANCH_W
OFF_MERGE = OFF_RG + BRANCH_W

LANE = 128
SUBLANE = 8
GATE_PAD = LANE
ZA_Q, ZA_S5, ZA_POOL, ZA_RQ, ZA_RK, ZA_RV, ZA_RG = range(7)
ZA_KV_OFF = 7 * BRANCH_W
ZA_G_OFF = ZA_KV_OFF + NSA_KV_COLS
ZA_COLS = ZA_G_OFF + NSA_KV_GROUPS * GATE_PAD

VMEM_LIMIT = 56 * 1024 * 1024


def _params(*sem):
    return pltpu.CompilerParams(dimension_semantics=sem, vmem_limit_bytes=VMEM_LIMIT)


def _rms(x, g):
    return x * lax.rsqrt(jnp.mean(x * x, axis=-1, keepdims=True) + NORM_EPS) * g


def _sigmoid(x):
    return 1.0 / (1.0 + jnp.exp(-x))


def _inproj_kernel(h_ref, g_ref, w_ref, z_ref, u_ref):
    @pl.when(pl.program_id(1) == 0)
    def _():
        u_ref[...] = _rms(h_ref[...], g_ref[...]).astype(MX)

    z_ref[...] = jnp.dot(u_ref[...], w_ref[...], preferred_element_type=F32)


def _inproj(h, g, w, *, tm, tn):
    S, D = h.shape
    N = w.shape[1]
    return pl.pallas_call(
        _inproj_kernel,
        grid=(S // tm, N // tn),
        in_specs=[pl.BlockSpec((tm, D), lambda i, j: (i, 0)),
                  pl.BlockSpec((1, D), lambda i, j: (0, 0)),
                  pl.BlockSpec((D, tn), lambda i, j: (0, j))],
        out_specs=[pl.BlockSpec((tm, tn), lambda i, j: (i, j)),
                   pl.BlockSpec((tm, D), lambda i, j: (i, 0))],
        out_shape=[jax.ShapeDtypeStruct((S, N), F32), jax.ShapeDtypeStruct((S, D), MX)],
        compiler_params=_params("parallel", "arbitrary"),
        name="inproj",
    )(h, g, w)


def _merge_kernel(u_ref, o0_ref, o1_ref, o2_ref, o3_ref, wm0_ref, wm1_ref, wm2_ref, wm3_ref,
                  wb_ref, out_ref):
    u = u_ref[...]
    acc = None
    for j, (o_ref, wm_ref) in enumerate(((o0_ref, wm0_ref), (o1_ref, wm1_ref),
                                         (o2_ref, wm2_ref), (o3_ref, wm3_ref))):
        gate = _sigmoid(jnp.dot(u, wm_ref[...], preferred_element_type=F32))
        br = jnp.dot(o_ref[...], wb_ref[j], preferred_element_type=F32)
        acc = gate * br if acc is None else acc + gate * br
    out_ref[...] = acc.astype(MX)


def _merge(u, outs, wm, wb, *, tm, tn):
    S, D = u.shape
    nt = D // tn
    o_specs = [pl.BlockSpec((tm, BRANCH_W), lambda i, c: (i, 0)) for _ in range(N_BRANCH)]
    wm_specs = [pl.BlockSpec((D, tn), functools.partial(lambda i, c, j: (0, j * nt + c), j=j))
                for j in range(N_BRANCH)]
    return pl.pallas_call(
        _merge_kernel,
        grid=(S // tm, nt),
        in_specs=[pl.BlockSpec((tm, D), lambda i, c: (i, 0))] + o_specs + wm_specs
                 + [pl.BlockSpec((N_BRANCH, BRANCH_W, tn), lambda i, c: (0, 0, c))],
        out_specs=pl.BlockSpec((tm, tn), lambda i, c: (i, c)),
        out_shape=jax.ShapeDtypeStruct((S, D), MX),
        compiler_params=_params("parallel", "arbitrary"),
        name="merge",
    )(u, *outs, wm, wm, wm, wm, wb)


def _resid_mm_kernel(h_ref, x_ref, w_ref, out_ref):
    out_ref[...] = h_ref[...] + jnp.dot(x_ref[...], w_ref[...], preferred_element_type=F32)


def _resid_mm(h, x, w, *, tm, tn):
    S, D = h.shape
    K = x.shape[1]
    return pl.pallas_call(
        _resid_mm_kernel,
        grid=(S // tm, D // tn),
        in_specs=[pl.BlockSpec((tm, tn), lambda i, c: (i, c)),
                  pl.BlockSpec((tm, K), lambda i, c: (i, 0)),
                  pl.BlockSpec((K, tn), lambda i, c: (0, c))],
        out_specs=pl.BlockSpec((tm, tn), lambda i, c: (i, c)),
        out_shape=jax.ShapeDtypeStruct((S, D), F32),
        compiler_params=_params("parallel", "arbitrary"),
        name="resid_mm",
    )(h, x, w)


def _mlp_kernel(h_ref, g_ref, wu_ref, wd_ref, out_ref, v_ref):
    @pl.when(pl.program_id(1) == 0)
    def _():
        h = h_ref[...]
        v_ref[...] = _rms(h, g_ref[...]).astype(MX)
        out_ref[...] = h

    hid = jnp.maximum(jnp.dot(v_ref[...], wu_ref[...].astype(MX), preferred_element_type=F32), 0.0)
    out_ref[...] += jnp.dot((hid * hid).astype(MX), wd_ref[...].astype(MX),
                            preferred_element_type=F32)


def _mlp(h, g, wu, wd, layer, *, tm, tf):
    S, D = h.shape
    return pl.pallas_call(
        _mlp_kernel,
        grid=(S // tm, wu.shape[2] // tf),
        in_specs=[pl.BlockSpec((tm, D), lambda i, f: (i, 0)),
                  pl.BlockSpec((1, D), lambda i, f: (0, 0)),
                  pl.BlockSpec((None, D, tf), lambda i, f: (layer, 0, f)),
                  pl.BlockSpec((None, tf, D), lambda i, f: (layer, f, 0))],
        out_specs=pl.BlockSpec((tm, D), lambda i, f: (i, 0)),
        out_shape=jax.ShapeDtypeStruct((S, D), F32),
        scratch_shapes=[pltpu.VMEM((tm, D), MX)],
        compiler_params=_params("parallel", "arbitrary"),
        name="mlp",
    )(h, g, wu, wd)


def _ple_kernel(h_ref, p_ref, wg_ref, wp_ref, gf_ref, out_ref, *, final_norm):
    h = h_ref[...]
    gate = _sigmoid(jnp.dot(h.astype(MX), wg_ref[...], preferred_element_type=F32))
    emb = jnp.dot(p_ref[...].astype(MX), wp_ref[...], preferred_element_type=F32)
    y = h + gate * emb
    out_ref[...] = _rms(y, gf_ref[...]) if final_norm else y


def _ple(h, p, wg, wp, gf, *, tm, final_norm):
    S, D = h.shape
    P = p.shape[1]
    return pl.pallas_call(
        functools.partial(_ple_kernel, final_norm=final_norm),
        grid=(S // tm,),
        in_specs=[pl.BlockSpec((tm, D), lambda i: (i, 0)),
                  pl.BlockSpec((tm, P), lambda i: (i, 0)),
                  pl.BlockSpec((D, D), lambda i: (0, 0)),
                  pl.BlockSpec((P, D), lambda i: (0, 0)),
                  pl.BlockSpec((1, D), lambda i: (0, 0))],
        out_specs=pl.BlockSpec((tm, D), lambda i: (i, 0)),
        out_shape=jax.ShapeDtypeStruct((S, D), F32),
        compiler_params=_params("parallel"),
        name="ple",
    )(h, p, wg, wp, gf)


S5_PANELS = BRANCH_W // LANE
S5_PANEL_COLS = S5_COLS // S5_PANELS
T_A0, T_A1, T_A2, T_A4 = range(4)


def _cmul_add(xr, xi, ar, ai, br, bi):
    return xr + (ar * br - ai * bi), xi + (ar * bi + ai * br)


def _s5_kernel(u_ref, bre_ref, bim_ref, tab_ref, cre_ref, cim_ref, d_ref, wglu_ref, out_ref,
               hre_ref, him_ref, carry_ref):
    tm = u_ref.shape[0]

    @pl.when(pl.program_id(0) == 0)
    def _():
        carry_ref[...] = jnp.zeros_like(carry_ref)

    u = u_ref[...]
    ub = u.astype(MX)
    for k in range(S5_PANELS):
        uk = ub[:, k * LANE:(k + 1) * LANE]
        cols = slice(k * S5_PANEL_COLS, (k + 1) * S5_PANEL_COLS)
        hre_ref[:, cols] = jnp.dot(uk, bre_ref[k], preferred_element_type=F32)
        him_ref[:, cols] = jnp.dot(uk, bim_ref[k], preferred_element_type=F32)

    def body(r, carry):
        rows = pl.ds(pl.multiple_of(r * SUBLANE, SUBLANE), SUBLANE)
        for cb in range(S5_COLS // LANE):
            cols = slice(cb * LANE, (cb + 1) * LANE)
            xr, xi = _cmul_add(hre_ref[rows, cols], him_ref[rows, cols],
                               tab_ref[T_A0, 0, :, cols], tab_ref[T_A0, 1, :, cols],
                               carry_ref[0, :, cols], carry_ref[1, :, cols])
            for k, sh in ((T_A1, 1), (T_A2, 2), (T_A4, 4)):
                xr, xi = _cmul_add(xr, xi, tab_ref[k, 0, :, cols], tab_ref[k, 1, :, cols],
                                   pltpu.roll(xr, sh, 0), pltpu.roll(xi, sh, 0))
            hre_ref[rows, cols] = xr
            him_ref[rows, cols] = xi
            carry_ref[0, :, cols] = jnp.broadcast_to(xr[SUBLANE - 1:SUBLANE, :], xr.shape)
            carry_ref[1, :, cols] = jnp.broadcast_to(xi[SUBLANE - 1:SUBLANE, :], xi.shape)
        return carry

    lax.fori_loop(0, tm // SUBLANE, body, 0)

    ys = []
    for k in range(S5_PANELS):
        cols = slice(k * S5_PANEL_COLS, (k + 1) * S5_PANEL_COLS)
        ys.append(jnp.dot(hre_ref[:, cols].astype(MX), cre_ref[k], preferred_element_type=F32)
                  + jnp.dot(him_ref[:, cols].astype(MX), cim_ref[k], preferred_element_type=F32))
    y = jnp.concatenate(ys, axis=1) + d_ref[...] * u
    y = 0.5 * y * (1.0 + jnp.tanh(math.sqrt(2.0 / math.pi) * (y + 0.044715 * (y * y * y))))
    gl = jnp.dot(y.astype(MX), wglu_ref[...], preferred_element_type=F32)
    out_ref[...] = (gl[:, :BRANCH_W] * _sigmoid(gl[:, BRANCH_W:])).astype(MX)


def _s5_tables(a_re, a_im, log_dt, b_re, b_im, c_re, c_im):
    dt = jnp.exp(log_dt)[:, None]
    xr, xi = a_re * dt, a_im * dt

    pr = jnp.exp(xr) * jnp.cos(xi)
    pi = jnp.exp(xr) * jnp.sin(xi)
    den = a_re * a_re + a_im * a_im
    qr = ((pr - 1.0) * a_re + pi * a_im) / den
    qi = (pi * a_re - (pr - 1.0) * a_im) / den
    bbar_re = qr[..., None] * b_re - qi[..., None] * b_im
    bbar_im = qr[..., None] * b_im + qi[..., None] * b_re

    flat = lambda z: z.reshape(1, S5_COLS)
    powers = [(flat(pr), flat(pi))]
    for _ in range(3):
        zr, zi = powers[-1]
        powers.append((zr * flat(pr) - zi * flat(pi), zr * flat(pi) + zi * flat(pr)))
    row = jnp.arange(SUBLANE)[:, None]
    planes = [(jnp.where(row == 0, flat(pr), 0.0), jnp.where(row == 0, flat(pi), 0.0))]
    for sh in (1, 2, 4):
        zr, zi = powers[sh - 1]
        planes.append((jnp.where(row >= sh, zr, 0.0), jnp.where(row >= sh, zi, 0.0)))
    tab = jnp.stack([jnp.stack(p) for p in planes]).astype(F32)

    pg = S5_GROUPS // S5_PANELS
    eye = jnp.eye(pg, dtype=F32)

    def bdiag_in(m):
        m = m.reshape(S5_PANELS, pg, S5_STATE, S5_GROUP)
        return jnp.einsum('pgnc,gh->pgchn', m, eye).reshape(S5_PANELS, LANE, S5_PANEL_COLS)

    def bdiag_out(m):
        m = m.reshape(S5_PANELS, pg, S5_GROUP, S5_STATE)
        return jnp.einsum('pgcn,gh->pgnhc', m, eye).reshape(S5_PANELS, S5_PANEL_COLS, LANE)

    return (tab, bdiag_in(bbar_re).astype(MX), bdiag_in(bbar_im).astype(MX),
            bdiag_out(c_re).astype(MX), bdiag_out(-c_im).astype(MX))


def _s5_mixer(za, tabs, d_skip, w_glu, *, tm):
    S = za.shape[0]
    tab, bre, bim, cre, cim = tabs
    full = lambda shape: pl.BlockSpec(shape, lambda i: (0,) * len(shape))
    return pl.pallas_call(
        _s5_kernel,
        grid=(S // tm,),
        in_specs=[pl.BlockSpec((tm, BRANCH_W), lambda i: (i, ZA_S5)),
                  full(bre.shape), full(bim.shape), full(tab.shape), full(cre.shape),
                  full(cim.shape), full((1, BRANCH_W)), full(w_glu.shape)],
        out_specs=pl.BlockSpec((tm, BRANCH_W), lambda i: (i, 0)),
        out_shape=jax.ShapeDtypeStruct((S, BRANCH_W), MX),
        scratch_shapes=[pltpu.VMEM((tm, S5_COLS), F32), pltpu.VMEM((tm, S5_COLS), F32),
                        pltpu.VMEM((2, SUBLANE, S5_COLS), F32)],
        compiler_params=_params("arbitrary"),
        name="s5",
    )(za, bre, bim, tab, cre, cim, d_skip.reshape(1, BRANCH_W), w_glu)


def _pool_kernel(x_ref, w_ref, scale_ref, out_ref, ext_ref):
    tm = x_ref.shape[0]
    i = pl.program_id(0)

    @pl.when(i == 0)
    def _():
        ext_ref[0:POOL_HALO, :] = jnp.zeros((POOL_HALO, BRANCH_W), F32)

    x = x_ref[...]
    ext_ref[POOL_HALO:, :] = x
    t = i * tm + lax.broadcasted_iota(jnp.int32, (tm, 1), 0)
    for gi, w in enumerate(POOL_WINDOWS):
        cols = slice(gi * POOL_GROUP, (gi + 1) * POOL_GROUP)
        acc = x[:, cols]
        for d in range(1, w):
            acc = acc + ext_ref[POOL_HALO - d:POOL_HALO - d + tm, cols]
        count = jnp.minimum(t + 1, w).astype(F32)
        diff = (acc / count - x[:, cols]).astype(MX)
        y = jnp.dot(diff, w_ref[gi], preferred_element_type=F32)
        out_ref[:, cols] = (y * scale_ref[:, cols]).astype(MX)
    ext_ref[0:POOL_HALO, :] = x[tm - POOL_HALO:, :]


def _pool_mixer(za, w_pool, pool_scale, *, tm):
    S = za.shape[0]
    return pl.pallas_call(
        _pool_kernel,
        grid=(S // tm,),
        in_specs=[pl.BlockSpec((tm, BRANCH_W), lambda i: (i, ZA_POOL)),
                  pl.BlockSpec(w_pool.shape, lambda i: (0, 0, 0)),
                  pl.BlockSpec((1, BRANCH_W), lambda i: (0, 0))],
        out_specs=pl.BlockSpec((tm, BRANCH_W), lambda i: (i, 0)),
        out_shape=jax.ShapeDtypeStruct((S, BRANCH_W), MX),
        scratch_shapes=[pltpu.VMEM((tm + POOL_HALO, BRANCH_W), F32)],
        compiler_params=_params("arbitrary"),
        name="pool",
    )(za, w_pool, pool_scale.reshape(1, BRANCH_W))


def _rotary_tables(S):
    dh, C = RET_HEAD_DIM, RET_CHUNK
    angle = np.repeat(1.0 / (10000.0 ** np.linspace(0.0, 1.0, dh // 2)), 2)[None, :]
    within = np.arange(C, dtype=np.float64)[:, None] * angle
    start = (np.arange(S // C, dtype=np.float64) * C)[:, None] * angle
    tab = lambda a: jnp.asarray(a, dtype=F32)
    return (tab(np.cos(within)), tab(np.sin(within)),
            tab(np.cos(start)[:, None, :]), tab(np.sin(start)[:, None, :]))


def _ret_consts():
    H, dh, C = RET_HEADS, RET_HEAD_DIM, RET_CHUNK
    log_g = jnp.log(1.0 - 2.0 ** (-5.0 - jnp.arange(H, dtype=F32)))
    idx = jnp.arange(C, dtype=F32)
    rel = idx[:, None] - idx[None, :]
    decay = jnp.where(rel >= 0, jnp.exp(jnp.maximum(rel, 0.0)[None] * log_g[:, None, None]), 0.0)
    xi = jnp.exp((idx + 1.0)[None, :] * log_g[:, None])[..., None]
    zeta = jnp.exp((C - 1.0 - idx)[None, :] * log_g[:, None])[..., None]
    g_chunk = jnp.broadcast_to(jnp.exp(C * log_g)[:, None, None], (H, 1, dh))
    return decay, xi, zeta, g_chunk


def _rotate_pairs(x, even):
    n = x.shape[-1]
    return jnp.where(even, -pltpu.roll(x, n - 1, 1), pltpu.roll(x, 1, 1))


def _ret_kernel(q_ref, k_ref, v_ref, g_ref, ca_ref, sa_ref, cb_ref, sb_ref, decay_ref, xi_ref,
                zeta_ref, gch_ref, out_ref, state_ref):
    C, dh = RET_CHUNK, RET_HEAD_DIM

    @pl.when(pl.program_id(0) == 0)
    def _():
        state_ref[...] = jnp.zeros_like(state_ref)

    ca, sa, cb, sb = ca_ref[...], sa_ref[...], cb_ref[0], sb_ref[0]
    cos = ca * cb - sa * sb
    sin = sa * cb + ca * sb
    even = (lax.broadcasted_iota(jnp.int32, (C, dh), 1) % 2) == 0
    for hd in range(RET_HEADS):
        cols = slice(hd * dh, (hd + 1) * dh)
        q = q_ref[:, cols]
        k = k_ref[:, cols]
        vb = v_ref[:, cols].astype(MX)
        q = q * cos + _rotate_pairs(q, even) * sin
        k = (k * cos + _rotate_pairs(k, even) * sin) * dh ** -0.5
        qb = q.astype(MX)
        s = lax.dot_general(qb, k.astype(MX), (((1,), (1,)), ((), ())),
                            preferred_element_type=F32) * decay_ref[hd]
        inner = jnp.dot(s.astype(MX), vb, preferred_element_type=F32)
        state = state_ref[hd]
        cross = jnp.dot(qb, state.astype(MX), preferred_element_type=F32) * xi_ref[hd]
        kz = (k * zeta_ref[hd]).astype(MX)
        state_ref[hd] = state * gch_ref[hd] + lax.dot_general(
            kz, vb, (((0,), (0,)), ((), ())), preferred_element_type=F32)
        y = inner + cross
        y = y * lax.rsqrt(jnp.mean(y * y, axis=-1, keepdims=True) + NORM_EPS)
        g = g_ref[:, cols]
        out_ref[:, cols] = (g * _sigmoid(g) * y).astype(MX)


def _ret_mixer(za):
    S = za.shape[0]
    C = RET_CHUNK
    ca, sa, cb, sb = _rotary_tables(S)
    consts = (ca, sa, cb, sb) + _ret_consts()
    col = lambda c: pl.BlockSpec((C, BRANCH_W), lambda i: (i, c))
    full = lambda a: pl.BlockSpec(a.shape, lambda i: (0,) * a.ndim)
    start = lambda a: pl.BlockSpec((1,) + a.shape[1:], lambda i: (i, 0, 0))
    return pl.pallas_call(
        _ret_kernel,
        grid=(S // C,),
        in_specs=[col(ZA_RQ), col(ZA_RK), col(ZA_RV), col(ZA_RG), full(ca), full(sa), start(cb),
                  start(sb)] + [full(a) for a in consts[4:]],
        out_specs=pl.BlockSpec((C, BRANCH_W), lambda i: (i, 0)),
        out_shape=jax.ShapeDtypeStruct((S, BRANCH_W), MX),
        scratch_shapes=[pltpu.VMEM((RET_HEADS, RET_HEAD_DIM, RET_HEAD_DIM), F32)],
        compiler_params=_params("arbitrary"),
        name="retention",
    )(za, za, za, za, *consts)


NSA_R = NSA_HEADS // NSA_KV_GROUPS
NSA_TQ = 512
SLC_TQ = 512
WIN_TQ = 256
NSA_TK = 512
CMP_CH = 256
P_OFF = SUBLANE
NT_DIMS = (((1,), (1,)), ((), ()))


def _compress_kernel(x_ref, w_ref, out_ref):
    out_ref[0] = jnp.dot(x_ref[0], w_ref[...], preferred_element_type=F32).astype(out_ref.dtype)


def _compress_t_kernel(x_ref, wt_ref, out_ref):
    out_ref[0] = lax.dot_general(wt_ref[...], x_ref[0], NT_DIMS,
                                 preferred_element_type=F32).astype(out_ref.dtype)


def _compress(x, w, *, transposed):
    G, nc, kk = x.shape
    if transposed:
        kern, out_block = _compress_t_kernel, (1, w.shape[0], nc)
    else:
        kern, out_block = _compress_kernel, (1, nc, w.shape[1])
    return pl.pallas_call(
        kern,
        grid=(G,),
        in_specs=[pl.BlockSpec((1, nc, kk), lambda g: (g, 0, 0)),
                  pl.BlockSpec(w.shape, lambda g: (0, 0))],
        out_specs=pl.BlockSpec(out_block, lambda g: (g, 0, 0)),
        out_shape=jax.ShapeDtypeStruct((G,) + out_block[1:], x.dtype),
        compiler_params=_params("arbitrary"),
        name="nsa_compress",
    )(x, w)


def _group_queries(q_ref, g, extra_scale=1.0):
    hd = NSA_HEAD_DIM
    base = g * NSA_R * hd
    qs = [q_ref[:, base + r * hd:base + (r + 1) * hd] for r in range(NSA_R)]
    return (jnp.concatenate(qs, axis=0) * (hd ** -0.5 * extra_scale)).astype(MX)


def _cmp_select_kernel(q_ref, kc_ref, vct_ref, ocmp_ref, sel_ref, s_sc, p_sc, ot_sc):
    i = pl.program_id(0)
    tq, hd = NSA_TQ, NSA_HEAD_DIM
    nc = kc_ref.shape[1]
    ns = sel_ref.shape[2]
    ch = min(CMP_CH, nc)
    nch = jnp.minimum((tq // CMP_STRIDE * (i + 1) + ch - 1) // ch, nc // ch)
    t4 = i * tq + lax.broadcasted_iota(jnp.int32, (1, NSA_R * tq), 1) % tq
    t1 = i * tq + lax.broadcasted_iota(jnp.int32, (1, tq), 1)
    nrow = lax.broadcasted_iota(jnp.int32, (ch, 1), 0)

    def rows(c):
        return pl.ds(pl.multiple_of(c * ch, ch), ch)

    def visible(c):
        return ((c * ch + nrow) * CMP_STRIDE + (CMP_LEN - 1)) <= t4

    groups = range(NSA_KV_GROUPS)
    qgs = [_group_queries(q_ref, g) for g in groups]
    lanes = (1, NSA_R * tq)

    def scores(c, ms):
        vis = visible(c)
        out = []
        for g in groups:
            s = lax.dot_general(kc_ref[g, rows(c), :], qgs[g], NT_DIMS, preferred_element_type=F32)
            s = jnp.where(vis, s, NEG_INF)
            s_sc[g, rows(c), :] = s
            out.append(jnp.maximum(ms[g], jnp.max(s, axis=0, keepdims=True)))
        return tuple(out)

    ms = lax.fori_loop(0, nch, scores, tuple(jnp.full(lanes, NEG_INF, F32) for _ in groups))

    def expsum(c, ls):
        vis = visible(c)
        out = []
        for g in groups:
            e = jnp.where(vis, jnp.exp(s_sc[g, rows(c), :] - ms[g]), 0.0)
            s_sc[g, rows(c), :] = e
            out.append(ls[g] + jnp.sum(e, axis=0, keepdims=True))
        return tuple(out)

    ls = lax.fori_loop(0, nch, expsum, tuple(jnp.zeros(lanes, F32) for _ in groups))
    invs = [1.0 / jnp.maximum(l, 1e-30) for l in ls]
    p_sc[...] = jnp.zeros_like(p_sc)
    ot_sc[...] = jnp.zeros_like(ot_sc)

    def weigh(c, carry):
        for g in groups:
            p = s_sc[g, rows(c), :] * invs[g]
            psum = p[:, 0:tq] + p[:, tq:2 * tq] + p[:, 2 * tq:3 * tq] + p[:, 3 * tq:4 * tq]
            for w in range(tq // LANE):
                p_sc[g, w, pl.ds(pl.multiple_of(P_OFF + c * ch, SUBLANE), ch), :] = (
                    psum[:, w * LANE:(w + 1) * LANE])
            ot_sc[g] += jnp.dot(vct_ref[g, :, rows(c)], p.astype(MX), preferred_element_type=F32)
        return carry

    lax.fori_loop(0, nch, weigh, 0)
    for h in range(NSA_HEADS):
        g, r = divmod(h, NSA_R)
        o = ot_sc[g, :, r * tq:(r + 1) * tq].T
        ocmp_ref[:, h * hd:(h + 1) * hd] = o[:, :hd]

    ratio = SLC_BLOCK // CMP_STRIDE

    def pick(nr):
        jrow = lax.broadcasted_iota(jnp.int32, (nr, 1), 0)
        jfull = jnp.broadcast_to(jrow.astype(F32), (nr, tq))
        valid = jrow * SLC_BLOCK <= t1
        bonus = jnp.where((jrow == 0) | (jrow == t1 // SLC_BLOCK), FORCE_BONUS, 0.0)
        xs = []
        for g in groups:
            slabs = []
            for w in range(tq // LANE):
                imp = p_sc[g, w, pl.ds(P_OFF - 1, nr, stride=ratio), :]
                for k in range(ratio):
                    imp = imp + p_sc[g, w, pl.ds(P_OFF + k, nr, stride=ratio), :]
                slabs.append(imp)
            xs.append(jnp.where(valid, jnp.concatenate(slabs, axis=1) + bonus, NEG_INF))
        for _ in range(min(SLC_TOP_N, ns)):
            for g in groups:
                mx = jnp.max(xs[g], axis=0, keepdims=True)
                cand = jnp.where(xs[g] == mx, jfull, float(nr))
                xs[g] = jnp.where(cand == jnp.min(cand, axis=0, keepdims=True), -jnp.inf, xs[g])
        for g in groups:
            picked = jnp.where(xs[g] == -jnp.inf, 1.0, 0.0).T.astype(BF16)
            if nr < ns:
                picked = jnp.concatenate([picked, jnp.zeros((tq, ns - nr), BF16)], axis=1)
            sel_ref[g] = picked

    half = ns // 2
    if half >= SLC_TOP_N:
        early = (i + 1) * (tq // SLC_BLOCK) <= half

        @pl.when(early)
        def _():
            pick(half)

        @pl.when(jnp.logical_not(early))
        def _():
            pick(ns)
    else:
        pick(ns)


def _cmp_select(za, kc, vct):
    S = za.shape[0]
    G, nc, hd = kc.shape
    ns = S // SLC_BLOCK
    tq = NSA_TQ
    return pl.pallas_call(
        _cmp_select_kernel,
        grid=(S // tq,),
        in_specs=[pl.BlockSpec((tq, BRANCH_W), lambda i: (i, ZA_Q)),
                  pl.BlockSpec(kc.shape, lambda i: (0, 0, 0)),
                  pl.BlockSpec(vct.shape, lambda i: (0, 0, 0))],
        out_specs=[pl.BlockSpec((tq, BRANCH_W), lambda i: (i, 0)),
                   pl.BlockSpec((G, tq, ns), lambda i: (0, i, 0))],
        out_shape=[jax.ShapeDtypeStruct((S, BRANCH_W), F32),
                   jax.ShapeDtypeStruct((G, S, ns), BF16)],
        scratch_shapes=[pltpu.VMEM((G, nc, NSA_R * tq), F32),
                        pltpu.VMEM((G, tq // LANE, P_OFF + nc, LANE), F32),
                        pltpu.VMEM((G, LANE, NSA_R * tq), F32)],
        compiler_params=_params("parallel"),
        name="nsa_cmp_select",
    )(za, kc, vct)


SLC_MASK = 2.0 ** 100
LOG2E = math.log2(math.e)


def _slc_kernel(q_ref, kx_ref, vs_ref, sel_ref, out_ref, lhs_sc, m_sc, acc_sc, *, tk):
    i = pl.program_id(1)
    tq, hd = SLC_TQ, NSA_HEAD_DIM
    ksel = kx_ref.shape[2] - LANE
    bpt = tk // SLC_BLOCK
    qg = _group_queries(q_ref, 0, LOG2E)
    pad = jnp.zeros((NSA_R * tq, LANE - hd), MX)
    for h in range(sel_ref.shape[2] // ksel):
        unsel = (1.0 - sel_ref[0, :, h * ksel:(h + 1) * ksel].astype(F32)).astype(MX)
        lhs_sc[h] = jnp.concatenate([qg, pad, jnp.concatenate([unsel] * NSA_R, axis=0)], axis=1)
    t = i * tq + lax.broadcasted_iota(jnp.int32, (tq, 1), 0)
    lane = lax.broadcasted_iota(jnp.int32, (1, tk), 1)
    m_sc[...] = jnp.full_like(m_sc, NEG_INF)
    acc_sc[...] = jnp.zeros_like(acc_sc)

    def tile(kt, diagonal):
        k0 = pl.multiple_of(kt * tk, tk)
        s = lax.dot_general(lhs_sc[(kt * bpt) // ksel], kx_ref[0, pl.ds(k0, tk), :], NT_DIMS,
                            preferred_element_type=F32)
        if diagonal:
            hidden = jnp.where(k0 + lane <= t, 0.0, NEG_INF)
            s = s + jnp.concatenate([hidden] * NSA_R, axis=0)
        m_prev = m_sc[...]
        m_new = jnp.maximum(m_prev, jnp.max(s, axis=-1, keepdims=True))
        p = jnp.exp2(s - jnp.concatenate([m_new] * (tk // LANE), axis=1))
        acc_sc[...] = jnp.exp2(m_prev - m_new) * acc_sc[...] + jnp.dot(
            p.astype(MX), vs_ref[0, pl.ds(k0, tk), :], preferred_element_type=F32)
        m_sc[...] = m_new

    last = (i * tq) // tk

    def quad(j, carry):
        for u in range(4):
            tile(4 * j + u, False)
        return carry

    lax.fori_loop(0, last // 4, quad, 0)
    done = (last // 4) * 4

    @pl.when(last - done >= 2)
    def _():
        tile(done, False)
        tile(done + 1, False)

    @pl.when((last - done) % 2 == 1)
    def _():
        tile(last - 1, False)

    tile(last, True)
    acc = acc_sc[...]
    o = acc[:, :hd] / acc[:, hd:]
    for r in range(NSA_R):
        out_ref[:, r * hd:(r + 1) * hd] = o[r * tq:(r + 1) * tq, :]


def _slc_attention(za, ks, vs, sel):
    S = za.shape[0]
    G = ks.shape[0]
    ns = sel.shape[2]
    tq, hd = SLC_TQ, NSA_HEAD_DIM
    tk = min(NSA_TK, S)
    ksel = min(LANE, ns)
    own_block = (jnp.arange(S)[:, None] // SLC_BLOCK) % ksel == jnp.arange(ksel)[None, :]
    kx = jnp.concatenate([ks, jnp.zeros((G, S, LANE - hd), ks.dtype),
                          jnp.broadcast_to(jnp.where(own_block, -SLC_MASK, 0.0).astype(ks.dtype),
                                           (G, S, ksel))], axis=2)
    gw = NSA_R * hd
    return pl.pallas_call(
        functools.partial(_slc_kernel, tk=tk),
        grid=(G, S // tq),
        in_specs=[pl.BlockSpec((tq, gw), lambda g, i: (i, g)),
                  pl.BlockSpec((1, S, LANE + ksel), lambda g, i: (g, 0, 0)),
                  pl.BlockSpec((1, S, LANE), lambda g, i: (g, 0, 0)),
                  pl.BlockSpec((1, tq, ns), lambda g, i: (g, i, 0))],
        out_specs=pl.BlockSpec((tq, gw), lambda g, i: (i, g)),
        out_shape=jax.ShapeDtypeStruct((S, BRANCH_W), F32),
        scratch_shapes=[pltpu.VMEM((ns // ksel, NSA_R * tq, LANE + ksel), MX),
                        pltpu.VMEM((NSA_R * tq, LANE), F32), pltpu.VMEM((NSA_R * tq, LANE), F32)],
        compiler_params=_params("arbitrary", "arbitrary"),
        name="nsa_selected",
    )(za, kx, vs, sel)


def _win_kernel(q_ref, kw_ref, vw_ref, gate_ref, ocmp_ref, oslc_ref, out_ref):
    i = pl.program_id(1)
    tq, hd = WIN_TQ, NSA_HEAD_DIM
    span = tq + WIN
    qg = _group_queries(q_ref, 0, LOG2E)
    r0 = pl.multiple_of(i * tq, tq)
    t = i * tq + lax.broadcasted_iota(jnp.int32, (tq, 1), 0)
    pos = i * tq - WIN + lax.broadcasted_iota(jnp.int32, (1, span), 1)
    ok = (pos <= t) & (pos > t - WIN) & (pos >= 0)
    s = lax.dot_general(qg, kw_ref[0, pl.ds(r0, span), :], NT_DIMS, preferred_element_type=F32)
    s = s + jnp.concatenate([jnp.where(ok, 0.0, NEG_INF)] * NSA_R, axis=0)
    e = jnp.exp2(s - jnp.max(s, axis=-1, keepdims=True))
    acc = jnp.dot(e.astype(MX), vw_ref[0, pl.ds(r0, span), :], preferred_element_type=F32)
    ow = acc[:, :hd] / acc[:, hd:]
    sig = _sigmoid(gate_ref[...])
    for r in range(NSA_R):
        cols = slice(r * hd, (r + 1) * hd)
        out_ref[:, cols] = (sig[:, r:r + 1] * ocmp_ref[:, cols]
                            + sig[:, NSA_R + r:NSA_R + r + 1] * oslc_ref[:, cols]
                            + sig[:, 2 * NSA_R + r:2 * NSA_R + r + 1] * ow[r * tq:(r + 1) * tq, :]
                            ).astype(MX)


def _win_combine(za, kw, vw, ocmp, oslc):
    S = za.shape[0]
    G = kw.shape[0]
    tq, hd = WIN_TQ, NSA_HEAD_DIM
    gw = NSA_R * hd
    blk = lambda: pl.BlockSpec((tq, gw), lambda g, i: (i, g))
    return pl.pallas_call(
        _win_kernel,
        grid=(G, S // tq),
        in_specs=[blk(),
                  pl.BlockSpec((1, S + WIN, hd), lambda g, i: (g, 0, 0)),
                  pl.BlockSpec((1, S + WIN, LANE), lambda g, i: (g, 0, 0)),
                  pl.BlockSpec((tq, GATE_PAD), lambda g, i: (i, ZA_G_OFF // GATE_PAD + g)),
                  blk(), blk()],
        out_specs=blk(),
        out_shape=jax.ShapeDtypeStruct((S, BRANCH_W), MX),
        compiler_params=_params("arbitrary", "arbitrary"),
        name="nsa_window_combine",
    )(za, kw, vw, za, ocmp, oslc)


def _nsa_mixer(za, w_cmp_k, w_cmp_v):
    S = za.shape[0]
    G, hd = NSA_KV_GROUPS, NSA_HEAD_DIM
    nc = S // CMP_STRIDE
    kvz = lax.optimization_barrier(za[:, ZA_KV_OFF:ZA_KV_OFF + NSA_KV_COLS].astype(MX))

    def kv(idx):
        off = idx * G * hd
        return jnp.stack([kvz[:, off + g * hd:off + (g + 1) * hd] for g in range(G)])

    def blocks(x):
        half = x.reshape(G, nc, CMP_STRIDE * hd)
        nxt = jnp.concatenate([half[:, 1:], jnp.zeros_like(half[:, :1])], axis=1)
        return jnp.concatenate([half, nxt], axis=2).astype(MX)

    def with_ones(x):
        return jnp.concatenate([x, jnp.ones((G, S, LANE - hd), x.dtype)], axis=2).astype(MX)

    front = lambda x: jnp.pad(x, ((0, 0), (WIN, 0), (0, 0)))
    wk = w_cmp_k.reshape(CMP_LEN * hd, hd).astype(MX)
    wvt = jnp.pad(w_cmp_v.reshape(CMP_LEN * hd, hd).T, ((0, LANE - hd), (0, 0))).astype(MX)
    kc = _compress(blocks(kv(0)), wk, transposed=False)
    vct = _compress(blocks(kv(1)), wvt, transposed=True)
    ocmp, sel = _cmp_select(za, kc, vct)
    oslc = _slc_attention(za, kv(2).astype(MX), with_ones(kv(3)), sel)
    return _win_combine(za, front(kv(4).astype(MX)), front(with_ones(kv(5))), ocmp, oslc)


def _pack_w_in(w):
    seg = lambda off, n: w[:, off:off + n]
    R = NSA_HEADS // NSA_KV_GROUPS
    gates = []
    for g in range(NSA_KV_GROUPS):
        cols = [OFF_G + br * NSA_HEADS + g * R + r for br in range(3) for r in range(R)]
        gates.append(jnp.pad(w[:, jnp.array(cols)], ((0, 0), (0, GATE_PAD - len(cols)))))
    wa = jnp.concatenate([seg(OFF_Q, BRANCH_W), seg(OFF_S5, BRANCH_W), seg(OFF_POOL, BRANCH_W),
                          seg(OFF_RQ, BRANCH_W), seg(OFF_RK, BRANCH_W), seg(OFF_RV, BRANCH_W),
                          seg(OFF_RG, BRANCH_W), seg(OFF_KV, NSA_KV_COLS)] + gates, axis=1)
    return wa.astype(MX), w[:, OFF_MERGE:].astype(MX)


def kernel(x, p, g_mix, w_in, w_cmp_k, w_cmp_v, s5_a_re, s5_a_im, s5_log_dt, s5_b_re, s5_b_im,
           s5_c_re, s5_c_im, s5_d, s5_w_glu, pool_w, pool_scale, w_branch, w_out, g_mlp,
           w_mlp_up, w_mlp_down, w_ple_gate, w_ple_proj, g_final):
    B, S, D = x.shape
    assert B == 1 and D == D_MODEL
    depth = w_in.shape[0]
    h = x.reshape(S, D)
    for i in range(depth):
        wa, wm = _pack_w_in(w_in[i])
        za, u = _inproj(h, g_mix[i].reshape(1, D), wa, tm=1024, tn=1536)

        o_nsa = _nsa_mixer(za, w_cmp_k[i], w_cmp_v[i])
        s5_tabs = _s5_tables(s5_a_re[i], s5_a_im[i], s5_log_dt[i], s5_b_re[i], s5_b_im[i],
                             s5_c_re[i], s5_c_im[i])
        o_s5 = _s5_mixer(za, s5_tabs, s5_d[i], s5_w_glu[i].astype(MX), tm=512)
        o_pool = _pool_mixer(za, pool_w[i].astype(MX), pool_scale[i], tm=1024)
        o_ret = _ret_mixer(za)

        merged = _merge(u, (o_nsa, o_s5, o_pool, o_ret), wm, w_branch[i].astype(MX),
                        tm=1024, tn=512)
        h = _resid_mm(h, merged, w_out[i].astype(MX), tm=1024, tn=1024)
        h = _mlp(h, g_mlp[i].reshape(1, D), w_mlp_up, w_mlp_down, i, tm=1024, tf=512)
        h = _ple(h, p[i].reshape(S, PLE_DIM), w_ple_gate[i].astype(MX),
                 w_ple_proj[i].astype(MX), g_final.reshape(1, D), tm=512,
                 final_norm=(i == depth - 1))
    return h.reshape(B, S, D)
```

```python
import functools
import math

import jax
import jax.numpy as jnp
import numpy as np
from jax import lax
from jax.experimental import pallas as pl
from jax.experimental.pallas import tpu as pltpu

F32 = jnp.float32
BF16 = jnp.bfloat16
MX = BF16

D_MODEL = 2048
N_BRANCH = 4
BRANCH_W = D_MODEL // N_BRANCH
NSA_HEADS = 8
NSA_KV_GROUPS = 2
NSA_HEAD_DIM = BRANCH_W // NSA_HEADS
CMP_LEN = 32
CMP_STRIDE = 16
SLC_BLOCK = 64
SLC_TOP_N = 16
WIN = 512
Q_BLOCK = 128
S5_GROUP = 16
S5_GROUPS = BRANCH_W // S5_GROUP
S5_STATE = 64
S5_COLS = S5_GROUPS * S5_STATE
POOL_WINDOWS = (2, 4, 8, 16)
POOL_GROUP = BRANCH_W // len(POOL_WINDOWS)
POOL_HALO = 16
RET_HEADS = 4
RET_HEAD_DIM = BRANCH_W // RET_HEADS
RET_CHUNK = 128
D_FF = 4 * D_MODEL
PLE_DIM = 256
NORM_EPS = 1e-6
NEG_INF = -1e30
FORCE_BONUS = 1e4

NSA_KV_COLS = 6 * NSA_KV_GROUPS * NSA_HEAD_DIM
NSA_GATE_COLS = 3 * NSA_HEADS
OFF_Q = 0
OFF_KV = OFF_Q + BRANCH_W
OFF_G = OFF_KV + NSA_KV_COLS
OFF_S5 = OFF_G + NSA_GATE_COLS
OFF_POOL = OFF_S5 + BRANCH_W
OFF_RQ = OFF_POOL + BRANCH_W
OFF_RK = OFF_RQ + BRANCH_W
OFF_RV = OFF_RK + BRANCH_W
OFF_RG = OFF_RV + BRANCH_W
OFF_MERGE = OFF_RG + BRANCH_W

LANE = 128
SUBLANE = 8
GATE_PAD = LANE
ZA_Q, ZA_S5, ZA_POOL, ZA_RQ, ZA_RK, ZA_RV, ZA_RG = range(7)
ZA_KV_OFF = 7 * BRANCH_W
ZA_G_OFF = ZA_KV_OFF + NSA_KV_COLS
ZA_COLS = ZA_G_OFF + NSA_KV_GROUPS * GATE_PAD

VMEM_LIMIT = 56 * 1024 * 1024


def _params(*sem):
    return pltpu.CompilerParams(dimension_semantics=sem, vmem_limit_bytes=VMEM_LIMIT)


def _rms(x, g):
    return x * lax.rsqrt(jnp.mean(x * x, axis=-1, keepdims=True) + NORM_EPS) * g


def _sigmoid(x):
    return 1.0 / (1.0 + jnp.exp(-x))


def _inproj_kernel(h_ref, g_ref, w_ref, z_ref, u_ref):
    @pl.when(pl.program_id(1) == 0)
    def _():
        u_ref[...] = _rms(h_ref[...], g_ref[...]).astype(MX)

    z_ref[...] = jnp.dot(u_ref[...], w_ref[...], preferred_element_type=F32)


def _inproj(h, g, w, *, tm, tn):
    S, D = h.shape
    N = w.shape[1]
    return pl.pallas_call(
        _inproj_kernel,
        grid=(S // tm, N // tn),
        in_specs=[pl.BlockSpec((tm, D), lambda i, j: (i, 0)),
                  pl.BlockSpec((1, D), lambda i, j: (0, 0)),
                  pl.BlockSpec((D, tn), lambda i, j: (0, j))],
        out_specs=[pl.BlockSpec((tm, tn), lambda i, j: (i, j)),
                   pl.BlockSpec((tm, D), lambda i, j: (i, 0))],
        out_shape=[jax.ShapeDtypeStruct((S, N), F32), jax.ShapeDtypeStruct((S, D), MX)],
        compiler_params=_params("parallel", "arbitrary"),
        name="inproj",
    )(h, g, w)


def _merge_kernel(u_ref, o0_ref, o1_ref, o2_ref, o3_ref, wm0_ref, wm1_ref, wm2_ref, wm3_ref,
                  wb_ref, wo_ref, h_ref, out_ref):
    u = u_ref[...]
    acc = None
    for j, (o_ref, wm_ref) in enumerate(((o0_ref, wm0_ref), (o1_ref, wm1_ref),
                                         (o2_ref, wm2_ref), (o3_ref, wm3_ref))):
        gate = _sigmoid(jnp.dot(u, wm_ref[...], preferred_element_type=F32))
        br = jnp.dot(o_ref[...], wb_ref[j], preferred_element_type=F32)
        acc = gate * br if acc is None else acc + gate * br
    @pl.when(pl.program_id(1) == 0)
    def _():
        out_ref[...] = h_ref[...]

    out_ref[...] += jnp.dot(acc.astype(MX), wo_ref[...], preferred_element_type=F32)


def _merge_out(h, u, outs, wm, wb, wo, *, tm, tn):
    S, D = u.shape
    nt = D // tn
    o_specs = [pl.BlockSpec((tm, BRANCH_W), lambda i, c: (i, 0)) for _ in range(N_BRANCH)]
    wm_specs = [pl.BlockSpec((D, tn), functools.partial(lambda i, c, j: (0, j * nt + c), j=j))
                for j in range(N_BRANCH)]
    return pl.pallas_call(
        _merge_kernel,
        grid=(S // tm, nt),
        in_specs=[pl.BlockSpec((tm, D), lambda i, c: (i, 0))] + o_specs + wm_specs
                 + [pl.BlockSpec((N_BRANCH, BRANCH_W, tn), lambda i, c: (0, 0, c)),
                    pl.BlockSpec((tn, D), lambda i, c: (c, 0)),
                    pl.BlockSpec((tm, D), lambda i, c: (i, 0))],
        out_specs=pl.BlockSpec((tm, D), lambda i, c: (i, 0)),
        out_shape=jax.ShapeDtypeStruct((S, D), F32),
        compiler_params=_params("parallel", "arbitrary"),
        name="merge_out",
    )(u, *outs, wm, wm, wm, wm, wb, wo, h)


def _mlp_kernel(h_ref, g_ref, wu_ref, wd_ref, out_ref, v_ref):
    @pl.when(pl.program_id(1) == 0)
    def _():
        h = h_ref[...]
        v_ref[...] = _rms(h, g_ref[...]).astype(MX)
        out_ref[...] = h

    hid = jnp.maximum(jnp.dot(v_ref[...], wu_ref[...].astype(MX), preferred_element_type=F32), 0.0)
    out_ref[...] += jnp.dot((hid * hid).astype(MX), wd_ref[...].astype(MX),
                            preferred_element_type=F32)


def _mlp(h, g, wu, wd, layer, *, tm, tf):
    S, D = h.shape
    return pl.pallas_call(
        _mlp_kernel,
        grid=(S // tm, wu.shape[2] // tf),
        in_specs=[pl.BlockSpec((tm, D), lambda i, f: (i, 0)),
                  pl.BlockSpec((1, D), lambda i, f: (0, 0)),
                  pl.BlockSpec((None, D, tf), lambda i, f: (layer, 0, f)),
                  pl.BlockSpec((None, tf, D), lambda i, f: (layer, f, 0))],
        out_specs=pl.BlockSpec((tm, D), lambda i, f: (i, 0)),
        out_shape=jax.ShapeDtypeStruct((S, D), F32),
        scratch_shapes=[pltpu.VMEM((tm, D), MX)],
        compiler_params=_params("parallel", "arbitrary"),
        name="mlp",
    )(h, g, wu, wd)


def _ple_kernel(h_ref, p_ref, wg_ref, wp_ref, gf_ref, out_ref, *, final_norm):
    h = h_ref[...]
    gate = _sigmoid(jnp.dot(h.astype(MX), wg_ref[...], preferred_element_type=F32))
    emb = jnp.dot(p_ref[...].astype(MX), wp_ref[...], preferred_element_type=F32)
    y = h + gate * emb
    out_ref[...] = _rms(y, gf_ref[...]) if final_norm else y


def _ple(h, p, wg, wp, gf, *, tm, final_norm):
    S, D = h.shape
    P = p.shape[1]
    return pl.pallas_call(
        functools.partial(_ple_kernel, final_norm=final_norm),
        grid=(S // tm,),
        in_specs=[pl.BlockSpec((tm, D), lambda i: (i, 0)),
                  pl.BlockSpec((tm, P), lambda i: (i, 0)),
                  pl.BlockSpec((D, D), lambda i: (0, 0)),
                  pl.BlockSpec((P, D), lambda i: (0, 0)),
                  pl.BlockSpec((1, D), lambda i: (0, 0))],
        out_specs=pl.BlockSpec((tm, D), lambda i: (i, 0)),
        out_shape=jax.ShapeDtypeStruct((S, D), F32),
        compiler_params=_params("parallel"),
        name="ple",
    )(h, p, wg, wp, gf)


S5_PANELS = BRANCH_W // LANE
S5_PANEL_COLS = S5_COLS // S5_PANELS
T_A0, T_A1, T_A2, T_A4 = range(4)


def _cmul_add(xr, xi, ar, ai, br, bi):
    return xr + (ar * br - ai * bi), xi + (ar * bi + ai * br)


def _s5_kernel(u_ref, bre_ref, bim_ref, tab_ref, cre_ref, cim_ref, d_ref, wglu_ref, out_ref,
               hre_ref, him_ref, carry_ref):
    tm = u_ref.shape[0]

    @pl.when(pl.program_id(0) == 0)
    def _():
        carry_ref[...] = jnp.zeros_like(carry_ref)

    u = u_ref[...]
    ub = u.astype(MX)
    for k in range(S5_PANELS):
        uk = ub[:, k * LANE:(k + 1) * LANE]
        cols = slice(k * S5_PANEL_COLS, (k + 1) * S5_PANEL_COLS)
        hre_ref[:, cols] = jnp.dot(uk, bre_ref[k], preferred_element_type=F32)
        him_ref[:, cols] = jnp.dot(uk, bim_ref[k], preferred_element_type=F32)

    def body(r, carry):
        rows = pl.ds(pl.multiple_of(r * SUBLANE, SUBLANE), SUBLANE)
        for cb in range(S5_COLS // LANE):
            cols = slice(cb * LANE, (cb + 1) * LANE)
            xr, xi = _cmul_add(hre_ref[rows, cols], him_ref[rows, cols],
                               tab_ref[T_A0, 0, :, cols], tab_ref[T_A0, 1, :, cols],
                               carry_ref[0, :, cols], carry_ref[1, :, cols])
            for k, sh in ((T_A1, 1), (T_A2, 2), (T_A4, 4)):
                xr, xi = _cmul_add(xr, xi, tab_ref[k, 0, :, cols], tab_ref[k, 1, :, cols],
                                   pltpu.roll(xr, sh, 0), pltpu.roll(xi, sh, 0))
            hre_ref[rows, cols] = xr
            him_ref[rows, cols] = xi
            carry_ref[0, :, cols] = jnp.broadcast_to(xr[SUBLANE - 1:SUBLANE, :], xr.shape)
            carry_ref[1, :, cols] = jnp.broadcast_to(xi[SUBLANE - 1:SUBLANE, :], xi.shape)
        return carry

    lax.fori_loop(0, tm // SUBLANE, body, 0)

    ys = []
    for k in range(S5_PANELS):
        cols = slice(k * S5_PANEL_COLS, (k + 1) * S5_PANEL_COLS)
        ys.append(jnp.dot(hre_ref[:, cols].astype(MX), cre_ref[k], preferred_element_type=F32)
                  + jnp.dot(him_ref[:, cols].astype(MX), cim_ref[k], preferred_element_type=F32))
    y = jnp.concatenate(ys, axis=1) + d_ref[...] * u
    y = 0.5 * y * (1.0 + jnp.tanh(math.sqrt(2.0 / math.pi) * (y + 0.044715 * (y * y * y))))
    gl = jnp.dot(y.astype(MX), wglu_ref[...], preferred_element_type=F32)
    out_ref[...] = (gl[:, :BRANCH_W] * _sigmoid(gl[:, BRANCH_W:])).astype(MX)


def _s5_tables(a_re, a_im, log_dt, b_re, b_im, c_re, c_im):
    dt = jnp.exp(log_dt)[:, None]
    xr, xi = a_re * dt, a_im * dt

    pr = jnp.exp(xr) * jnp.cos(xi)
    pi = jnp.exp(xr) * jnp.sin(xi)
    den = a_re * a_re + a_im * a_im
    qr = ((pr - 1.0) * a_re + pi * a_im) / den
    qi = (pi * a_re - (pr - 1.0) * a_im) / den
    bbar_re = qr[..., None] * b_re - qi[..., None] * b_im
    bbar_im = qr[..., None] * b_im + qi[..., None] * b_re

    flat = lambda z: z.reshape(1, S5_COLS)
    powers = [(flat(pr), flat(pi))]
    for _ in range(3):
        zr, zi = powers[-1]
        powers.append((zr * flat(pr) - zi * flat(pi), zr * flat(pi) + zi * flat(pr)))
    row = jnp.arange(SUBLANE)[:, None]
    planes = [(jnp.where(row == 0, flat(pr), 0.0), jnp.where(row == 0, flat(pi), 0.0))]
    for sh in (1, 2, 4):
        zr, zi = powers[sh - 1]
        planes.append((jnp.where(row >= sh, zr, 0.0), jnp.where(row >= sh, zi, 0.0)))
    tab = jnp.stack([jnp.stack(p) for p in planes]).astype(F32)

    pg = S5_GROUPS // S5_PANELS
    eye = jnp.eye(pg, dtype=F32)

    def bdiag_in(m):
        m = m.reshape(S5_PANELS, pg, S5_STATE, S5_GROUP)
        return jnp.einsum('pgnc,gh->pgchn', m, eye).reshape(S5_PANELS, LANE, S5_PANEL_COLS)

    def bdiag_out(m):
        m = m.reshape(S5_PANELS, pg, S5_GROUP, S5_STATE)
        return jnp.einsum('pgcn,gh->pgnhc', m, eye).reshape(S5_PANELS, S5_PANEL_COLS, LANE)

    return (tab, bdiag_in(bbar_re).astype(MX), bdiag_in(bbar_im).astype(MX),
            bdiag_out(c_re).astype(MX), bdiag_out(-c_im).astype(MX))


def _s5_mixer(za, tabs, d_skip, w_glu, *, tm):
    S = za.shape[0]
    tab, bre, bim, cre, cim = tabs
    full = lambda shape: pl.BlockSpec(shape, lambda i: (0,) * len(shape))
    return pl.pallas_call(
        _s5_kernel,
        grid=(S // tm,),
        in_specs=[pl.BlockSpec((tm, BRANCH_W), lambda i: (i, ZA_S5)),
                  full(bre.shape), full(bim.shape), full(tab.shape), full(cre.shape),
                  full(cim.shape), full((1, BRANCH_W)), full(w_glu.shape)],
        out_specs=pl.BlockSpec((tm, BRANCH_W), lambda i: (i, 0)),
        out_shape=jax.ShapeDtypeStruct((S, BRANCH_W), MX),
        scratch_shapes=[pltpu.VMEM((tm, S5_COLS), F32), pltpu.VMEM((tm, S5_COLS), F32),
                        pltpu.VMEM((2, SUBLANE, S5_COLS), F32)],
        compiler_params=_params("arbitrary"),
        name="s5",
    )(za, bre, bim, tab, cre, cim, d_skip.reshape(1, BRANCH_W), w_glu)


def _pool_kernel(x_ref, w_ref, scale_ref, out_ref, ext_ref):
    tm = x_ref.shape[0]
    i = pl.program_id(0)

    @pl.when(i == 0)
    def _():
        ext_ref[0:POOL_HALO, :] = jnp.zeros((POOL_HALO, BRANCH_W), F32)

    x = x_ref[...]
    ext_ref[POOL_HALO:, :] = x
    t = i * tm + lax.broadcasted_iota(jnp.int32, (tm, 1), 0)
    for gi, w in enumerate(POOL_WINDOWS):
        cols = slice(gi * POOL_GROUP, (gi + 1) * POOL_GROUP)
        acc = x[:, cols]
        for d in range(1, w):
            acc = acc + ext_ref[POOL_HALO - d:POOL_HALO - d + tm, cols]
        count = jnp.minimum(t + 1, w).astype(F32)
        diff = (acc / count - x[:, cols]).astype(MX)
        y = jnp.dot(diff, w_ref[gi], preferred_element_type=F32)
        out_ref[:, cols] = (y * scale_ref[:, cols]).astype(MX)
    ext_ref[0:POOL_HALO, :] = x[tm - POOL_HALO:, :]


def _pool_mixer(za, w_pool, pool_scale, *, tm):
    S = za.shape[0]
    return pl.pallas_call(
        _pool_kernel,
        grid=(S // tm,),
        in_specs=[pl.BlockSpec((tm, BRANCH_W), lambda i: (i, ZA_POOL)),
                  pl.BlockSpec(w_pool.shape, lambda i: (0, 0, 0)),
                  pl.BlockSpec((1, BRANCH_W), lambda i: (0, 0))],
        out_specs=pl.BlockSpec((tm, BRANCH_W), lambda i: (i, 0)),
        out_shape=jax.ShapeDtypeStruct((S, BRANCH_W), MX),
        scratch_shapes=[pltpu.VMEM((tm + POOL_HALO, BRANCH_W), F32)],
        compiler_params=_params("arbitrary"),
        name="pool",
    )(za, w_pool, pool_scale.reshape(1, BRANCH_W))


def _rotary_tables(S):
    dh, C = RET_HEAD_DIM, RET_CHUNK
    angle = np.repeat(1.0 / (10000.0 ** np.linspace(0.0, 1.0, dh // 2)), 2)[None, :]
    within = np.arange(C, dtype=np.float64)[:, None] * angle
    start = (np.arange(S // C, dtype=np.float64) * C)[:, None] * angle
    tab = lambda a: jnp.asarray(a, dtype=F32)
    return (tab(np.cos(within)), tab(np.sin(within)),
            tab(np.cos(start)[:, None, :]), tab(np.sin(start)[:, None, :]))


def _ret_consts():
    H, dh, C = RET_HEADS, RET_HEAD_DIM, RET_CHUNK
    log_g = jnp.log(1.0 - 2.0 ** (-5.0 - jnp.arange(H, dtype=F32)))
    idx = jnp.arange(C, dtype=F32)
    rel = idx[:, None] - idx[None, :]
    decay = jnp.where(rel >= 0, jnp.exp(jnp.maximum(rel, 0.0)[None] * log_g[:, None, None]), 0.0)
    xi = jnp.exp((idx + 1.0)[None, :] * log_g[:, None])[..., None]
    zeta = jnp.exp((C - 1.0 - idx)[None, :] * log_g[:, None])[..., None]
    g_chunk = jnp.broadcast_to(jnp.exp(C * log_g)[:, None, None], (H, 1, dh))
    return decay, xi, zeta, g_chunk


def _rotate_pairs(x, even):
    n = x.shape[-1]
    return jnp.where(even, -pltpu.roll(x, n - 1, 1), pltpu.roll(x, 1, 1))


def _ret_kernel(q_ref, k_ref, v_ref, g_ref, ca_ref, sa_ref, cb_ref, sb_ref, decay_ref, xi_ref,
                zeta_ref, gch_ref, out_ref, state_ref):
    C, dh = RET_CHUNK, RET_HEAD_DIM

    @pl.when(pl.program_id(0) == 0)
    def _():
        state_ref[...] = jnp.zeros_like(state_ref)

    ca, sa, cb, sb = ca_ref[...], sa_ref[...], cb_ref[0], sb_ref[0]
    cos = ca * cb - sa * sb
    sin = sa * cb + ca * sb
    even = (lax.broadcasted_iota(jnp.int32, (C, dh), 1) % 2) == 0
    for hd in range(RET_HEADS):
        cols = slice(hd * dh, (hd + 1) * dh)
        q = q_ref[:, cols]
        k = k_ref[:, cols]
        vb = v_ref[:, cols].astype(MX)
        q = q * cos + _rotate_pairs(q, even) * sin
        k = (k * cos + _rotate_pairs(k, even) * sin) * dh ** -0.5
        qb = q.astype(MX)
        s = lax.dot_general(qb, k.astype(MX), (((1,), (1,)), ((), ())),
                            preferred_element_type=F32) * decay_ref[hd]
        inner = jnp.dot(s.astype(MX), vb, preferred_element_type=F32)
        state = state_ref[hd]
        cross = jnp.dot(qb, state.astype(MX), preferred_element_type=F32) * xi_ref[hd]
        kz = (k * zeta_ref[hd]).astype(MX)
        state_ref[hd] = state * gch_ref[hd] + lax.dot_general(
            kz, vb, (((0,), (0,)), ((), ())), preferred_element_type=F32)
        y = inner + cross
        y = y * lax.rsqrt(jnp.mean(y * y, axis=-1, keepdims=True) + NORM_EPS)
        g = g_ref[:, cols]
        out_ref[:, cols] = (g * _sigmoid(g) * y).astype(MX)


def _ret_mixer(za):
    S = za.shape[0]
    C = RET_CHUNK
    ca, sa, cb, sb = _rotary_tables(S)
    consts = (ca, sa, cb, sb) + _ret_consts()
    col = lambda c: pl.BlockSpec((C, BRANCH_W), lambda i: (i, c))
    full = lambda a: pl.BlockSpec(a.shape, lambda i: (0,) * a.ndim)
    start = lambda a: pl.BlockSpec((1,) + a.shape[1:], lambda i: (i, 0, 0))
    return pl.pallas_call(
        _ret_kernel,
        grid=(S // C,),
        in_specs=[col(ZA_RQ), col(ZA_RK), col(ZA_RV), col(ZA_RG), full(ca), full(sa), start(cb),
                  start(sb)] + [full(a) for a in consts[4:]],
        out_specs=pl.BlockSpec((C, BRANCH_W), lambda i: (i, 0)),
        out_shape=jax.ShapeDtypeStruct((S, BRANCH_W), MX),
        scratch_shapes=[pltpu.VMEM((RET_HEADS, RET_HEAD_DIM, RET_HEAD_DIM), F32)],
        compiler_params=_params("arbitrary"),
        name="retention",
    )(za, za, za, za, *consts)


NSA_R = NSA_HEADS // NSA_KV_GROUPS
NSA_TQ = 512
SLC_TQ = 512
WIN_TQ = 256
NSA_TK = 512
CMP_CH = 256
P_OFF = SUBLANE
NT_DIMS = (((1,), (1,)), ((), ()))


def _compress_kernel(x_ref, w_ref, out_ref):
    out_ref[0] = jnp.dot(x_ref[0], w_ref[...], preferred_element_type=F32).astype(out_ref.dtype)


def _compress_t_kernel(x_ref, wt_ref, out_ref):
    out_ref[0] = lax.dot_general(wt_ref[...], x_ref[0], NT_DIMS,
                                 preferred_element_type=F32).astype(out_ref.dtype)


def _compress(x, w, *, transposed):
    G, nc, kk = x.shape
    if transposed:
        kern, out_block = _compress_t_kernel, (1, w.shape[0], nc)
    else:
        kern, out_block = _compress_kernel, (1, nc, w.shape[1])
    return pl.pallas_call(
        kern,
        grid=(G,),
        in_specs=[pl.BlockSpec((1, nc, kk), lambda g: (g, 0, 0)),
                  pl.BlockSpec(w.shape, lambda g: (0, 0))],
        out_specs=pl.BlockSpec(out_block, lambda g: (g, 0, 0)),
        out_shape=jax.ShapeDtypeStruct((G,) + out_block[1:], x.dtype),
        compiler_params=_params("arbitrary"),
        name="nsa_compress",
    )(x, w)


def _group_queries(q_ref, g, extra_scale=1.0):
    hd = NSA_HEAD_DIM
    base = g * NSA_R * hd
    qs = [q_ref[:, base + r * hd:base + (r + 1) * hd] for r in range(NSA_R)]
    return (jnp.concatenate(qs, axis=0) * (hd ** -0.5 * extra_scale)).astype(MX)


def _cmp_select_kernel(q_ref, kc_ref, vct_ref, ocmp_ref, sel_ref, s_sc, p_sc, ot_sc):
    i = pl.program_id(0)
    tq, hd = NSA_TQ, NSA_HEAD_DIM
    nc = kc_ref.shape[1]
    ns = sel_ref.shape[2]
    ch = min(CMP_CH, nc)
    nch = jnp.minimum((tq // CMP_STRIDE * (i + 1) + ch - 1) // ch, nc // ch)
    t4 = i * tq + lax.broadcasted_iota(jnp.int32, (1, NSA_R * tq), 1) % tq
    t1 = i * tq + lax.broadcasted_iota(jnp.int32, (1, tq), 1)
    nrow = lax.broadcasted_iota(jnp.int32, (ch, 1), 0)

    def rows(c):
        return pl.ds(pl.multiple_of(c * ch, ch), ch)

    def visible(c):
        return ((c * ch + nrow) * CMP_STRIDE + (CMP_LEN - 1)) <= t4

    groups = range(NSA_KV_GROUPS)
    qgs = [_group_queries(q_ref, g) for g in groups]
    lanes = (1, NSA_R * tq)

    def scores(c, ms):
        vis = visible(c)
        out = []
        for g in groups:
            s = lax.dot_general(kc_ref[g, rows(c), :], qgs[g], NT_DIMS, preferred_element_type=F32)
            s = jnp.where(vis, s, NEG_INF)
            s_sc[g, rows(c), :] = s
            out.append(jnp.maximum(ms[g], jnp.max(s, axis=0, keepdims=True)))
        return tuple(out)

    ms = lax.fori_loop(0, nch, scores, tuple(jnp.full(lanes, NEG_INF, F32) for _ in groups))

    def expsum(c, ls):
        vis = visible(c)
        out = []
        for g in groups:
            e = jnp.where(vis, jnp.exp(s_sc[g, rows(c), :] - ms[g]), 0.0)
            s_sc[g, rows(c), :] = e
            out.append(ls[g] + jnp.sum(e, axis=0, keepdims=True))
        return tuple(out)

    ls = lax.fori_loop(0, nch, expsum, tuple(jnp.zeros(lanes, F32) for _ in groups))
    invs = [1.0 / jnp.maximum(l, 1e-30) for l in ls]
    p_sc[...] = jnp.zeros_like(p_sc)
    ot_sc[...] = jnp.zeros_like(ot_sc)

    def weigh(c, carry):
        for g in groups:
            p = s_sc[g, rows(c), :] * invs[g]
            psum = p[:, 0:tq] + p[:, tq:2 * tq] + p[:, 2 * tq:3 * tq] + p[:, 3 * tq:4 * tq]
            for w in range(tq // LANE):
                p_sc[g, w, pl.ds(pl.multiple_of(P_OFF + c * ch, SUBLANE), ch), :] = (
                    psum[:, w * LANE:(w + 1) * LANE])
            ot_sc[g] += jnp.dot(vct_ref[g, :, rows(c)], p.astype(MX), preferred_element_type=F32)
        return carry

    lax.fori_loop(0, nch, weigh, 0)
    for h in range(NSA_HEADS):
        g, r = divmod(h, NSA_R)
        o = ot_sc[g, :, r * tq:(r + 1) * tq].T
        ocmp_ref[:, h * hd:(h + 1) * hd] = o[:, :hd]

    ratio = SLC_BLOCK // CMP_STRIDE

    def pick(nr):
        jrow = lax.broadcasted_iota(jnp.int32, (nr, 1), 0)
        jfull = jnp.broadcast_to(jrow.astype(F32), (nr, tq))
        valid = jrow * SLC_BLOCK <= t1
        bonus = jnp.where((jrow == 0) | (jrow == t1 // SLC_BLOCK), FORCE_BONUS, 0.0)
        xs = []
        for g in groups:
            slabs = []
            for w in range(tq // LANE):
                imp = p_sc[g, w, pl.ds(P_OFF - 1, nr, stride=ratio), :]
                for k in range(ratio):
                    imp = imp + p_sc[g, w, pl.ds(P_OFF + k, nr, stride=ratio), :]
                slabs.append(imp)
            xs.append(jnp.where(valid, jnp.concatenate(slabs, axis=1) + bonus, NEG_INF))
        for _ in range(min(SLC_TOP_N, ns)):
            for g in groups:
                mx = jnp.max(xs[g], axis=0, keepdims=True)
                cand = jnp.where(xs[g] == mx, jfull, float(nr))
                xs[g] = jnp.where(cand == jnp.min(cand, axis=0, keepdims=True), -jnp.inf, xs[g])
        for g in groups:
            picked = jnp.where(xs[g] == -jnp.inf, 1.0, 0.0).T.astype(BF16)
            if nr < ns:
                picked = jnp.concatenate([picked, jnp.zeros((tq, ns - nr), BF16)], axis=1)
            sel_ref[g] = picked

    half = ns // 2
    if half >= SLC_TOP_N:
        early = (i + 1) * (tq // SLC_BLOCK) <= half

        @pl.when(early)
        def _():
            pick(half)

        @pl.when(jnp.logical_not(early))
        def _():
            pick(ns)
    else:
        pick(ns)


def _cmp_select(za, kc, vct):
    S = za.shape[0]
    G, nc, hd = kc.shape
    ns = S // SLC_BLOCK
    tq = NSA_TQ
    return pl.pallas_call(
        _cmp_select_kernel,
        grid=(S // tq,),
        in_specs=[pl.BlockSpec((tq, BRANCH_W), lambda i: (i, ZA_Q)),
                  pl.BlockSpec(kc.shape, lambda i: (0, 0, 0)),
                  pl.BlockSpec(vct.shape, lambda i: (0, 0, 0))],
        out_specs=[pl.BlockSpec((tq, BRANCH_W), lambda i: (i, 0)),
                   pl.BlockSpec((G, tq, ns), lambda i: (0, i, 0))],
        out_shape=[jax.ShapeDtypeStruct((S, BRANCH_W), F32),
                   jax.ShapeDtypeStruct((G, S, ns), BF16)],
        scratch_shapes=[pltpu.VMEM((G, nc, NSA_R * tq), F32),
                        pltpu.VMEM((G, tq // LANE, P_OFF + nc, LANE), F32),
                        pltpu.VMEM((G, LANE, NSA_R * tq), F32)],
        compiler_params=_params("parallel"),
        name="nsa_cmp_select",
    )(za, kc, vct)


SLC_MASK = 2.0 ** 100
LOG2E = math.log2(math.e)


def _slc_kernel(q_ref, kx_ref, vs_ref, sel_ref, out_ref, lhs_sc, m_sc, acc_sc, *, tk):
    i = pl.program_id(1)
    tq, hd = SLC_TQ, NSA_HEAD_DIM
    ksel = kx_ref.shape[2] - LANE
    bpt = tk // SLC_BLOCK
    qg = _group_queries(q_ref, 0, LOG2E)
    pad = jnp.zeros((NSA_R * tq, LANE - hd), MX)
    for h in range(sel_ref.shape[2] // ksel):
        unsel = (1.0 - sel_ref[0, :, h * ksel:(h + 1) * ksel].astype(F32)).astype(MX)
        lhs_sc[h] = jnp.concatenate([qg, pad, jnp.concatenate([unsel] * NSA_R, axis=0)], axis=1)
    t = i * tq + lax.broadcasted_iota(jnp.int32, (tq, 1), 0)
    lane = lax.broadcasted_iota(jnp.int32, (1, tk), 1)
    m_sc[...] = jnp.full_like(m_sc, NEG_INF)
    acc_sc[...] = jnp.zeros_like(acc_sc)

    def tile(kt, diagonal):
        k0 = pl.multiple_of(kt * tk, tk)
        s = lax.dot_general(lhs_sc[(kt * bpt) // ksel], kx_ref[0, pl.ds(k0, tk), :], NT_DIMS,
                            preferred_element_type=F32)
        if diagonal:
            hidden = jnp.where(k0 + lane <= t, 0.0, NEG_INF)
            s = s + jnp.concatenate([hidden] * NSA_R, axis=0)
        m_prev = m_sc[...]
        m_new = jnp.maximum(m_prev, jnp.max(s, axis=-1, keepdims=True))
        p = jnp.exp2(s - jnp.concatenate([m_new] * (tk // LANE), axis=1))
        acc_sc[...] = jnp.exp2(m_prev - m_new) * acc_sc[...] + jnp.dot(
            p.astype(MX), vs_ref[0, pl.ds(k0, tk), :], preferred_element_type=F32)
        m_sc[...] = m_new

    last = (i * tq) // tk

    def quad(j, carry):
        for u in range(4):
            tile(4 * j + u, False)
        return carry

    lax.fori_loop(0, last // 4, quad, 0)
    done = (last // 4) * 4

    @pl.when(last - done >= 2)
    def _():
        tile(done, False)
        tile(done + 1, False)

    @pl.when((last - done) % 2 == 1)
    def _():
        tile(last - 1, False)

    tile(last, True)
    acc = acc_sc[...]
    o = acc[:, :hd] / acc[:, hd:]
    for r in range(NSA_R):
        out_ref[:, r * hd:(r + 1) * hd] = o[r * tq:(r + 1) * tq, :]


def _slc_attention(za, ks, vs, sel):
    S = za.shape[0]
    G = ks.shape[0]
    ns = sel.shape[2]
    tq, hd = SLC_TQ, NSA_HEAD_DIM
    tk = min(NSA_TK, S)
    ksel = min(LANE, ns)
    own_block = (jnp.arange(S)[:, None] // SLC_BLOCK) % ksel == jnp.arange(ksel)[None, :]
    kx = jnp.concatenate([ks, jnp.zeros((G, S, LANE - hd), ks.dtype),
                          jnp.broadcast_to(jnp.where(own_block, -SLC_MASK, 0.0).astype(ks.dtype),
                                           (G, S, ksel))], axis=2)
    gw = NSA_R * hd
    return pl.pallas_call(
        functools.partial(_slc_kernel, tk=tk),
        grid=(G, S // tq),
        in_specs=[pl.BlockSpec((tq, gw), lambda g, i: (i, g)),
                  pl.BlockSpec((1, S, LANE + ksel), lambda g, i: (g, 0, 0)),
                  pl.BlockSpec((1, S, LANE), lambda g, i: (g, 0, 0)),
                  pl.BlockSpec((1, tq, ns), lambda g, i: (g, i, 0))],
        out_specs=pl.BlockSpec((tq, gw), lambda g, i: (i, g)),
        out_shape=jax.ShapeDtypeStruct((S, BRANCH_W), F32),
        scratch_shapes=[pltpu.VMEM((ns // ksel, NSA_R * tq, LANE + ksel), MX),
                        pltpu.VMEM((NSA_R * tq, LANE), F32), pltpu.VMEM((NSA_R * tq, LANE), F32)],
        compiler_params=_params("arbitrary", "arbitrary"),
        name="nsa_selected",
    )(za, kx, vs, sel)


def _win_kernel(q_ref, kw_ref, vw_ref, gate_ref, ocmp_ref, oslc_ref, out_ref):
    i = pl.program_id(1)
    tq, hd = WIN_TQ, NSA_HEAD_DIM
    span = tq + WIN
    qg = _group_queries(q_ref, 0, LOG2E)
    r0 = pl.multiple_of(i * tq, tq)
    t = i * tq + lax.broadcasted_iota(jnp.int32, (tq, 1), 0)
    pos = i * tq - WIN + lax.broadcasted_iota(jnp.int32, (1, span), 1)
    ok = (pos <= t) & (pos > t - WIN) & (pos >= 0)
    s = lax.dot_general(qg, kw_ref[0, pl.ds(r0, span), :], NT_DIMS, preferred_element_type=F32)
    s = s + jnp.concatenate([jnp.where(ok, 0.0, NEG_INF)] * NSA_R, axis=0)
    e = jnp.exp2(s - jnp.max(s, axis=-1, keepdims=True))
    acc = jnp.dot(e.astype(MX), vw_ref[0, pl.ds(r0, span), :], preferred_element_type=F32)
    ow = acc[:, :hd] / acc[:, hd:]
    sig = _sigmoid(gate_ref[...])
    for r in range(NSA_R):
        cols = slice(r * hd, (r + 1) * hd)
        out_ref[:, cols] = (sig[:, r:r + 1] * ocmp_ref[:, cols]
                            + sig[:, NSA_R + r:NSA_R + r + 1] * oslc_ref[:, cols]
                            + sig[:, 2 * NSA_R + r:2 * NSA_R + r + 1] * ow[r * tq:(r + 1) * tq, :]
                            ).astype(MX)


def _win_combine(za, kw, vw, ocmp, oslc):
    S = za.shape[0]
    G = kw.shape[0]
    tq, hd = WIN_TQ, NSA_HEAD_DIM
    gw = NSA_R * hd
    blk = lambda: pl.BlockSpec((tq, gw), lambda g, i: (i, g))
    return pl.pallas_call(
        _win_kernel,
        grid=(G, S // tq),
        in_specs=[blk(),
                  pl.BlockSpec((1, S + WIN, hd), lambda g, i: (g, 0, 0)),
                  pl.BlockSpec((1, S + WIN, LANE), lambda g, i: (g, 0, 0)),
                  pl.BlockSpec((tq, GATE_PAD), lambda g, i: (i, ZA_G_OFF // GATE_PAD + g)),
                  blk(), blk()],
        out_specs=blk(),
        out_shape=jax.ShapeDtypeStruct((S, BRANCH_W), MX),
        compiler_params=_params("arbitrary", "arbitrary"),
        name="nsa_window_combine",
    )(za, kw, vw, za, ocmp, oslc)


def _nsa_mixer(za, w_cmp_k, w_cmp_v):
    S = za.shape[0]
    G, hd = NSA_KV_GROUPS, NSA_HEAD_DIM
    nc = S // CMP_STRIDE
    kvz = lax.optimization_barrier(za[:, ZA_KV_OFF:ZA_KV_OFF + NSA_KV_COLS].astype(MX))

    def kv(idx):
        off = idx * G * hd
        return jnp.stack([kvz[:, off + g * hd:off + (g + 1) * hd] for g in range(G)])

    def blocks(x):
        half = x.reshape(G, nc, CMP_STRIDE * hd)
        nxt = jnp.concatenate([half[:, 1:], jnp.zeros_like(half[:, :1])], axis=1)
        return jnp.concatenate([half, nxt], axis=2).astype(MX)

    def with_ones(x):
        return jnp.concatenate([x, jnp.ones((G, S, LANE - hd), x.dtype)], axis=2).astype(MX)

    front = lambda x: jnp.pad(x, ((0, 0), (WIN, 0), (0, 0)))
    wk = w_cmp_k.reshape(CMP_LEN * hd, hd).astype(MX)
    wvt = jnp.pad(w_cmp_v.reshape(CMP_LEN * hd, hd).T, ((0, LANE - hd), (0, 0))).astype(MX)
    kc = _compress(blocks(kv(0)), wk, transposed=False)
    vct = _compress(blocks(kv(1)), wvt, transposed=True)
    ocmp, sel = _cmp_select(za, kc, vct)
    oslc = _slc_attention(za, kv(2).astype(MX), with_ones(kv(3)), sel)
    return _win_combine(za, front(kv(4).astype(MX)), front(with_ones(kv(5))), ocmp, oslc)


def _pack_w_in(w):
    seg = lambda off, n: w[:, off:off + n]
    R = NSA_HEADS // NSA_KV_GROUPS
    gates = []
    for g in range(NSA_KV_GROUPS):
        cols = [OFF_G + br * NSA_HEADS + g * R + r for br in range(3) for r in range(R)]
        gates.append(jnp.pad(w[:, jnp.array(cols)], ((0, 0), (0, GATE_PAD - len(cols)))))
    wa = jnp.concatenate([seg(OFF_Q, BRANCH_W), seg(OFF_S5, BRANCH_W), seg(OFF_POOL, BRANCH_W),
                          seg(OFF_RQ, BRANCH_W), seg(OFF_RK, BRANCH_W), seg(OFF_RV, BRANCH_W),
                          seg(OFF_RG, BRANCH_W), seg(OFF_KV, NSA_KV_COLS)] + gates, axis=1)
    return wa.astype(MX), w[:, OFF_MERGE:].astype(MX)


def kernel(x, p, g_mix, w_in, w_cmp_k, w_cmp_v, s5_a_re, s5_a_im, s5_log_dt, s5_b_re, s5_b_im,
           s5_c_re, s5_c_im, s5_d, s5_w_glu, pool_w, pool_scale, w_branch, w_out, g_mlp,
           w_mlp_up, w_mlp_down, w_ple_gate, w_ple_proj, g_final):
    B, S, D = x.shape
    assert B == 1 and D == D_MODEL
    depth = w_in.shape[0]
    h = x.reshape(S, D)
    for i in range(depth):
        wa, wm = _pack_w_in(w_in[i])
        za, u = _inproj(h, g_mix[i].reshape(1, D), wa, tm=1024, tn=1536)

        o_nsa = _nsa_mixer(za, w_cmp_k[i], w_cmp_v[i])
        s5_tabs = _s5_tables(s5_a_re[i], s5_a_im[i], s5_log_dt[i], s5_b_re[i], s5_b_im[i],
                             s5_c_re[i], s5_c_im[i])
        o_s5 = _s5_mixer(za, s5_tabs, s5_d[i], s5_w_glu[i].astype(MX), tm=512)
        o_pool = _pool_mixer(za, pool_w[i].astype(MX), pool_scale[i], tm=1024)
        o_ret = _ret_mixer(za)

        h = _merge_out(h, u, (o_nsa, o_s5, o_pool, o_ret), wm, w_branch[i].astype(MX),
                       w_out[i].astype(MX), tm=512, tn=512)
        h = _mlp(h, g_mlp[i].reshape(1, D), w_mlp_up, w_mlp_down, i, tm=1024, tf=512)
        h = _ple(h, p[i].reshape(S, PLE_DIM), w_ple_gate[i].astype(MX),
                 w_ple_proj[i].astype(MX), g_final.reshape(1, D), tm=512,
                 final_norm=(i == depth - 1))
    return h.reshape(B, S, D)
```

```python
import functools
import math

import jax
import jax.numpy as jnp
import numpy as np
from jax import lax
from jax.experimental import pallas as pl
from jax.experimental.pallas import tpu as pltpu

F32 = jnp.float32
BF16 = jnp.bfloat16
MX = BF16

D_MODEL = 2048
N_BRANCH = 4
BRANCH_W = D_MODEL // N_BRANCH
NSA_HEADS = 8
NSA_KV_GROUPS = 2
NSA_HEAD_DIM = BRANCH_W // NSA_HEADS
CMP_LEN = 32
CMP_STRIDE = 16
SLC_BLOCK = 64
SLC_TOP_N = 16
WIN = 512
Q_BLOCK = 128
S5_GROUP = 16
S5_GROUPS = BRANCH_W // S5_GROUP
S5_STATE = 64
S5_COLS = S5_GROUPS * S5_STATE
POOL_WINDOWS = (2, 4, 8, 16)
POOL_GROUP = BRANCH_W // len(POOL_WINDOWS)
POOL_HALO = 16
RET_HEADS = 4
RET_HEAD_DIM = BRANCH_W // RET_HEADS
RET_CHUNK = 128
D_FF = 4 * D_MODEL
PLE_DIM = 256
NORM_EPS = 1e-6
NEG_INF = -1e30
FORCE_BONUS = 1e4

NSA_KV_COLS = 6 * NSA_KV_GROUPS * NSA_HEAD_DIM
NSA_GATE_COLS = 3 * NSA_HEADS
OFF_Q = 0
OFF_KV = OFF_Q + BRANCH_W
OFF_G = OFF_KV + NSA_KV_COLS
OFF_S5 = OFF_G + NSA_GATE_COLS
OFF_POOL = OFF_S5 + BRANCH_W
OFF_RQ = OFF_POOL + BRANCH_W
OFF_RK = OFF_RQ + BRANCH_W
OFF_RV = OFF_RK + BRANCH_W
OFF_RG = OFF_RV + BRANCH_W
OFF_MERGE = OFF_RG + BRANCH_W

LANE = 128
SUBLANE = 8
GATE_PAD = LANE
ZA_Q, ZA_S5, ZA_POOL, ZA_RQ, ZA_RK, ZA_RV, ZA_RG = range(7)
ZA_KV_OFF = 7 * BRANCH_W
ZA_G_OFF = ZA_KV_OFF + NSA_KV_COLS
ZA_COLS = ZA_G_OFF + NSA_KV_GROUPS * GATE_PAD

VMEM_LIMIT = 56 * 1024 * 1024


def _params(*sem):
    return pltpu.CompilerParams(dimension_semantics=sem, vmem_limit_bytes=VMEM_LIMIT)


def _rms(x, g):
    return x * lax.rsqrt(jnp.mean(x * x, axis=-1, keepdims=True) + NORM_EPS) * g


def _sigmoid(x):
    return 1.0 / (1.0 + jnp.exp(-x))


def _norm_kernel(h_ref, g_ref, u_ref):
    u_ref[...] = _rms(h_ref[...], g_ref[...]).astype(MX)


def _norm_cast(h, g, *, tm):
    S, D = h.shape
    return pl.pallas_call(
        _norm_kernel,
        grid=(S // tm,),
        in_specs=[pl.BlockSpec((tm, D), lambda i: (i, 0)), pl.BlockSpec((1, D), lambda i: (0, 0))],
        out_specs=pl.BlockSpec((tm, D), lambda i: (i, 0)),
        out_shape=jax.ShapeDtypeStruct((S, D), MX),
        compiler_params=_params("parallel"),
        name="norm_cast",
    )(h, g)


def _inproj_kernel(u_ref, w_ref, z_ref):
    z_ref[...] = jnp.dot(u_ref[...], w_ref[...], preferred_element_type=F32)


def _inproj(u, w, *, tm, tn):
    S, D = u.shape
    N = w.shape[1]
    return pl.pallas_call(
        _inproj_kernel,
        grid=(N // tn, S // tm),
        in_specs=[pl.BlockSpec((tm, D), lambda j, i: (i, 0)),
                  pl.BlockSpec((D, tn), lambda j, i: (0, j))],
        out_specs=pl.BlockSpec((tm, tn), lambda j, i: (i, j)),
        out_shape=jax.ShapeDtypeStruct((S, N), F32),
        compiler_params=_params("parallel", "arbitrary"),
        name="inproj",
    )(u, w)


def _merge_kernel(u_ref, o0_ref, o1_ref, o2_ref, o3_ref, wm0_ref, wm1_ref, wm2_ref, wm3_ref,
                  wb_ref, wo_ref, h_ref, out_ref):
    u = u_ref[...]
    acc = None
    for j, (o_ref, wm_ref) in enumerate(((o0_ref, wm0_ref), (o1_ref, wm1_ref),
                                         (o2_ref, wm2_ref), (o3_ref, wm3_ref))):
        gate = _sigmoid(jnp.dot(u, wm_ref[...], preferred_element_type=F32))
        br = jnp.dot(o_ref[...], wb_ref[j], preferred_element_type=F32)
        acc = gate * br if acc is None else acc + gate * br
    @pl.when(pl.program_id(1) == 0)
    def _():
        out_ref[...] = h_ref[...]

    out_ref[...] += jnp.dot(acc.astype(MX), wo_ref[...], preferred_element_type=F32)


def _merge_out(h, u, outs, wm, wb, wo, *, tm, tn):
    S, D = u.shape
    nt = D // tn
    o_specs = [pl.BlockSpec((tm, BRANCH_W), lambda i, c: (i, 0)) for _ in range(N_BRANCH)]
    wm_specs = [pl.BlockSpec((D, tn), functools.partial(lambda i, c, j: (0, j * nt + c), j=j))
                for j in range(N_BRANCH)]
    return pl.pallas_call(
        _merge_kernel,
        grid=(S // tm, nt),
        in_specs=[pl.BlockSpec((tm, D), lambda i, c: (i, 0))] + o_specs + wm_specs
                 + [pl.BlockSpec((N_BRANCH, BRANCH_W, tn), lambda i, c: (0, 0, c)),
                    pl.BlockSpec((tn, D), lambda i, c: (c, 0)),
                    pl.BlockSpec((tm, D), lambda i, c: (i, 0))],
        out_specs=pl.BlockSpec((tm, D), lambda i, c: (i, 0)),
        out_shape=jax.ShapeDtypeStruct((S, D), F32),
        compiler_params=_params("parallel", "arbitrary"),
        name="merge_out",
    )(u, *outs, wm, wm, wm, wm, wb, wo, h)


def _mlp_kernel(h_ref, g_ref, wu_ref, wd_ref, out_ref, v_ref):
    @pl.when(pl.program_id(1) == 0)
    def _():
        h = h_ref[...]
        v_ref[...] = _rms(h, g_ref[...]).astype(MX)
        out_ref[...] = h

    hid = jnp.maximum(jnp.dot(v_ref[...], wu_ref[...].astype(MX), preferred_element_type=F32), 0.0)
    out_ref[...] += jnp.dot((hid * hid).astype(MX), wd_ref[...].astype(MX),
                            preferred_element_type=F32)


def _mlp(h, g, wu, wd, layer, *, tm, tf):
    S, D = h.shape
    return pl.pallas_call(
        _mlp_kernel,
        grid=(S // tm, wu.shape[2] // tf),
        in_specs=[pl.BlockSpec((tm, D), lambda i, f: (i, 0)),
                  pl.BlockSpec((1, D), lambda i, f: (0, 0)),
                  pl.BlockSpec((None, D, tf), lambda i, f: (layer, 0, f)),
                  pl.BlockSpec((None, tf, D), lambda i, f: (layer, f, 0))],
        out_specs=pl.BlockSpec((tm, D), lambda i, f: (i, 0)),
        out_shape=jax.ShapeDtypeStruct((S, D), F32),
        scratch_shapes=[pltpu.VMEM((tm, D), MX)],
        compiler_params=_params("parallel", "arbitrary"),
        name="mlp",
    )(h, g, wu, wd)


def _ple_kernel(h_ref, p_ref, wg_ref, wp_ref, gn_ref, out_ref, *rest, final_norm):
    h = h_ref[...]
    gate = _sigmoid(jnp.dot(h.astype(MX), wg_ref[...], preferred_element_type=F32))
    emb = jnp.dot(p_ref[...].astype(MX), wp_ref[...], preferred_element_type=F32)
    y = h + gate * emb
    if final_norm:
        out_ref[...] = _rms(y, gn_ref[...])
    else:
        out_ref[...] = y
        rest[0][...] = _rms(y, gn_ref[...]).astype(MX)


def _ple(h, p, wg, wp, gn, *, tm, final_norm):
    S, D = h.shape
    P = p.shape[1]
    row = pl.BlockSpec((tm, D), lambda i: (i, 0))
    out_specs, out_shape = row, jax.ShapeDtypeStruct((S, D), F32)
    if not final_norm:
        out_specs, out_shape = [row, row], [out_shape, jax.ShapeDtypeStruct((S, D), MX)]
    return pl.pallas_call(
        functools.partial(_ple_kernel, final_norm=final_norm),
        grid=(S // tm,),
        in_specs=[row,
                  pl.BlockSpec((tm, P), lambda i: (i, 0)),
                  pl.BlockSpec((D, D), lambda i: (0, 0)),
                  pl.BlockSpec((P, D), lambda i: (0, 0)),
                  pl.BlockSpec((1, D), lambda i: (0, 0))],
        out_specs=out_specs,
        out_shape=out_shape,
        compiler_params=_params("parallel"),
        name="ple",
    )(h, p, wg, wp, gn)


S5_PANELS = BRANCH_W // LANE
S5_PANEL_COLS = S5_COLS // S5_PANELS
T_A0, T_A1, T_A2, T_A4 = range(4)


def _cmul_add(xr, xi, ar, ai, br, bi):
    return xr + (ar * br - ai * bi), xi + (ar * bi + ai * br)


def _s5_kernel(u_ref, bre_ref, bim_ref, tab_ref, cre_ref, cim_ref, d_ref, wglu_ref, out_ref,
               hre_ref, him_ref, carry_ref):
    tm = u_ref.shape[0]

    @pl.when(pl.program_id(0) == 0)
    def _():
        carry_ref[...] = jnp.zeros_like(carry_ref)

    u = u_ref[...]
    ub = u.astype(MX)
    for k in range(S5_PANELS):
        uk = ub[:, k * LANE:(k + 1) * LANE]
        cols = slice(k * S5_PANEL_COLS, (k + 1) * S5_PANEL_COLS)
        hre_ref[:, cols] = jnp.dot(uk, bre_ref[k], preferred_element_type=F32)
        him_ref[:, cols] = jnp.dot(uk, bim_ref[k], preferred_element_type=F32)

    def body(r, carry):
        rows = pl.ds(pl.multiple_of(r * SUBLANE, SUBLANE), SUBLANE)
        for cb in range(S5_COLS // LANE):
            cols = slice(cb * LANE, (cb + 1) * LANE)
            xr, xi = _cmul_add(hre_ref[rows, cols], him_ref[rows, cols],
                               tab_ref[T_A0, 0, :, cols], tab_ref[T_A0, 1, :, cols],
                               carry_ref[0, :, cols], carry_ref[1, :, cols])
            for k, sh in ((T_A1, 1), (T_A2, 2), (T_A4, 4)):
                xr, xi = _cmul_add(xr, xi, tab_ref[k, 0, :, cols], tab_ref[k, 1, :, cols],
                                   pltpu.roll(xr, sh, 0), pltpu.roll(xi, sh, 0))
            hre_ref[rows, cols] = xr
            him_ref[rows, cols] = xi
            carry_ref[0, :, cols] = jnp.broadcast_to(xr[SUBLANE - 1:SUBLANE, :], xr.shape)
            carry_ref[1, :, cols] = jnp.broadcast_to(xi[SUBLANE - 1:SUBLANE, :], xi.shape)
        return carry

    lax.fori_loop(0, tm // SUBLANE, body, 0)

    ys = []
    for k in range(S5_PANELS):
        cols = slice(k * S5_PANEL_COLS, (k + 1) * S5_PANEL_COLS)
        ys.append(jnp.dot(hre_ref[:, cols].astype(MX), cre_ref[k], preferred_element_type=F32)
                  + jnp.dot(him_ref[:, cols].astype(MX), cim_ref[k], preferred_element_type=F32))
    y = jnp.concatenate(ys, axis=1) + d_ref[...] * u
    y = 0.5 * y * (1.0 + jnp.tanh(math.sqrt(2.0 / math.pi) * (y + 0.044715 * (y * y * y))))
    gl = jnp.dot(y.astype(MX), wglu_ref[...], preferred_element_type=F32)
    out_ref[...] = (gl[:, :BRANCH_W] * _sigmoid(gl[:, BRANCH_W:])).astype(MX)


def _s5_tables(a_re, a_im, log_dt, b_re, b_im, c_re, c_im):
    dt = jnp.exp(log_dt)[:, None]
    xr, xi = a_re * dt, a_im * dt

    pr = jnp.exp(xr) * jnp.cos(xi)
    pi = jnp.exp(xr) * jnp.sin(xi)
    den = a_re * a_re + a_im * a_im
    qr = ((pr - 1.0) * a_re + pi * a_im) / den
    qi = (pi * a_re - (pr - 1.0) * a_im) / den
    bbar_re = qr[..., None] * b_re - qi[..., None] * b_im
    bbar_im = qr[..., None] * b_im + qi[..., None] * b_re

    flat = lambda z: z.reshape(1, S5_COLS)
    powers = [(flat(pr), flat(pi))]
    for _ in range(3):
        zr, zi = powers[-1]
        powers.append((zr * flat(pr) - zi * flat(pi), zr * flat(pi) + zi * flat(pr)))
    row = jnp.arange(SUBLANE)[:, None]
    planes = [(jnp.where(row == 0, flat(pr), 0.0), jnp.where(row == 0, flat(pi), 0.0))]
    for sh in (1, 2, 4):
        zr, zi = powers[sh - 1]
        planes.append((jnp.where(row >= sh, zr, 0.0), jnp.where(row >= sh, zi, 0.0)))
    tab = jnp.stack([jnp.stack(p) for p in planes]).astype(F32)

    pg = S5_GROUPS // S5_PANELS
    eye = jnp.eye(pg, dtype=F32)

    def bdiag_in(m):
        m = m.reshape(S5_PANELS, pg, S5_STATE, S5_GROUP)
        return jnp.einsum('pgnc,gh->pgchn', m, eye).reshape(S5_PANELS, LANE, S5_PANEL_COLS)

    def bdiag_out(m):
        m = m.reshape(S5_PANELS, pg, S5_GROUP, S5_STATE)
        return jnp.einsum('pgcn,gh->pgnhc', m, eye).reshape(S5_PANELS, S5_PANEL_COLS, LANE)

    return (tab, bdiag_in(bbar_re).astype(MX), bdiag_in(bbar_im).astype(MX),
            bdiag_out(c_re).astype(MX), bdiag_out(-c_im).astype(MX))


def _s5_mixer(za, tabs, d_skip, w_glu, *, tm):
    S = za.shape[0]
    tab, bre, bim, cre, cim = tabs
    full = lambda shape: pl.BlockSpec(shape, lambda i: (0,) * len(shape))
    return pl.pallas_call(
        _s5_kernel,
        grid=(S // tm,),
        in_specs=[pl.BlockSpec((tm, BRANCH_W), lambda i: (i, ZA_S5)),
                  full(bre.shape), full(bim.shape), full(tab.shape), full(cre.shape),
                  full(cim.shape), full((1, BRANCH_W)), full(w_glu.shape)],
        out_specs=pl.BlockSpec((tm, BRANCH_W), lambda i: (i, 0)),
        out_shape=jax.ShapeDtypeStruct((S, BRANCH_W), MX),
        scratch_shapes=[pltpu.VMEM((tm, S5_COLS), F32), pltpu.VMEM((tm, S5_COLS), F32),
                        pltpu.VMEM((2, SUBLANE, S5_COLS), F32)],
        compiler_params=_params("arbitrary"),
        name="s5",
    )(za, bre, bim, tab, cre, cim, d_skip.reshape(1, BRANCH_W), w_glu)


def _pool_kernel(x_ref, w_ref, scale_ref, out_ref, ext_ref):
    tm = x_ref.shape[0]
    i = pl.program_id(0)

    @pl.when(i == 0)
    def _():
        ext_ref[0:POOL_HALO, :] = jnp.zeros((POOL_HALO, BRANCH_W), F32)

    x = x_ref[...]
    ext_ref[POOL_HALO:, :] = x
    t = i * tm + lax.broadcasted_iota(jnp.int32, (tm, 1), 0)
    for gi, w in enumerate(POOL_WINDOWS):
        cols = slice(gi * POOL_GROUP, (gi + 1) * POOL_GROUP)
        acc = x[:, cols]
        for d in range(1, w):
            acc = acc + ext_ref[POOL_HALO - d:POOL_HALO - d + tm, cols]
        count = jnp.minimum(t + 1, w).astype(F32)
        diff = (acc / count - x[:, cols]).astype(MX)
        y = jnp.dot(diff, w_ref[gi], preferred_element_type=F32)
        out_ref[:, cols] = (y * scale_ref[:, cols]).astype(MX)
    ext_ref[0:POOL_HALO, :] = x[tm - POOL_HALO:, :]


def _pool_mixer(za, w_pool, pool_scale, *, tm):
    S = za.shape[0]
    return pl.pallas_call(
        _pool_kernel,
        grid=(S // tm,),
        in_specs=[pl.BlockSpec((tm, BRANCH_W), lambda i: (i, ZA_POOL)),
                  pl.BlockSpec(w_pool.shape, lambda i: (0, 0, 0)),
                  pl.BlockSpec((1, BRANCH_W), lambda i: (0, 0))],
        out_specs=pl.BlockSpec((tm, BRANCH_W), lambda i: (i, 0)),
        out_shape=jax.ShapeDtypeStruct((S, BRANCH_W), MX),
        scratch_shapes=[pltpu.VMEM((tm + POOL_HALO, BRANCH_W), F32)],
        compiler_params=_params("arbitrary"),
        name="pool",
    )(za, w_pool, pool_scale.reshape(1, BRANCH_W))


def _rotary_tables(S):
    dh, C = RET_HEAD_DIM, RET_CHUNK
    angle = np.repeat(1.0 / (10000.0 ** np.linspace(0.0, 1.0, dh // 2)), 2)[None, :]
    within = np.arange(C, dtype=np.float64)[:, None] * angle
    start = (np.arange(S // C, dtype=np.float64) * C)[:, None] * angle
    tab = lambda a: jnp.asarray(a, dtype=F32)
    return (tab(np.cos(within)), tab(np.sin(within)),
            tab(np.cos(start)[:, None, :]), tab(np.sin(start)[:, None, :]))


def _ret_consts():
    H, dh, C = RET_HEADS, RET_HEAD_DIM, RET_CHUNK
    log_g = jnp.log(1.0 - 2.0 ** (-5.0 - jnp.arange(H, dtype=F32)))
    idx = jnp.arange(C, dtype=F32)
    rel = idx[:, None] - idx[None, :]
    decay = jnp.where(rel >= 0, jnp.exp(jnp.maximum(rel, 0.0)[None] * log_g[:, None, None]), 0.0)
    xi = jnp.exp((idx + 1.0)[None, :] * log_g[:, None])[..., None]
    zeta = jnp.exp((C - 1.0 - idx)[None, :] * log_g[:, None])[..., None]
    g_chunk = jnp.broadcast_to(jnp.exp(C * log_g)[:, None, None], (H, 1, dh))
    return decay, xi, zeta, g_chunk


def _rotate_pairs(x, even):
    n = x.shape[-1]
    return jnp.where(even, -pltpu.roll(x, n - 1, 1), pltpu.roll(x, 1, 1))


def _ret_kernel(q_ref, k_ref, v_ref, g_ref, ca_ref, sa_ref, cb_ref, sb_ref, decay_ref, xi_ref,
                zeta_ref, gch_ref, out_ref, state_ref):
    C, dh = RET_CHUNK, RET_HEAD_DIM

    @pl.when(pl.program_id(0) == 0)
    def _():
        state_ref[...] = jnp.zeros_like(state_ref)

    ca, sa, cb, sb = ca_ref[...], sa_ref[...], cb_ref[0], sb_ref[0]
    cos = ca * cb - sa * sb
    sin = sa * cb + ca * sb
    even = (lax.broadcasted_iota(jnp.int32, (C, dh), 1) % 2) == 0
    for hd in range(RET_HEADS):
        cols = slice(hd * dh, (hd + 1) * dh)
        q = q_ref[:, cols]
        k = k_ref[:, cols]
        vb = v_ref[:, cols].astype(MX)
        q = q * cos + _rotate_pairs(q, even) * sin
        k = (k * cos + _rotate_pairs(k, even) * sin) * dh ** -0.5
        qb = q.astype(MX)
        s = lax.dot_general(qb, k.astype(MX), (((1,), (1,)), ((), ())),
                            preferred_element_type=F32) * decay_ref[hd]
        inner = jnp.dot(s.astype(MX), vb, preferred_element_type=F32)
        state = state_ref[hd]
        cross = jnp.dot(qb, state.astype(MX), preferred_element_type=F32) * xi_ref[hd]
        kz = (k * zeta_ref[hd]).astype(MX)
        state_ref[hd] = state * gch_ref[hd] + lax.dot_general(
            kz, vb, (((0,), (0,)), ((), ())), preferred_element_type=F32)
        y = inner + cross
        y = y * lax.rsqrt(jnp.mean(y * y, axis=-1, keepdims=True) + NORM_EPS)
        g = g_ref[:, cols]
        out_ref[:, cols] = (g * _sigmoid(g) * y).astype(MX)


def _ret_mixer(za):
    S = za.shape[0]
    C = RET_CHUNK
    ca, sa, cb, sb = _rotary_tables(S)
    consts = (ca, sa, cb, sb) + _ret_consts()
    col = lambda c: pl.BlockSpec((C, BRANCH_W), lambda i: (i, c))
    full = lambda a: pl.BlockSpec(a.shape, lambda i: (0,) * a.ndim)
    start = lambda a: pl.BlockSpec((1,) + a.shape[1:], lambda i: (i, 0, 0))
    return pl.pallas_call(
        _ret_kernel,
        grid=(S // C,),
        in_specs=[col(ZA_RQ), col(ZA_RK), col(ZA_RV), col(ZA_RG), full(ca), full(sa), start(cb),
                  start(sb)] + [full(a) for a in consts[4:]],
        out_specs=pl.BlockSpec((C, BRANCH_W), lambda i: (i, 0)),
        out_shape=jax.ShapeDtypeStruct((S, BRANCH_W), MX),
        scratch_shapes=[pltpu.VMEM((RET_HEADS, RET_HEAD_DIM, RET_HEAD_DIM), F32)],
        compiler_params=_params("arbitrary"),
        name="retention",
    )(za, za, za, za, *consts)


NSA_R = NSA_HEADS // NSA_KV_GROUPS
NSA_TQ = 512
SLC_TQ = 512
WIN_TQ = 256
NSA_TK = 512
CMP_CH = 256
P_OFF = SUBLANE
NT_DIMS = (((1,), (1,)), ((), ()))


def _compress_kernel(x_ref, w_ref, out_ref):
    out_ref[0] = jnp.dot(x_ref[0], w_ref[...], preferred_element_type=F32).astype(out_ref.dtype)


def _compress_t_kernel(x_ref, wt_ref, out_ref):
    out_ref[0] = lax.dot_general(wt_ref[...], x_ref[0], NT_DIMS,
                                 preferred_element_type=F32).astype(out_ref.dtype)


def _compress(x, w, *, transposed):
    G, nc, kk = x.shape
    if transposed:
        kern, out_block = _compress_t_kernel, (1, w.shape[0], nc)
    else:
        kern, out_block = _compress_kernel, (1, nc, w.shape[1])
    return pl.pallas_call(
        kern,
        grid=(G,),
        in_specs=[pl.BlockSpec((1, nc, kk), lambda g: (g, 0, 0)),
                  pl.BlockSpec(w.shape, lambda g: (0, 0))],
        out_specs=pl.BlockSpec(out_block, lambda g: (g, 0, 0)),
        out_shape=jax.ShapeDtypeStruct((G,) + out_block[1:], x.dtype),
        compiler_params=_params("arbitrary"),
        name="nsa_compress",
    )(x, w)


def _group_queries(q_ref, g, extra_scale=1.0):
    hd = NSA_HEAD_DIM
    base = g * NSA_R * hd
    qs = [q_ref[:, base + r * hd:base + (r + 1) * hd] for r in range(NSA_R)]
    return (jnp.concatenate(qs, axis=0) * (hd ** -0.5 * extra_scale)).astype(MX)


def _cmp_select_kernel(q_ref, kc_ref, vct_ref, ocmp_ref, sel_ref, s_sc, p_sc, ot_sc):
    i = pl.program_id(0)
    tq, hd = NSA_TQ, NSA_HEAD_DIM
    nc = kc_ref.shape[1]
    ns = sel_ref.shape[2]
    ch = min(CMP_CH, nc)
    nch = jnp.minimum((tq // CMP_STRIDE * (i + 1) + ch - 1) // ch, nc // ch)
    t4 = i * tq + lax.broadcasted_iota(jnp.int32, (1, NSA_R * tq), 1) % tq
    t1 = i * tq + lax.broadcasted_iota(jnp.int32, (1, tq), 1)
    nrow = lax.broadcasted_iota(jnp.int32, (ch, 1), 0)

    def rows(c):
        return pl.ds(pl.multiple_of(c * ch, ch), ch)

    def visible(c):
        return ((c * ch + nrow) * CMP_STRIDE + (CMP_LEN - 1)) <= t4

    groups = range(NSA_KV_GROUPS)
    qgs = [_group_queries(q_ref, g) for g in groups]
    lanes = (1, NSA_R * tq)

    def scores(c, ms):
        vis = visible(c)
        out = []
        for g in groups:
            s = lax.dot_general(kc_ref[g, rows(c), :], qgs[g], NT_DIMS, preferred_element_type=F32)
            s = jnp.where(vis, s, NEG_INF)
            s_sc[g, rows(c), :] = s
            out.append(jnp.maximum(ms[g], jnp.max(s, axis=0, keepdims=True)))
        return tuple(out)

    ms = lax.fori_loop(0, nch, scores, tuple(jnp.full(lanes, NEG_INF, F32) for _ in groups))

    def expsum(c, ls):
        vis = visible(c)
        out = []
        for g in groups:
            e = jnp.where(vis, jnp.exp(s_sc[g, rows(c), :] - ms[g]), 0.0)
            s_sc[g, rows(c), :] = e
            out.append(ls[g] + jnp.sum(e, axis=0, keepdims=True))
        return tuple(out)

    ls = lax.fori_loop(0, nch, expsum, tuple(jnp.zeros(lanes, F32) for _ in groups))
    invs = [1.0 / jnp.maximum(l, 1e-30) for l in ls]
    p_sc[...] = jnp.zeros_like(p_sc)
    ot_sc[...] = jnp.zeros_like(ot_sc)

    def weigh(c, carry):
        for g in groups:
            p = s_sc[g, rows(c), :] * invs[g]
            psum = p[:, 0:tq] + p[:, tq:2 * tq] + p[:, 2 * tq:3 * tq] + p[:, 3 * tq:4 * tq]
            for w in range(tq // LANE):
                p_sc[g, w, pl.ds(pl.multiple_of(P_OFF + c * ch, SUBLANE), ch), :] = (
                    psum[:, w * LANE:(w + 1) * LANE])
            ot_sc[g] += jnp.dot(vct_ref[g, :, rows(c)], p.astype(MX), preferred_element_type=F32)
        return carry

    lax.fori_loop(0, nch, weigh, 0)
    for h in range(NSA_HEADS):
        g, r = divmod(h, NSA_R)
        o = ot_sc[g, :, r * tq:(r + 1) * tq].T
        ocmp_ref[:, h * hd:(h + 1) * hd] = o[:, :hd]

    ratio = SLC_BLOCK // CMP_STRIDE

    def pick(nr):
        jrow = lax.broadcasted_iota(jnp.int32, (nr, 1), 0)
        jfull = jnp.broadcast_to(jrow.astype(F32), (nr, tq))
        valid = jrow * SLC_BLOCK <= t1
        bonus = jnp.where((jrow == 0) | (jrow == t1 // SLC_BLOCK), FORCE_BONUS, 0.0)
        xs = []
        for g in groups:
            slabs = []
            for w in range(tq // LANE):
                imp = p_sc[g, w, pl.ds(P_OFF - 1, nr, stride=ratio), :]
                for k in range(ratio):
                    imp = imp + p_sc[g, w, pl.ds(P_OFF + k, nr, stride=ratio), :]
                slabs.append(imp)
            xs.append(jnp.where(valid, jnp.concatenate(slabs, axis=1) + bonus, NEG_INF))
        for _ in range(min(SLC_TOP_N, ns)):
            for g in groups:
                mx = jnp.max(xs[g], axis=0, keepdims=True)
                cand = jnp.where(xs[g] == mx, jfull, float(nr))
                xs[g] = jnp.where(cand == jnp.min(cand, axis=0, keepdims=True), -jnp.inf, xs[g])
        for g in groups:
            picked = jnp.where(xs[g] == -jnp.inf, 1.0, 0.0).T.astype(BF16)
            if nr < ns:
                picked = jnp.concatenate([picked, jnp.zeros((tq, ns - nr), BF16)], axis=1)
            sel_ref[g] = picked

    half = ns // 2
    if half >= SLC_TOP_N:
        early = (i + 1) * (tq // SLC_BLOCK) <= half

        @pl.when(early)
        def _():
            pick(half)

        @pl.when(jnp.logical_not(early))
        def _():
            pick(ns)
    else:
        pick(ns)


def _cmp_select(za, kc, vct):
    S = za.shape[0]
    G, nc, hd = kc.shape
    ns = S // SLC_BLOCK
    tq = NSA_TQ
    return pl.pallas_call(
        _cmp_select_kernel,
        grid=(S // tq,),
        in_specs=[pl.BlockSpec((tq, BRANCH_W), lambda i: (i, ZA_Q)),
                  pl.BlockSpec(kc.shape, lambda i: (0, 0, 0)),
                  pl.BlockSpec(vct.shape, lambda i: (0, 0, 0))],
        out_specs=[pl.BlockSpec((tq, BRANCH_W), lambda i: (i, 0)),
                   pl.BlockSpec((G, tq, ns), lambda i: (0, i, 0))],
        out_shape=[jax.ShapeDtypeStruct((S, BRANCH_W), F32),
                   jax.ShapeDtypeStruct((G, S, ns), BF16)],
        scratch_shapes=[pltpu.VMEM((G, nc, NSA_R * tq), F32),
                        pltpu.VMEM((G, tq // LANE, P_OFF + nc, LANE), F32),
                        pltpu.VMEM((G, LANE, NSA_R * tq), F32)],
        compiler_params=_params("parallel"),
        name="nsa_cmp_select",
    )(za, kc, vct)


SLC_MASK = 2.0 ** 100
LOG2E = math.log2(math.e)


def _slc_kernel(q_ref, kx_ref, vs_ref, sel_ref, out_ref, lhs_sc, m_sc, acc_sc, *, tk):
    i = pl.program_id(1)
    tq, hd = SLC_TQ, NSA_HEAD_DIM
    ksel = kx_ref.shape[2] - LANE
    bpt = tk // SLC_BLOCK
    qg = _group_queries(q_ref, 0, LOG2E)
    pad = jnp.zeros((NSA_R * tq, LANE - hd), MX)
    for h in range(sel_ref.shape[2] // ksel):
        unsel = (1.0 - sel_ref[0, :, h * ksel:(h + 1) * ksel].astype(F32)).astype(MX)
        lhs_sc[h] = jnp.concatenate([qg, pad, jnp.concatenate([unsel] * NSA_R, axis=0)], axis=1)
    t = i * tq + lax.broadcasted_iota(jnp.int32, (tq, 1), 0)
    lane = lax.broadcasted_iota(jnp.int32, (1, tk), 1)
    m_sc[...] = jnp.full_like(m_sc, NEG_INF)
    acc_sc[...] = jnp.zeros_like(acc_sc)

    def tile(kt, diagonal):
        k0 = pl.multiple_of(kt * tk, tk)
        s = lax.dot_general(lhs_sc[(kt * bpt) // ksel], kx_ref[0, pl.ds(k0, tk), :], NT_DIMS,
                            preferred_element_type=F32)
        if diagonal:
            hidden = jnp.where(k0 + lane <= t, 0.0, NEG_INF)
            s = s + jnp.concatenate([hidden] * NSA_R, axis=0)
        m_prev = m_sc[...]
        m_new = jnp.maximum(m_prev, jnp.max(s, axis=-1, keepdims=True))
        p = jnp.exp2(s - jnp.concatenate([m_new] * (tk // LANE), axis=1))
        acc_sc[...] = jnp.exp2(m_prev - m_new) * acc_sc[...] + jnp.dot(
            p.astype(MX), vs_ref[0, pl.ds(k0, tk), :], preferred_element_type=F32)
        m_sc[...] = m_new

    last = (i * tq) // tk

    def quad(j, carry):
        for u in range(4):
            tile(4 * j + u, False)
        return carry

    lax.fori_loop(0, last // 4, quad, 0)
    done = (last // 4) * 4

    @pl.when(last - done >= 2)
    def _():
        tile(done, False)
        tile(done + 1, False)

    @pl.when((last - done) % 2 == 1)
    def _():
        tile(last - 1, False)

    tile(last, True)
    acc = acc_sc[...]
    o = acc[:, :hd] / acc[:, hd:]
    for r in range(NSA_R):
        out_ref[:, r * hd:(r + 1) * hd] = o[r * tq:(r + 1) * tq, :]


def _slc_attention(za, ks, vs, sel):
    S = za.shape[0]
    G = ks.shape[0]
    ns = sel.shape[2]
    tq, hd = SLC_TQ, NSA_HEAD_DIM
    tk = min(NSA_TK, S)
    ksel = min(LANE, ns)
    own_block = (jnp.arange(S)[:, None] // SLC_BLOCK) % ksel == jnp.arange(ksel)[None, :]
    kx = jnp.concatenate([ks, jnp.zeros((G, S, LANE - hd), ks.dtype),
                          jnp.broadcast_to(jnp.where(own_block, -SLC_MASK, 0.0).astype(ks.dtype),
                                           (G, S, ksel))], axis=2)
    gw = NSA_R * hd
    return pl.pallas_call(
        functools.partial(_slc_kernel, tk=tk),
        grid=(G, S // tq),
        in_specs=[pl.BlockSpec((tq, gw), lambda g, i: (i, g)),
                  pl.BlockSpec((1, S, LANE + ksel), lambda g, i: (g, 0, 0)),
                  pl.BlockSpec((1, S, LANE), lambda g, i: (g, 0, 0)),
                  pl.BlockSpec((1, tq, ns), lambda g, i: (g, i, 0))],
        out_specs=pl.BlockSpec((tq, gw), lambda g, i: (i, g)),
        out_shape=jax.ShapeDtypeStruct((S, BRANCH_W), F32),
        scratch_shapes=[pltpu.VMEM((ns // ksel, NSA_R * tq, LANE + ksel), MX),
                        pltpu.VMEM((NSA_R * tq, LANE), F32), pltpu.VMEM((NSA_R * tq, LANE), F32)],
        compiler_params=_params("arbitrary", "arbitrary"),
        name="nsa_selected",
    )(za, kx, vs, sel)


def _win_kernel(q_ref, kw_ref, vw_ref, gate_ref, ocmp_ref, oslc_ref, out_ref):
    i = pl.program_id(1)
    tq, hd = WIN_TQ, NSA_HEAD_DIM
    span = tq + WIN
    qg = _group_queries(q_ref, 0, LOG2E)
    r0 = pl.multiple_of(i * tq, tq)
    t = i * tq + lax.broadcasted_iota(jnp.int32, (tq, 1), 0)
    pos = i * tq - WIN + lax.broadcasted_iota(jnp.int32, (1, span), 1)
    ok = (pos <= t) & (pos > t - WIN) & (pos >= 0)
    s = lax.dot_general(qg, kw_ref[0, pl.ds(r0, span), :], NT_DIMS, preferred_element_type=F32)
    s = s + jnp.concatenate([jnp.where(ok, 0.0, NEG_INF)] * NSA_R, axis=0)
    e = jnp.exp2(s - jnp.max(s, axis=-1, keepdims=True))
    acc = jnp.dot(e.astype(MX), vw_ref[0, pl.ds(r0, span), :], preferred_element_type=F32)
    ow = acc[:, :hd] / acc[:, hd:]
    sig = _sigmoid(gate_ref[...])
    for r in range(NSA_R):
        cols = slice(r * hd, (r + 1) * hd)
        out_ref[:, cols] = (sig[:, r:r + 1] * ocmp_ref[:, cols]
                            + sig[:, NSA_R + r:NSA_R + r + 1] * oslc_ref[:, cols]
                            + sig[:, 2 * NSA_R + r:2 * NSA_R + r + 1] * ow[r * tq:(r + 1) * tq, :]
                            ).astype(MX)


def _win_combine(za, kw, vw, ocmp, oslc):
    S = za.shape[0]
    G = kw.shape[0]
    tq, hd = WIN_TQ, NSA_HEAD_DIM
    gw = NSA_R * hd
    blk = lambda: pl.BlockSpec((tq, gw), lambda g, i: (i, g))
    return pl.pallas_call(
        _win_kernel,
        grid=(G, S // tq),
        in_specs=[blk(),
                  pl.BlockSpec((1, S + WIN, hd), lambda g, i: (g, 0, 0)),
                  pl.BlockSpec((1, S + WIN, LANE), lambda g, i: (g, 0, 0)),
                  pl.BlockSpec((tq, GATE_PAD), lambda g, i: (i, ZA_G_OFF // GATE_PAD + g)),
                  blk(), blk()],
        out_specs=blk(),
        out_shape=jax.ShapeDtypeStruct((S, BRANCH_W), MX),
        compiler_params=_params("arbitrary", "arbitrary"),
        name="nsa_window_combine",
    )(za, kw, vw, za, ocmp, oslc)


def _nsa_mixer(za, w_cmp_k, w_cmp_v):
    S = za.shape[0]
    G, hd = NSA_KV_GROUPS, NSA_HEAD_DIM
    nc = S // CMP_STRIDE
    kvz = lax.optimization_barrier(za[:, ZA_KV_OFF:ZA_KV_OFF + NSA_KV_COLS].astype(MX))

    def kv(idx):
        off = idx * G * hd
        return jnp.stack([kvz[:, off + g * hd:off + (g + 1) * hd] for g in range(G)])

    def blocks(x):
        half = x.reshape(G, nc, CMP_STRIDE * hd)
        nxt = jnp.concatenate([half[:, 1:], jnp.zeros_like(half[:, :1])], axis=1)
        return jnp.concatenate([half, nxt], axis=2).astype(MX)

    def with_ones(x):
        return jnp.concatenate([x, jnp.ones((G, S, LANE - hd), x.dtype)], axis=2).astype(MX)

    front = lambda x: jnp.pad(x, ((0, 0), (WIN, 0), (0, 0)))
    wk = w_cmp_k.reshape(CMP_LEN * hd, hd).astype(MX)
    wvt = jnp.pad(w_cmp_v.reshape(CMP_LEN * hd, hd).T, ((0, LANE - hd), (0, 0))).astype(MX)
    kc = _compress(blocks(kv(0)), wk, transposed=False)
    vct = _compress(blocks(kv(1)), wvt, transposed=True)
    ocmp, sel = _cmp_select(za, kc, vct)
    oslc = _slc_attention(za, kv(2).astype(MX), with_ones(kv(3)), sel)
    return _win_combine(za, front(kv(4).astype(MX)), front(with_ones(kv(5))), ocmp, oslc)


def _pack_w_in(w):
    seg = lambda off, n: w[:, off:off + n]
    R = NSA_HEADS // NSA_KV_GROUPS
    gates = []
    for g in range(NSA_KV_GROUPS):
        cols = [OFF_G + br * NSA_HEADS + g * R + r for br in range(3) for r in range(R)]
        gates.append(jnp.pad(w[:, jnp.array(cols)], ((0, 0), (0, GATE_PAD - len(cols)))))
    wa = jnp.concatenate([seg(OFF_Q, BRANCH_W), seg(OFF_S5, BRANCH_W), seg(OFF_POOL, BRANCH_W),
                          seg(OFF_RQ, BRANCH_W), seg(OFF_RK, BRANCH_W), seg(OFF_RV, BRANCH_W),
                          seg(OFF_RG, BRANCH_W), seg(OFF_KV, NSA_KV_COLS)] + gates, axis=1)
    return wa.astype(MX), w[:, OFF_MERGE:].astype(MX)


def kernel(x, p, g_mix, w_in, w_cmp_k, w_cmp_v, s5_a_re, s5_a_im, s5_log_dt, s5_b_re, s5_b_im,
           s5_c_re, s5_c_im, s5_d, s5_w_glu, pool_w, pool_scale, w_branch, w_out, g_mlp,
           w_mlp_up, w_mlp_down, w_ple_gate, w_ple_proj, g_final):
    B, S, D = x.shape
    assert B == 1 and D == D_MODEL
    depth = w_in.shape[0]
    h = x.reshape(S, D)
    u = _norm_cast(h, g_mix[0].reshape(1, D), tm=1024)
    for i in range(depth):
        last = i == depth - 1
        wa, wm = _pack_w_in(w_in[i])
        za = _inproj(u, wa, tm=1024, tn=1536)

        o_nsa = _nsa_mixer(za, w_cmp_k[i], w_cmp_v[i])
        s5_tabs = _s5_tables(s5_a_re[i], s5_a_im[i], s5_log_dt[i], s5_b_re[i], s5_b_im[i],
                             s5_c_re[i], s5_c_im[i])
        o_s5 = _s5_mixer(za, s5_tabs, s5_d[i], s5_w_glu[i].astype(MX), tm=512)
        o_pool = _pool_mixer(za, pool_w[i].astype(MX), pool_scale[i], tm=1024)
        o_ret = _ret_mixer(za)

        h = _merge_out(h, u, (o_nsa, o_s5, o_pool, o_ret), wm, w_branch[i].astype(MX),
                       w_out[i].astype(MX), tm=512, tn=512)
        h = _mlp(h, g_mlp[i].reshape(1, D), w_mlp_up, w_mlp_down, i, tm=1024, tf=512)
        gn = g_final if last else g_mix[i + 1]
        res = _ple(h, p[i].reshape(S, PLE_DIM), w_ple_gate[i].astype(MX),
                   w_ple_proj[i].astype(MX), gn.reshape(1, D), tm=512, final_norm=last)
        h, u = (res, None) if last else res
    return h.reshape(B, S, D)
```
